```python
import jax, jax.numpy as jnp
from jax import lax
import numpy as np

D_MODEL = 1024
BATCH = 8
SEQ = 2048
DEPTH = 4
DEC_BATCH = 128
DEC_SEQ = 4
PAST_LEN = 2048
PAGE_SIZE = 128

HEAD_DIM = 64
N_HEADS_A = 8
N_KV_A = 2
N_IDX_HEADS = 4
D_IDX = 64
IDX_TOPK_MAX = 256
N_HEADS_B = 8
N_KV_B = 2
MOBA_BLOCK = 256
MOBA_TOPK_MAX = 3
N_GROUPS = 4
EXPERTS_PER_GROUP = 8
N_EXPERTS = N_GROUPS * EXPERTS_PER_GROUP
D_EXPERT = 256
TOPK_IN_GROUP = 2
MOE_ROW_BLOCK = 128
ROPE_THETA = 10000.0
RMS_EPS = 1e-6
Q_BLOCK_DSA = 128
Q_BLOCK_MOBA = 32

WIDTH_A = N_HEADS_A * HEAD_DIM
KV_WIDTH_A = N_KV_A * HEAD_DIM
WIDTH_B = N_HEADS_B * HEAD_DIM
KV_WIDTH_B = N_KV_B * HEAD_DIM
IN_SPLITS = (WIDTH_A, KV_WIDTH_A, KV_WIDTH_A, N_IDX_HEADS * D_IDX, D_IDX, N_IDX_HEADS,
             WIDTH_B, KV_WIDTH_B, KV_WIDTH_B, D_MODEL, D_MODEL)
D_IN = sum(IN_SPLITS)

kernel_name = 'dsa_moba_hier_moe_decoder_step'


def rms_norm(x, g):
    xf = x.astype(jnp.float32)
    y = xf * lax.rsqrt(jnp.mean(xf * xf, axis=-1, keepdims=True) + RMS_EPS)
    return (y * g.astype(jnp.float32)).astype(x.dtype)


def rope(x, pos):
    half = x.shape[-1] // 2
    inv = jnp.power(ROPE_THETA, -jnp.arange(half, dtype=jnp.float32) / half)
    ang = pos.astype(jnp.float32)[:, None] * inv[None, :]
    cos = jnp.cos(ang)[None, :, None, :]
    sin = jnp.sin(ang)[None, :, None, :]
    xf = x.astype(jnp.float32)
    x1, x2 = xf[..., :half], xf[..., half:]
    return jnp.concatenate([x1 * cos - x2 * sin, x2 * cos + x1 * sin], axis=-1).astype(x.dtype)


def sweep_queries(fn, qargs, block):
    tq = qargs[0].shape[1]
    if tq <= block or tq % block != 0:
        return fn(*qargs)
    nb = tq // block

    def split(a):
        return jnp.moveaxis(a.reshape((a.shape[0], nb, block) + a.shape[2:]), 1, 0)

    out = lax.map(lambda args: fn(*args), tuple(split(a) for a in qargs))
    out = jnp.moveaxis(out, 0, 1)
    return out.reshape((out.shape[0], tq) + out.shape[3:])


def dsa_attention(q, q_idx, w_idx, q_pos, k, v, k_idx, n_keep):
    b, tq, h, dh = q.shape
    kvh = k.shape[2]
    k_pos = jnp.arange(k.shape[1], dtype=jnp.int32)
    logits = jnp.einsum('bthd,bsd->bths', q_idx, k_idx, preferred_element_type=jnp.float32)
    score = jnp.einsum('bths,bth->bts', jax.nn.relu(logits), w_idx.astype(jnp.float32))
    causal = k_pos[None, None, :] <= q_pos[:, :, None]
    score = jnp.where(causal, score, -jnp.inf)
    _, sel = lax.top_k(score, n_keep)
    valid = sel <= q_pos[:, :, None]
    gather = jax.vmap(lambda a, i: a[i])
    k_sel = gather(k, sel)
    v_sel = gather(v, sel)
    qg = q.reshape(b, tq, kvh, h // kvh, dh)
    s = jnp.einsum('btkgd,btskd->btkgs', qg, k_sel, preferred_element_type=jnp.float32) * (dh ** -0.5)
    s = jnp.where(valid[:, :, None, None, :], s, -jnp.inf)
    p = jax.nn.softmax(s, axis=-1)
    o = jnp.einsum('btkgs,btskd->btkgd', p, v_sel, preferred_element_type=jnp.float32)
    return o.reshape(b, tq, h * dh).astype(q.dtype)


def moba_blocks(k, v):
    b, l, kvh, dh = k.shape
    nb = -(-l // MOBA_BLOCK)
    pad = nb * MOBA_BLOCK - l

    def blk(a):
        a = jnp.pad(a, ((0, 0), (0, pad), (0, 0), (0, 0)))
        return a.reshape(b, nb, MOBA_BLOCK, kvh, dh).transpose(0, 3, 1, 2, 4)

    kb, vb = blk(k), blk(v)
    k_means = jnp.mean(kb.astype(jnp.float32), axis=3)
    return kb, vb, k_means


def moba_attention(q, q_pos, k_blocks, v_blocks, k_means, n_sel):
    b, tq, h, dh = q.shape
    kvh, nb = k_blocks.shape[1], k_blocks.shape[2]
    qg = q.reshape(b, tq, kvh, h // kvh, dh)
    own = q_pos // MOBA_BLOCK
    gate = jnp.einsum('btkgd,bknd->btkgn', qg.astype(jnp.float32), k_means)
    past = jnp.arange(nb)[None, None, None, None, :] < own[:, :, None, None, None]
    gate = jnp.where(past, gate, -jnp.inf)
    _, sel = lax.top_k(gate, n_sel)
    sel_valid = sel < own[:, :, None, None, None]
    b_ix = jnp.arange(b)[:, None, None, None]
    kv_ix = jnp.arange(kvh)[None, None, :, None]
    scale = dh ** -0.5
    scores, vals = [], []
    for j in range(n_sel):
        kj = k_blocks[b_ix, kv_ix, sel[..., j]]
        vj = v_blocks[b_ix, kv_ix, sel[..., j]]
        sj = jnp.einsum('btkgd,btkgsd->btkgs', qg, kj, preferred_element_type=jnp.float32) * scale
        scores.append(jnp.where(sel_valid[..., j:j + 1], sj, -jnp.inf))
        vals.append(vj)
    take_own = jax.vmap(lambda a, o: a[:, o])
    k_own = take_own(k_blocks, own)
    v_own = take_own(v_blocks, own)
    s_own = jnp.einsum('btkgd,bktsd->btkgs', qg, k_own, preferred_element_type=jnp.float32) * scale
    own_pos = own[:, :, None] * MOBA_BLOCK + jnp.arange(MOBA_BLOCK, dtype=jnp.int32)[None, None, :]
    own_ok = own_pos <= q_pos[:, :, None]
    s_own = jnp.where(own_ok[:, :, None, None, :], s_own, -jnp.inf)
    p = jax.nn.softmax(jnp.concatenate(scores + [s_own], axis=-1), axis=-1)
    o = jnp.einsum('btkgs,bktsd->btkgd', p[..., n_sel * MOBA_BLOCK:], v_own,
                   preferred_element_type=jnp.float32)
    for j in range(n_sel):
        o = o + jnp.einsum('btkgs,btkgsd->btkgd', p[..., j * MOBA_BLOCK:(j + 1) * MOBA_BLOCK], vals[j],
                           preferred_element_type=jnp.float32)
    return o.reshape(b, tq, h * dh).astype(q.dtype)


def mixer_block(h, pos, past, w_in_l, w_pa_l, w_pb_l, w_out_l):
    b, t, _ = h.shape
    proj = h @ w_in_l
    offsets = np.cumsum(IN_SPLITS)[:-1].tolist()
    qa, ka, va, qi, ki, wi, qb, kb, vb, ga, gb = jnp.split(proj, offsets, axis=-1)
    qa = rope(qa.reshape(b, t, N_HEADS_A, HEAD_DIM), pos)
    ka = rope(ka.reshape(b, t, N_KV_A, HEAD_DIM), pos)
    va = va.reshape(b, t, N_KV_A, HEAD_DIM)
    qi = rope(qi.reshape(b, t, N_IDX_HEADS, D_IDX), pos)
    ki = rope(ki.reshape(b, t, 1, D_IDX), pos)[:, :, 0]
    qb = rope(qb.reshape(b, t, N_HEADS_B, HEAD_DIM), pos)
    kb = rope(kb.reshape(b, t, N_KV_B, HEAD_DIM), pos)
    vb = vb.reshape(b, t, N_KV_B, HEAD_DIM)
    new_rows = (ka, va, ki, kb, vb)
    if past is None:
        ka_all, va_all, ki_all, kb_all, vb_all = new_rows
    else:
        ka_all, va_all, ki_all, kb_all, vb_all = [jnp.concatenate([p_, n_], axis=1)
                                                  for p_, n_ in zip(past, new_rows)]
    n_keys = ka_all.shape[1]
    q_pos = jnp.broadcast_to(pos[None, :], (b, t))
    n_keep = min(IDX_TOPK_MAX, n_keys // 4)
    ya = sweep_queries(lambda q_, qi_, wi_, qp_: dsa_attention(q_, qi_, wi_, qp_, ka_all, va_all, ki_all, n_keep),
                       (qa, qi, wi, q_pos), Q_BLOCK_DSA)
    k_blocks, v_blocks, k_means = moba_blocks(kb_all, vb_all)
    n_sel = min(MOBA_TOPK_MAX, k_blocks.shape[2])
    yb = sweep_queries(lambda q_, qp_: moba_attention(q_, qp_, k_blocks, v_blocks, k_means, n_sel),
                       (qb, q_pos), Q_BLOCK_MOBA)
    merged = jax.nn.sigmoid(ga) * (ya @ w_pa_l) + jax.nn.sigmoid(gb) * (yb @ w_pb_l)
    return merged @ w_out_l, new_rows


def routed_experts(xf, expert_ids, combine, w_eg, w_eu, w_ed):
    n, d = xf.shape
    k = expert_ids.shape[1]
    m = n * k
    flat_e = expert_ids.reshape(m)
    order = jnp.argsort(flat_e)
    sorted_e = flat_e[order]
    counts = jnp.bincount(flat_e, length=N_EXPERTS)
    padded = (counts + MOE_ROW_BLOCK - 1) // MOE_ROW_BLOCK * MOE_ROW_BLOCK
    pad_end = jnp.cumsum(padded)
    pad_start = pad_end - padded
    start = jnp.cumsum(counts) - counts
    dest = pad_start[sorted_e] + jnp.arange(m, dtype=jnp.int32) - start[sorted_e]
    n_blocks = -(-m // MOE_ROW_BLOCK) + N_EXPERTS
    rows = jnp.zeros((n_blocks * MOE_ROW_BLOCK, d), xf.dtype).at[dest].set(xf[order // k])
    block_start = jnp.arange(n_blocks, dtype=jnp.int32) * MOE_ROW_BLOCK
    block_e = jnp.minimum(jnp.searchsorted(pad_end, block_start, side='right'), N_EXPERTS - 1)

    def expert_block(args):
        xb, e = args
        hdn = jax.nn.silu(xb @ w_eg[e]) * (xb @ w_eu[e])
        return hdn @ w_ed[e]

    out = lax.map(expert_block, (rows.reshape(n_blocks, MOE_ROW_BLOCK, d), block_e))
    y_sorted = out.reshape(n_blocks * MOE_ROW_BLOCK, d)[dest]
    y_slots = jnp.zeros((m, d), y_sorted.dtype).at[order].set(y_sorted).reshape(n, k, d)
    return jnp.einsum('nkd,nk->nd', y_slots, combine.astype(y_slots.dtype))


def hier_moe(h, w_rg, b_rg, w_re, b_re, w_eg, w_eu, w_ed):
    b, t, d = h.shape
    hf = h.reshape(b * t, d)
    p_grp = jax.nn.softmax((hf @ w_rg).astype(jnp.float32) + b_rg.astype(jnp.float32), axis=-1)
    g_star = jnp.argmax(p_grp, axis=-1).astype(jnp.int32)
    p_g = jnp.take_along_axis(p_grp, g_star[:, None], axis=-1)
    e_logits = ((hf @ w_re).astype(jnp.float32) + b_re.astype(jnp.float32)).reshape(
        b * t, N_GROUPS, EXPERTS_PER_GROUP)
    in_grp = jnp.take_along_axis(e_logits, g_star[:, None, None], axis=1)[:, 0]
    top_l, top_i = lax.top_k(in_grp, TOPK_IN_GROUP)
    combine = p_g * jax.nn.softmax(top_l, axis=-1)
    expert_ids = g_star[:, None] * EXPERTS_PER_GROUP + top_i.astype(jnp.int32)
    return routed_experts(hf, expert_ids, combine, w_eg, w_eu, w_ed).reshape(b, t, d)


def trunk_layer(x, c, pos, past, w_ada_l, b_ada_l, g_mix_l, w_in_l, w_pa_l, w_pb_l, w_out_l, g_ffn_l,
                w_rg_l, b_rg_l, w_re_l, b_re_l, w_eg_l, w_eu_l, w_ed_l):
    mod = jax.nn.silu(c) @ w_ada_l + b_ada_l
    sh1, sc1, gt1, sh2, sc2, gt2 = [m_[:, None, :] for m_ in jnp.split(mod, 6, axis=-1)]
    h = rms_norm(x, g_mix_l) * (1.0 + sc1) + sh1
    mix, new_rows = mixer_block(h, pos, past, w_in_l, w_pa_l, w_pb_l, w_out_l)
    x = x + gt1 * mix
    h = rms_norm(x, g_ffn_l) * (1.0 + sc2) + sh2
    x = x + gt2 * hier_moe(h, w_rg_l, b_rg_l, w_re_l, b_re_l, w_eg_l, w_eu_l, w_ed_l)
    return x, new_rows


def gather_pages(pool, page_table):
    g = pool[page_table]
    return g.reshape((g.shape[0], g.shape[1] * g.shape[2]) + g.shape[3:])


def setup_inputs(seed: int = 0) -> dict:
    key = jax.random.key(seed)
    ks = jax.random.split(key, 32)
    f32 = jnp.float32
    n_pages = PAST_LEN // PAGE_SIZE
    n_used = DEC_BATCH * n_pages
    n_pool = n_used + n_used // 4

    def nrm(k_, shape, scale=1.0):
        return jax.random.normal(k_, shape, f32) * scale

    page_table = jax.random.permutation(ks[9], n_pool)[:n_used].reshape(DEC_BATCH, n_pages).astype(jnp.int32)
    return {
        'x_prompt': nrm(ks[0], (BATCH, SEQ, D_MODEL)),
        'x_sample': nrm(ks[1], (DEC_BATCH, DEC_SEQ, D_MODEL)),
        'c_prompt': nrm(ks[2], (BATCH, D_MODEL)),
        'c_sample': nrm(ks[3], (DEC_BATCH, D_MODEL)),
        'cache_k_a': nrm(ks[4], (DEPTH, n_pool, PAGE_SIZE, N_KV_A, HEAD_DIM)),
        'cache_v_a': nrm(ks[5], (DEPTH, n_pool, PAGE_SIZE, N_KV_A, HEAD_DIM)),
        'cache_idx_k': nrm(ks[6], (DEPTH, n_pool, PAGE_SIZE, D_IDX)),
        'cache_k_b': nrm(ks[7], (DEPTH, n_pool, PAGE_SIZE, N_KV_B, HEAD_DIM)),
        'cache_v_b': nrm(ks[8], (DEPTH, n_pool, PAGE_SIZE, N_KV_B, HEAD_DIM)),
        'page_table': page_table,
        'w_ada': nrm(ks[10], (DEPTH, D_MODEL, 6 * D_MODEL), 0.5 * D_MODEL ** -0.5),
        'b_ada': nrm(ks[11], (DEPTH, 6 * D_MODEL), 0.01),
        'g_mix': 1.0 + nrm(ks[12], (DEPTH, D_MODEL), 0.01),
        'w_in': nrm(ks[13], (DEPTH, D_MODEL, D_IN), D_MODEL ** -0.5),
        'w_proj_a': nrm(ks[14], (DEPTH, WIDTH_A, D_MODEL), WIDTH_A ** -0.5),
        'w_proj_b': nrm(ks[15], (DEPTH, WIDTH_B, D_MODEL), WIDTH_B ** -0.5),
        'w_out': nrm(ks[16], (DEPTH, D_MODEL, D_MODEL), D_MODEL ** -0.5),
        'g_ffn': 1.0 + nrm(ks[17], (DEPTH, D_MODEL), 0.01),
        'w_router_group': nrm(ks[18], (DEPTH, D_MODEL, N_GROUPS), D_MODEL ** -0.5),
        'b_router_group': nrm(ks[19], (DEPTH, N_GROUPS), 0.01),
        'w_router_expert': nrm(ks[20], (DEPTH, D_MODEL, N_EXPERTS), D_MODEL ** -0.5),
        'b_router_expert': nrm(ks[21], (DEPTH, N_EXPERTS), 0.01),
        'w_exp_gate': nrm(ks[22], (DEPTH, N_EXPERTS, D_MODEL, D_EXPERT), D_MODEL ** -0.5),
        'w_exp_up': nrm(ks[23], (DEPTH, N_EXPERTS, D_MODEL, D_EXPERT), D_MODEL ** -0.5),
        'w_exp_down': nrm(ks[24], (DEPTH, N_EXPERTS, D_EXPERT, D_MODEL), D_EXPERT ** -0.5),
        'g_final': 1.0 + nrm(ks[25], (D_MODEL,), 0.01),
    }


def stack_rows(rows, i):
    return jnp.stack([r[i] for r in rows], axis=0)


def reference(x_prompt, x_sample, c_prompt, c_sample, cache_k_a, cache_v_a, cache_idx_k, cache_k_b, cache_v_b,
              page_table, w_ada, b_ada, g_mix, w_in, w_proj_a, w_proj_b, w_out, g_ffn, w_router_group,
              b_router_group, w_router_expert, b_router_expert, w_exp_gate, w_exp_up, w_exp_down, g_final):
    past_len = page_table.shape[1] * cache_k_a.shape[2]
    pos_p = jnp.arange(x_prompt.shape[1], dtype=jnp.int32)
    pos_s = past_len + jnp.arange(x_sample.shape[1], dtype=jnp.int32)
    xp, xs = x_prompt, x_sample
    rows_p, rows_s = [], []
    for l in range(DEPTH):
        lw = (w_ada[l], b_ada[l], g_mix[l], w_in[l], w_proj_a[l], w_proj_b[l], w_out[l], g_ffn[l],
              w_router_group[l], b_router_group[l], w_router_expert[l], b_router_expert[l],
              w_exp_gate[l], w_exp_up[l], w_exp_down[l])
        xp, rp = trunk_layer(xp, c_prompt, pos_p, None, *lw)
        past = (gather_pages(cache_k_a[l], page_table), gather_pages(cache_v_a[l], page_table),
                gather_pages(cache_idx_k[l], page_table), gather_pages(cache_k_b[l], page_table),
                gather_pages(cache_v_b[l], page_table))
        xs, rs = trunk_layer(xs, c_sample, pos_s, past, *lw)
        rows_p.append(rp)
        rows_s.append(rs)
    y_prompt = rms_norm(xp, g_final)
    y_sample = rms_norm(xs, g_final)
    return (y_prompt, y_sample,
            stack_rows(rows_p, 0), stack_rows(rows_p, 1), stack_rows(rows_p, 3), stack_rows(rows_p, 4),
            stack_rows(rows_p, 2),
            stack_rows(rows_s, 0), stack_rows(rows_s, 1), stack_rows(rows_s, 3), stack_rows(rows_s, 4),
            stack_rows(rows_s, 2))
```

```python
import functools

import jax
import jax.numpy as jnp
from jax import lax
from jax.experimental import pallas as pl
from jax.experimental.pallas import tpu as pltpu

HEAD_DIM = 64
N_HEADS_A = 8
N_KV_A = 2
N_IDX_HEADS = 4
D_IDX = 64
IDX_TOPK_MAX = 256
N_HEADS_B = 8
N_KV_B = 2
MOBA_BLOCK = 256
MOBA_TOPK_MAX = 3
N_GROUPS = 4
EXPERTS_PER_GROUP = 8
N_EXPERTS = N_GROUPS * EXPERTS_PER_GROUP
TOPK_IN_GROUP = 2
ROPE_THETA = 10000.0
RMS_EPS = 1e-6

LANES = 128
PAIR = 2 * HEAD_DIM
assert PAIR == LANES and D_IDX == HEAD_DIM and N_KV_A * HEAD_DIM == LANES and N_KV_B * HEAD_DIM == LANES

VMEM_LIMIT = 56 * 1024 * 1024

F32 = jnp.float32
BF16 = jnp.bfloat16
NEG_INF = float("-inf")
INT_MIN = -(2 ** 31)
NEG_INF_KEY = -2139095041


def _cparams(n_grid):
    return pltpu.CompilerParams(dimension_semantics=("arbitrary",) * n_grid, vmem_limit_bytes=VMEM_LIMIT)


def _dot(a, b, precision=None):
    return jnp.dot(a, b, preferred_element_type=F32, precision=precision)


def _dot_nt(a, b, precision=None):
    return lax.dot_general(a, b, (((1,), (1,)), ((), ())), preferred_element_type=F32, precision=precision)


def _ada_kernel(c_ref, w_ref, b_ref, o_ref):
    c = c_ref[...]
    s = c * (1.0 / (1.0 + jnp.exp(-c)))
    o_ref[...] = _dot(s.astype(BF16), w_ref[...].astype(BF16)) + b_ref[...]


def _ada_mod(c_all, w_ada, b_ada):
    depth, d, n6 = w_ada.shape
    m = c_all.shape[0]
    tn = 1024
    return pl.pallas_call(
        _ada_kernel,
        grid=(depth, n6 // tn),
        in_specs=[
            pl.BlockSpec((m, d), lambda l, j: (0, 0)),
            pl.BlockSpec((None, d, tn), lambda l, j: (l, 0, j)),
            pl.BlockSpec((None, 1, tn), lambda l, j: (l, 0, j)),
        ],
        out_specs=pl.BlockSpec((None, m, tn), lambda l, j: (l, 0, j)),
        out_shape=jax.ShapeDtypeStruct((depth, m, n6), F32),
        compiler_params=_cparams(2),
        name="ada_mod",
    )(c_all, w_ada, b_ada.reshape(depth, 1, n6))


_SEG = {}
_off = 0
for _name, _w in (("qa", N_HEADS_A * HEAD_DIM), ("ka", LANES), ("va", LANES), ("qi", N_IDX_HEADS * D_IDX),
                  ("kiw", LANES), ("qb", N_HEADS_B * HEAD_DIM), ("kb", LANES), ("vb", LANES)):
    _SEG[_name] = (_off, _w)
    _off += _w
_GATE_OFF = _off


def _rope_chunk(y, cos, sin, lane):
    first_half = (lane % HEAD_DIM) < (HEAD_DIM // 2)
    rot = jnp.where(first_half, -pltpu.roll(y, LANES - HEAD_DIM // 2, 1), pltpu.roll(y, HEAD_DIM // 2, 1))
    return y * cos + rot * sin


def _inproj_kernel(x_ref, sc_ref, sh_ref, g_ref, w_ref, cos_ref, sin_ref,
                   qa_ref, ka_ref, va_ref, qi_ref, kiw_ref, qb_ref, kb_ref, vb_ref, ga_ref, gb_ref, *, d_model):
    x = x_ref[...]
    h = x * lax.rsqrt(jnp.mean(x * x, axis=-1, keepdims=True) + RMS_EPS) * g_ref[...]
    h = h * (1.0 + sc_ref[...]) + sh_ref[...]
    hb = h.astype(BF16)
    cos = cos_ref[...]
    sin = sin_ref[...]
    lane = lax.broadcasted_iota(jnp.int32, cos.shape, 1)

    def proj(c0, width):
        return _dot(hb, w_ref[:, c0:c0 + width])

    def roped(name, out_ref):
        c0, width = _SEG[name]
        for c in range(width // LANES):
            y = _rope_chunk(proj(c0 + c * LANES, LANES), cos, sin, lane)
            out_ref[:, c * LANES:(c + 1) * LANES] = y.astype(out_ref.dtype)

    roped("qa", qa_ref)
    roped("ka", ka_ref)
    va_ref[...] = proj(*_SEG["va"])
    roped("qi", qi_ref)
    y = proj(*_SEG["kiw"])
    kiw_ref[...] = jnp.where(lane < D_IDX, _rope_chunk(y, cos, sin, lane), y)
    roped("qb", qb_ref)
    roped("kb", kb_ref)
    vb_ref[...] = proj(*_SEG["vb"])
    for c in range(d_model // 512):
        ga = proj(_GATE_OFF + c * 512, 512)
        ga_ref[:, c * 512:(c + 1) * 512] = 1.0 / (1.0 + jnp.exp(-ga))
        gb = proj(_GATE_OFF + d_model + c * 512, 512)
        gb_ref[:, c * 512:(c + 1) * 512] = 1.0 / (1.0 + jnp.exp(-gb))


def _relayout_w_in(w_in_l, d_model):
    widths = (N_HEADS_A * HEAD_DIM, N_KV_A * HEAD_DIM, N_KV_A * HEAD_DIM, N_IDX_HEADS * D_IDX, D_IDX, N_IDX_HEADS,
              N_HEADS_B * HEAD_DIM, N_KV_B * HEAD_DIM, N_KV_B * HEAD_DIM, d_model, d_model)
    offs = [0]
    for w in widths:
        offs.append(offs[-1] + w)
    assert offs[-1] == w_in_l.shape[1]
    qa, ka, va, qi, ki, wi, qb, kb, vb, ga, gb = [w_in_l[:, offs[i]:offs[i + 1]] for i in range(11)]
    pad = jnp.zeros((w_in_l.shape[0], LANES - D_IDX - N_IDX_HEADS), w_in_l.dtype)
    return jnp.concatenate([qa, ka, va, qi, ki, wi, pad, qb, kb, vb, ga, gb], axis=1).astype(BF16)


def _inproj(x, mod, mod_spec, g, w_cat, cos, sin, trig_spec, tm, q_dtype):
    n, d = x.shape
    row = lambda width: pl.BlockSpec((tm, width), lambda i: (i, 0))
    widths = [(_SEG["qa"][1], q_dtype), (LANES, F32), (LANES, F32), (_SEG["qi"][1], q_dtype), (LANES, F32),
              (_SEG["qb"][1], F32), (LANES, F32), (LANES, F32), (d, F32), (d, F32)]
    return pl.pallas_call(
        functools.partial(_inproj_kernel, d_model=d),
        grid=(n // tm,),
        in_specs=[row(d), mod_spec(1), mod_spec(0), pl.BlockSpec((1, d), lambda i: (0, 0)),
                  pl.BlockSpec(w_cat.shape, lambda i: (0, 0)), trig_spec, trig_spec],
        out_specs=[row(w) for w, _ in widths],
        out_shape=[jax.ShapeDtypeStruct((n, w), dt) for w, dt in widths],
        compiler_params=_cparams(1),
        name="inproj",
    )(x, mod, mod, g.reshape(1, d), w_cat, cos, sin)


def _split_heads(kv, which):
    lane = lax.broadcasted_iota(jnp.int32, kv.shape, 1)
    if which == 0:
        left = jnp.where(lane < HEAD_DIM, kv, 0.0)
        right = pltpu.roll(left, HEAD_DIM, 1)
    else:
        right = jnp.where(lane >= HEAD_DIM, kv, 0.0)
        left = pltpu.roll(right, HEAD_DIM, 1)
    return left.astype(BF16), right.astype(BF16)


def _sort_key(x):
    bits = lax.bitcast_convert_type(x, jnp.int32)
    return bits ^ ((bits >> 31) & 0x7FFFFFFF)


def _count(mask):
    return jnp.sum(jnp.where(mask, 1.0, 0.0), axis=1, keepdims=True)


def _masked_attention(q_pair_fn, n_pairs, heads_per_kv, k_vals, v_vals, bias_fn, out_ref):
    pairs_per_kv = heads_per_kv // 2
    for kv in range(n_pairs // pairs_per_kv):
        k_sides = _split_heads(k_vals, kv)
        v_sides = _split_heads(v_vals, kv)
        for pj in range(pairs_per_kv):
            j = kv * pairs_per_kv + pj
            qp = q_pair_fn(j)
            o = None
            for side in range(2):
                s = _dot_nt(qp, k_sides[side]) + bias_fn(j, side)
                m = jnp.max(s, axis=1, keepdims=True)
                p = jnp.exp(s - m)
                l = jnp.sum(p, axis=1, keepdims=True)
                pv = _dot(p.astype(BF16), v_sides[side]) / l
                o = pv if o is None else o + pv
            out_ref[:, j * LANES:(j + 1) * LANES] = o.astype(out_ref.dtype)


def _dsa_attend(qi, kiw_q, qa, kiw_k, ka_k, va_k, q_pos, n_keep, n_real, bias_ref, out_ref):
    tq = qi.shape[0]
    s_len = kiw_k.shape[0]
    ki_sides = _split_heads(kiw_k, 0)
    score = None
    for h in range(N_IDX_HEADS):
        logits = _dot_nt(qi[:, (h // 2) * LANES:(h // 2 + 1) * LANES], ki_sides[h % 2])
        term = jnp.maximum(logits, 0.0) * kiw_q[:, D_IDX + h:D_IDX + h + 1]
        score = term if score is None else score + term
    k_pos = lax.broadcasted_iota(jnp.int32, (tq, s_len), 1)
    causal = k_pos <= q_pos
    score = jnp.where(score == 0.0, 0.0, score)
    key = _sort_key(jnp.where(causal, score, NEG_INF))

    keep = float(n_keep)
    thr = jnp.where(_count(key >= 0) >= keep, 0, INT_MIN).astype(jnp.int32)

    def bit_step(i, thr):
        cand = thr + jnp.left_shift(jnp.int32(1), 30 - i)
        return jnp.where(_count(key >= cand) >= keep, cand, thr)

    thr = lax.fori_loop(0, 31, bit_step, thr)
    above = key > thr
    at_least = key >= thr
    n_above = _count(above)
    bias_ref[...] = jnp.where(at_least & causal, 0.0, NEG_INF)

    row = lax.broadcasted_iota(jnp.int32, (tq, 1), 0)
    tied = (_count(at_least) > keep) & (thr > NEG_INF_KEY) & (row < n_real)

    @pl.when(jnp.max(jnp.where(tied, 1.0, 0.0)) > 0.0)
    def _():
        free = keep - n_above
        r_i = lax.broadcasted_iota(jnp.int32, (MOBA_BLOCK, MOBA_BLOCK), 0)
        c_i = lax.broadcasted_iota(jnp.int32, (MOBA_BLOCK, MOBA_BLOCK), 1)
        tri = jnp.where(r_i <= c_i, 1.0, 0.0).astype(BF16)
        seen = jnp.zeros((tq, 1), F32)
        for c0 in range(0, s_len, MOBA_BLOCK):
            w = min(MOBA_BLOCK, s_len - c0)
            key_c = key[:, c0:c0 + w]
            eq_c = jnp.where(key_c == thr, 1.0, 0.0)
            rank = _dot(eq_c.astype(BF16), tri[:w, :w]) + seen
            keep_c = (key_c > thr) | ((key_c == thr) & (rank <= free))
            causal_c = (c0 + lax.broadcasted_iota(jnp.int32, (tq, w), 1)) <= q_pos
            bias_ref[:, c0:c0 + w] = jnp.where(keep_c & causal_c, 0.0, NEG_INF)
            seen = seen + jnp.sum(eq_c, axis=1, keepdims=True)

    scale = HEAD_DIM ** -0.5
    _masked_attention(lambda j: qa[:, j * LANES:(j + 1) * LANES] * scale, N_HEADS_A // 2, N_HEADS_A // N_KV_A,
                      ka_k, va_k, lambda j, side: bias_ref[...], out_ref)


def _moba_attend(qb, kb_k, vb_k, q_pos, own, n_sel, bias_ref, out_ref):
    tq = qb.shape[0]
    s_len = kb_k.shape[0]
    n_full = s_len // MOBA_BLOCK
    k_mean = jnp.sum(kb_k[:n_full * MOBA_BLOCK].reshape(n_full, MOBA_BLOCK, LANES), axis=1) * (1.0 / MOBA_BLOCK)
    lane_m = lax.broadcasted_iota(jnp.int32, k_mean.shape, 1)
    blk = lax.broadcasted_iota(jnp.int32, (tq, n_full), 1)
    blk_f = blk.astype(F32)
    heads_per_kv = N_HEADS_B // N_KV_B
    sel = []
    for h in range(N_HEADS_B):
        kv = h // heads_per_kv
        km = jnp.where(lane_m >= HEAD_DIM if kv == 1 else lane_m < HEAD_DIM, k_mean, 0.0)
        if (h % 2) != kv:
            km = pltpu.roll(km, HEAD_DIM, 1)
        gate = _dot_nt(qb[:, (h // 2) * LANES:(h // 2 + 1) * LANES], km, precision=lax.Precision.HIGHEST)
        gate = jnp.where(blk < own, gate, NEG_INF)
        picked = jnp.zeros((tq, n_full), F32)
        for _ in range(n_sel):
            best = jnp.max(gate, axis=1, keepdims=True)
            first = jnp.min(jnp.where(gate == best, blk_f, float(n_full)), axis=1, keepdims=True)
            hit = (blk_f == first) & (best > NEG_INF)
            picked = jnp.where(hit, 1.0, picked)
            gate = jnp.where(blk_f == first, NEG_INF, gate)
        sel.append(picked)

    def bias_fn(j, side):
        picked = sel[2 * j + side]
        for c0 in range(0, s_len, MOBA_BLOCK):
            c = c0 // MOBA_BLOCK
            w = min(MOBA_BLOCK, s_len - c0)
            k_pos = c0 + lax.broadcasted_iota(jnp.int32, (tq, w), 1)
            allowed = (k_pos <= q_pos) & (own == c)
            if c < n_full:
                allowed = allowed | (picked[:, c:c + 1] > 0.0)
            bias_ref[:, c0:c0 + w] = jnp.where(allowed, 0.0, NEG_INF)
        return bias_ref[...]

    scale = HEAD_DIM ** -0.5
    _masked_attention(lambda j: (qb[:, j * LANES:(j + 1) * LANES] * scale).astype(BF16), N_HEADS_B // 2,
                      heads_per_kv, kb_k, vb_k, bias_fn, out_ref)


def _dsa_prompt_kernel(qi_ref, kiwq_ref, qa_ref, kiwk_ref, ka_ref, va_ref, o_ref, bias_ref, *, tq, n_keep):
    q_pos = pl.program_id(1) * tq + lax.broadcasted_iota(jnp.int32, (tq, 1), 0)
    _dsa_attend(qi_ref[...], kiwq_ref[...], qa_ref[...], kiwk_ref[...], ka_ref[...], va_ref[...],
                q_pos, n_keep, tq, bias_ref, o_ref)


def _dsa_prompt(qi, kiw, qa, ka, va, batch, seq, tq):
    nq = seq // tq
    qrow = lambda w: pl.BlockSpec((tq, w), lambda b, i: (b * nq + i, 0))
    keys = pl.BlockSpec((seq, LANES), lambda b, i: (b, 0))
    return pl.pallas_call(
        functools.partial(_dsa_prompt_kernel, tq=tq, n_keep=min(IDX_TOPK_MAX, seq // 4)),
        grid=(batch, nq),
        in_specs=[qrow(qi.shape[1]), qrow(LANES), qrow(qa.shape[1]), keys, keys, keys],
        out_specs=qrow(qa.shape[1]),
        out_shape=jax.ShapeDtypeStruct(qa.shape, BF16),
        scratch_shapes=[pltpu.VMEM((tq, seq), F32)],
        compiler_params=_cparams(2),
        name="dsa_prompt",
    )(qi, kiw, qa, kiw, ka, va)


def _moba_prompt_kernel(qb_ref, kb_ref, vb_ref, o_ref, bias_ref, *, tq, n_sel):
    base = pl.program_id(1) * tq
    q_pos = base + lax.broadcasted_iota(jnp.int32, (tq, 1), 0)
    _moba_attend(qb_ref[...], kb_ref[...], vb_ref[...], q_pos, base // MOBA_BLOCK, n_sel, bias_ref, o_ref)


def _moba_prompt(qb, kb, vb, batch, seq, tq):
    assert MOBA_BLOCK % tq == 0 and seq % MOBA_BLOCK == 0
    nq = seq // tq
    qrow = pl.BlockSpec((tq, qb.shape[1]), lambda b, i: (b * nq + i, 0))
    keys = pl.BlockSpec((seq, LANES), lambda b, i: (b, 0))
    return pl.pallas_call(
        functools.partial(_moba_prompt_kernel, tq=tq, n_sel=min(MOBA_TOPK_MAX, seq // MOBA_BLOCK)),
        grid=(batch, nq),
        in_specs=[qrow, keys, keys],
        out_specs=qrow,
        out_shape=jax.ShapeDtypeStruct(qb.shape, BF16),
        scratch_shapes=[pltpu.VMEM((tq, seq), F32)],
        compiler_params=_cparams(2),
        name="moba_prompt",
    )(qb, kb, vb)


SAMPLE_ROWS = 16


def _assemble_keys(page_refs, new_ref, dst_ref, page_size, past_len):
    width = page_refs[0].shape[-1]
    for j, page in enumerate(page_refs):
        dst_ref[j * page_size:(j + 1) * page_size, 0:width] = page[...]
        if width < LANES:
            dst_ref[j * page_size:(j + 1) * page_size, width:LANES] = jnp.zeros((page_size, LANES - width), F32)
    n_new = new_ref.shape[0]
    dst_ref[past_len:past_len + n_new, :] = new_ref[...]
    tail = dst_ref.shape[0] - past_len - n_new
    if tail:
        dst_ref[past_len + n_new:, :] = jnp.zeros((tail, LANES), F32)


def _dsa_sample_kernel(pt_ref, qi_ref, kiwn_ref, qa_ref, kan_ref, van_ref, *rest, n_pages, page_size, n_keep,
                       dec_seq):
    ki_pages = rest[0:n_pages]
    ka_pages = rest[n_pages:2 * n_pages]
    va_pages = rest[2 * n_pages:3 * n_pages]
    o_ref, kiw_s, ka_s, va_s, bias_ref = rest[3 * n_pages:]
    past = n_pages * page_size
    _assemble_keys(ki_pages, kiwn_ref, kiw_s, page_size, past)
    _assemble_keys(ka_pages, kan_ref, ka_s, page_size, past)
    _assemble_keys(va_pages, van_ref, va_s, page_size, past)
    q_pos = past + lax.broadcasted_iota(jnp.int32, (SAMPLE_ROWS, 1), 0)
    _dsa_attend(qi_ref[...].astype(BF16), kiwn_ref[...], qa_ref[...].astype(BF16), kiw_s[...], ka_s[...], va_s[...],
                q_pos, n_keep, dec_seq, bias_ref, o_ref)


def _page_specs(layer, n_pages, page_size, width):
    return [pl.BlockSpec((None, None, page_size, width), lambda b, pt, j=j: (layer, pt[b, j], 0, 0))
            for j in range(n_pages)]


def _dsa_sample(qi, kiw_new, qa, ka_new, va_new, cache_idx_k, cache_k, cache_v, page_table, layer, dec_seq):
    b, n_pages = page_table.shape
    page_size = cache_k.shape[2]
    past = n_pages * page_size
    s_len = past + LANES
    rows = lambda w: pl.BlockSpec((None, SAMPLE_ROWS, w), lambda i, pt: (i, 0, 0))
    grid_spec = pltpu.PrefetchScalarGridSpec(
        num_scalar_prefetch=1,
        grid=(b,),
        in_specs=[rows(qi.shape[2]), rows(LANES), rows(qa.shape[2]), rows(LANES), rows(LANES)]
        + _page_specs(layer, n_pages, page_size, cache_idx_k.shape[3])
        + _page_specs(layer, n_pages, page_size, LANES) + _page_specs(layer, n_pages, page_size, LANES),
        out_specs=rows(qa.shape[2]),
        scratch_shapes=[pltpu.VMEM((s_len, LANES), F32)] * 3 + [pltpu.VMEM((SAMPLE_ROWS, s_len), F32)],
    )
    return pl.pallas_call(
        functools.partial(_dsa_sample_kernel, n_pages=n_pages, page_size=page_size,
                          n_keep=min(IDX_TOPK_MAX, (past + dec_seq) // 4), dec_seq=dec_seq),
        grid_spec=grid_spec,
        out_shape=jax.ShapeDtypeStruct(qa.shape, BF16),
        compiler_params=_cparams(1),
        name="dsa_sample",
    )(page_table, qi, kiw_new, qa, ka_new, va_new,
      *([cache_idx_k] * n_pages), *([cache_k] * n_pages), *([cache_v] * n_pages))


def _moba_sample_kernel(pt_ref, qb_ref, kbn_ref, vbn_ref, *rest, n_pages, page_size, n_sel):
    kb_pages = rest[0:n_pages]
    vb_pages = rest[n_pages:2 * n_pages]
    o_ref, kb_s, vb_s, bias_ref = rest[2 * n_pages:]
    past = n_pages * page_size
    _assemble_keys(kb_pages, kbn_ref, kb_s, page_size, past)
    _assemble_keys(vb_pages, vbn_ref, vb_s, page_size, past)
    q_pos = past + lax.broadcasted_iota(jnp.int32, (SAMPLE_ROWS, 1), 0)
    _moba_attend(qb_ref[...], kb_s[...], vb_s[...], q_pos, past // MOBA_BLOCK, n_sel, bias_ref, o_ref)


def _moba_sample(qb, kb_new, vb_new, cache_k, cache_v, page_table, layer, dec_seq):
    b, n_pages = page_table.shape
    page_size = cache_k.shape[2]
    past = n_pages * page_size
    assert past % MOBA_BLOCK == 0 and dec_seq <= SAMPLE_ROWS <= MOBA_BLOCK
    s_len = past + LANES
    n_blocks = -(-(past + dec_seq) // MOBA_BLOCK)
    rows = lambda w: pl.BlockSpec((None, SAMPLE_ROWS, w), lambda i, pt: (i, 0, 0))
    grid_spec = pltpu.PrefetchScalarGridSpec(
        num_scalar_prefetch=1,
        grid=(b,),
        in_specs=[rows(qb.shape[2]), rows(LANES), rows(LANES)]
        + _page_specs(layer, n_pages, page_size, LANES) + _page_specs(layer, n_pages, page_size, LANES),
        out_specs=rows(qb.shape[2]),
        scratch_shapes=[pltpu.VMEM((s_len, LANES), F32)] * 2 + [pltpu.VMEM((SAMPLE_ROWS, s_len), F32)],
    )
    return pl.pallas_call(
        functools.partial(_moba_sample_kernel, n_pages=n_pages, page_size=page_size,
                          n_sel=min(MOBA_TOPK_MAX, n_blocks)),
        grid_spec=grid_spec,
        out_shape=jax.ShapeDtypeStruct(qb.shape, BF16),
        compiler_params=_cparams(1),
        name="moba_sample",
    )(page_table, qb, kb_new, vb_new, *([cache_k] * n_pages), *([cache_v] * n_pages))


ROUTE_E1, ROUTE_E2, ROUTE_C1, ROUTE_C2 = 0, 1, 2, 3


def _mixout_kernel(ya_ref, yb_ref, ga_ref, gb_ref, x_ref, gt1_ref, sc2_ref, sh2_ref, g_ref,
                   wpa_ref, wpb_ref, wout_ref, wr_ref, br_ref, x1_ref, h2_ref, route_ref):
    merged = ga_ref[...] * _dot(ya_ref[...], wpa_ref[...]) + gb_ref[...] * _dot(yb_ref[...], wpb_ref[...])
    mix = _dot(merged.astype(BF16), wout_ref[...])
    x1 = x_ref[...] + gt1_ref[...] * mix
    x1_ref[...] = x1
    h2 = x1 * lax.rsqrt(jnp.mean(x1 * x1, axis=-1, keepdims=True) + RMS_EPS) * g_ref[...]
    h2 = h2 * (1.0 + sc2_ref[...]) + sh2_ref[...]
    h2_ref[...] = h2

    logits = _dot(h2, wr_ref[...], precision=lax.Precision.HIGHEST) + br_ref[...]
    lane = lax.broadcasted_iota(jnp.int32, logits.shape, 1)
    lane_f = lane.astype(F32)
    is_grp = lane < N_GROUPS
    grp = jnp.where(is_grp, logits, NEG_INF)
    g_max = jnp.max(grp, axis=1, keepdims=True)
    g_star = jnp.min(jnp.where(grp == g_max, lane_f, float(N_GROUPS)), axis=1, keepdims=True)
    p_g = 1.0 / jnp.sum(jnp.where(is_grp, jnp.exp(grp - g_max), 0.0), axis=1, keepdims=True)
    e_id = lane_f - float(N_GROUPS)
    in_grp = (e_id >= g_star * EXPERTS_PER_GROUP) & (e_id < (g_star + 1.0) * EXPERTS_PER_GROUP)
    cand = jnp.where(in_grp, logits, NEG_INF)
    l1 = jnp.max(cand, axis=1, keepdims=True)
    e1 = jnp.min(jnp.where(cand == l1, e_id, float(N_EXPERTS)), axis=1, keepdims=True)
    cand = jnp.where(e_id == e1, NEG_INF, cand)
    l2 = jnp.max(cand, axis=1, keepdims=True)
    e2 = jnp.min(jnp.where(cand == l2, e_id, float(N_EXPERTS)), axis=1, keepdims=True)
    t = jnp.exp(l2 - l1)
    c1 = p_g / (1.0 + t)
    c2 = p_g * t / (1.0 + t)
    route = jnp.where(lane == ROUTE_E1, e1, 0.0)
    route = jnp.where(lane == ROUTE_E2, e2, route)
    route = jnp.where(lane == ROUTE_C1, c1, route)
    route_ref[...] = jnp.where(lane == ROUTE_C2, c2, route)


def _mixout(ya, yb, ga, gb, x, mod, mod_spec, g, w_pa, w_pb, w_out, w_r, b_r, tm):
    n, d = x.shape
    row = lambda width: pl.BlockSpec((tm, width), lambda i: (i, 0))
    full = lambda a: pl.BlockSpec(a.shape, lambda i: (0, 0))
    return pl.pallas_call(
        _mixout_kernel,
        grid=(n // tm,),
        in_specs=[row(ya.shape[1]), row(yb.shape[1]), row(d), row(d), row(d),
                  mod_spec(2), mod_spec(4), mod_spec(3), pl.BlockSpec((1, d), lambda i: (0, 0)),
                  full(w_pa), full(w_pb), full(w_out), full(w_r), full(b_r)],
        out_specs=[row(d), row(d), row(LANES)],
        out_shape=[jax.ShapeDtypeStruct((n, d), F32), jax.ShapeDtypeStruct((n, d), F32),
                   jax.ShapeDtypeStruct((n, LANES), F32)],
        compiler_params=_cparams(1),
        name="mixout",
    )(ya, yb, ga, gb, x, mod, mod, mod, g.reshape(1, d), w_pa, w_pb, w_out, w_r, b_r)


MOE_ROWS = 256


def _experts_kernel(blk_e_ref, n_valid_ref, slot_ref, h_hbm, cw_ref, wgu_ref, wd_ref, y_hbm,
                    x_buf, y_buf, gather_sem, scatter_sem, *, d_expert):
    i = pl.program_id(0)
    n_valid = n_valid_ref[i]
    base = i * MOE_ROWS

    def gather_copy(r):
        token = jnp.right_shift(slot_ref[base + r], 1)
        return pltpu.make_async_copy(h_hbm.at[pl.ds(token, 1)], x_buf.at[pl.ds(r, 1)], gather_sem)

    def scatter_copy(r):
        return pltpu.make_async_copy(y_buf.at[pl.ds(r, 1)], y_hbm.at[pl.ds(slot_ref[base + r], 1)], scatter_sem)

    @pl.when(n_valid > 0)
    def _():
        lax.fori_loop(0, MOE_ROWS, lambda r, c: (gather_copy(r).start(), c)[1], 0)
        lax.fori_loop(0, MOE_ROWS, lambda r, c: (gather_copy(r).wait(), c)[1], 0)
        xb = x_buf[...].astype(BF16)
        gu = _dot(xb, wgu_ref[...])
        gate = gu[:, :d_expert]
        hidden = gate * (1.0 / (1.0 + jnp.exp(-gate))) * gu[:, d_expert:]
        y_buf[...] = _dot(hidden.astype(BF16), wd_ref[...]) * cw_ref[...]
        lax.fori_loop(0, n_valid, lambda r, c: (scatter_copy(r).start(), c)[1], 0)
        lax.fori_loop(0, n_valid, lambda r, c: (scatter_copy(r).wait(), c)[1], 0)


def _experts(h2, expert_ids, combine, w_gu, w_d):
    n, d = h2.shape
    k = expert_ids.shape[1]
    m = n * k
    d_expert = w_d.shape[1]
    n_blocks = -(-m // MOE_ROWS) + N_EXPERTS
    n_pad = n_blocks * MOE_ROWS

    flat_e = expert_ids.reshape(m)
    order = jnp.argsort(flat_e).astype(jnp.int32)
    counts = jnp.sum(flat_e[:, None] == jnp.arange(N_EXPERTS, dtype=jnp.int32)[None, :], axis=0).astype(jnp.int32)
    padded = (counts + MOE_ROWS - 1) // MOE_ROWS * MOE_ROWS
    pad_end = jnp.cumsum(padded)
    pad_start = pad_end - padded
    start = jnp.cumsum(counts) - counts
    block_start = jnp.arange(n_blocks, dtype=jnp.int32) * MOE_ROWS
    blk_e = jnp.minimum(jnp.searchsorted(pad_end, block_start, side="right"), N_EXPERTS - 1).astype(jnp.int32)
    n_valid = jnp.clip(counts[blk_e] - (block_start - pad_start[blk_e]), 0, MOE_ROWS).astype(jnp.int32)
    last_e = jnp.max(jnp.where(n_valid > 0, blk_e, 0))
    blk_e = jnp.where(n_valid > 0, blk_e, last_e)
    pos = jnp.arange(n_pad, dtype=jnp.int32)
    pos_e = jnp.repeat(blk_e, MOE_ROWS)
    within = pos - jnp.repeat(block_start - 0, MOE_ROWS)
    is_row = within < jnp.repeat(n_valid, MOE_ROWS)
    sorted_ix = jnp.clip(start[pos_e] + (pos - pad_start[pos_e]), 0, m - 1)
    slot_of_pos = jnp.where(is_row, order[sorted_ix], 0).astype(jnp.int32)
    cw = jnp.where(is_row, combine.reshape(m)[slot_of_pos], 0.0).astype(F32).reshape(n_pad, 1)

    grid_spec = pltpu.PrefetchScalarGridSpec(
        num_scalar_prefetch=3,
        grid=(n_blocks,),
        in_specs=[
            pl.BlockSpec(memory_space=pl.ANY),
            pl.BlockSpec((MOE_ROWS, 1), lambda i, be, nv, sl: (i, 0)),
            pl.BlockSpec((None, d, 2 * d_expert), lambda i, be, nv, sl: (be[i], 0, 0)),
            pl.BlockSpec((None, d_expert, d), lambda i, be, nv, sl: (be[i], 0, 0)),
        ],
        out_specs=pl.BlockSpec(memory_space=pl.ANY),
        scratch_shapes=[pltpu.VMEM((MOE_ROWS, d), F32), pltpu.VMEM((MOE_ROWS, d), F32),
                        pltpu.SemaphoreType.DMA(()), pltpu.SemaphoreType.DMA(())],
    )
    return pl.pallas_call(
        functools.partial(_experts_kernel, d_expert=d_expert),
        grid_spec=grid_spec,
        out_shape=jax.ShapeDtypeStruct((m, d), F32),
        compiler_params=_cparams(1),
        name="experts",
    )(blk_e, n_valid, slot_of_pos, h2, cw, w_gu, w_d)


def _combine_kernel(x1_ref, y_ref, gt2_ref, g_ref, o_ref, *, d_model, final_norm):
    y = y_ref[...]
    x2 = x1_ref[...] + gt2_ref[...] * (y[:, :d_model] + y[:, d_model:])
    if final_norm:
        x2 = x2 * lax.rsqrt(jnp.mean(x2 * x2, axis=-1, keepdims=True) + RMS_EPS) * g_ref[...]
    o_ref[...] = x2


def _combine(x1, y_slots, mod, mod_spec, g_final, tm, final_norm):
    n, d = x1.shape
    return pl.pallas_call(
        functools.partial(_combine_kernel, d_model=d, final_norm=final_norm),
        grid=(n // tm,),
        in_specs=[pl.BlockSpec((tm, d), lambda i: (i, 0)), pl.BlockSpec((tm, TOPK_IN_GROUP * d), lambda i: (i, 0)),
                  mod_spec(5), pl.BlockSpec((1, d), lambda i: (0, 0))],
        out_specs=pl.BlockSpec((tm, d), lambda i: (i, 0)),
        out_shape=jax.ShapeDtypeStruct((n, d), F32),
        compiler_params=_cparams(1),
        name="combine",
    )(x1, y_slots.reshape(n, TOPK_IN_GROUP * d), mod, g_final.reshape(1, d))


def _rope_tables(pos):
    half = HEAD_DIM // 2
    inv = jnp.power(ROPE_THETA, -jnp.arange(half, dtype=F32) / half)
    ang = pos.astype(F32)[:, None] * inv[None, :]
    reps = LANES // half
    return jnp.tile(jnp.cos(ang), (1, reps)), jnp.tile(jnp.sin(ang), (1, reps))


def _pad_rows(a, batch, t):
    return jnp.pad(a.reshape(batch, t, a.shape[-1]), ((0, 0), (0, SAMPLE_ROWS - t), (0, 0)))


def kernel(x_prompt, x_sample, c_prompt, c_sample, cache_k_a, cache_v_a, cache_idx_k, cache_k_b, cache_v_b, page_table, w_ada, b_ada, g_mix, w_in, w_proj_a, w_proj_b, w_out, g_ffn, w_router_group, b_router_group, w_router_expert, b_router_expert, w_exp_gate, w_exp_up, w_exp_down, g_final):
    batch, seq, d = x_prompt.shape
    dec_batch, dec_seq, _ = x_sample.shape
    depth = w_in.shape[0]
    n_pool, page_size = cache_k_a.shape[1], cache_k_a.shape[2]
    past_len = page_table.shape[1] * page_size
    n_p, n_s = batch * seq, dec_batch * dec_seq
    tm_p = min(256, seq)
    tm_s = min(256, n_s)
    assert seq % tm_p == 0 and n_s % tm_s == 0

    mod_all = _ada_mod(jnp.concatenate([c_prompt, c_sample], axis=0), w_ada, b_ada)

    cos_p, sin_p = _rope_tables(jnp.arange(seq, dtype=jnp.int32))
    cos_s, sin_s = _rope_tables(jnp.tile(past_len + jnp.arange(dec_seq, dtype=jnp.int32), dec_batch))
    trig_p = pl.BlockSpec((tm_p, LANES), lambda i: (i % (seq // tm_p), 0))
    trig_s = pl.BlockSpec((tm_s, LANES), lambda i: (i, 0))
    mod_spec_p = lambda chunk: pl.BlockSpec((None, 1, d), lambda i: (i // (seq // tm_p), 0, chunk))
    mod_spec_s = lambda chunk: pl.BlockSpec((tm_s, d), lambda i: (i, chunk))

    caches_a = (cache_idx_k, cache_k_a.reshape(depth, n_pool, page_size, LANES),
                cache_v_a.reshape(depth, n_pool, page_size, LANES))
    caches_b = (cache_k_b.reshape(depth, n_pool, page_size, LANES), cache_v_b.reshape(depth, n_pool, page_size, LANES))

    xp = x_prompt.reshape(n_p, d)
    xs = x_sample.reshape(n_s, d)
    rows_p, rows_s = [], []
    for l in range(depth):
        w_cat = _relayout_w_in(w_in[l], d)
        w_pa, w_pb, w_o = w_proj_a[l].astype(BF16), w_proj_b[l].astype(BF16), w_out[l].astype(BF16)
        w_r = jnp.pad(jnp.concatenate([w_router_group[l], w_router_expert[l]], axis=1),
                      ((0, 0), (0, LANES - N_GROUPS - N_EXPERTS)))
        b_r = jnp.pad(jnp.concatenate([b_router_group[l], b_router_expert[l]]),
                      (0, LANES - N_GROUPS - N_EXPERTS)).reshape(1, LANES)
        w_gu = jnp.concatenate([w_exp_gate[l], w_exp_up[l]], axis=2).astype(BF16)
        w_d = w_exp_down[l].astype(BF16)
        mod_p = mod_all[l, :batch].reshape(batch, 1, 6 * d)
        mod_s = jnp.repeat(mod_all[l, batch:], dec_seq, axis=0)
        last = l == depth - 1

        qa, ka, va, qi, kiw, qb, kb, vb, ga, gb = _inproj(xp, mod_p, mod_spec_p, g_mix[l], w_cat, cos_p, sin_p,
                                                          trig_p, tm_p, BF16)
        rows_p.append((ka, va, kiw[:, :D_IDX], kb, vb))
        ya = _dsa_prompt(qi, kiw, qa, ka, va, batch, seq, min(128, seq))
        yb = _moba_prompt(qb, kb, vb, batch, seq, min(MOBA_BLOCK, seq))
        x1, h2, route = _mixout(ya, yb, ga, gb, xp, mod_p, mod_spec_p, g_ffn[l], w_pa, w_pb, w_o, w_r, b_r, tm_p)
        y_slots = _experts(h2, route[:, ROUTE_E1:ROUTE_E2 + 1].astype(jnp.int32), route[:, ROUTE_C1:ROUTE_C2 + 1],
                           w_gu, w_d)
        xp = _combine(x1, y_slots, mod_p, mod_spec_p, g_final, tm_p, last)

        qa, ka, va, qi, kiw, qb, kb, vb, ga, gb = _inproj(xs, mod_s, mod_spec_s, g_mix[l], w_cat, cos_s, sin_s,
                                                          trig_s, tm_s, F32)
        rows_s.append((ka, va, kiw[:, :D_IDX], kb, vb))
        pad = lambda a: _pad_rows(a, dec_batch, dec_seq)
        ya = _dsa_sample(pad(qi), pad(kiw), pad(qa), pad(ka), pad(va), *caches_a, page_table, l, dec_seq)
        yb = _moba_sample(pad(qb), pad(kb), pad(vb), *caches_b, page_table, l, dec_seq)
        unpad = lambda a: a[:, :dec_seq].reshape(n_s, a.shape[-1])
        x1, h2, route = _mixout(unpad(ya), unpad(yb), ga, gb, xs, mod_s, mod_spec_s, g_ffn[l], w_pa, w_pb, w_o,
                                w_r, b_r, tm_s)
        y_slots = _experts(h2, route[:, ROUTE_E1:ROUTE_E2 + 1].astype(jnp.int32), route[:, ROUTE_C1:ROUTE_C2 + 1],
                           w_gu, w_d)
        xs = _combine(x1, y_slots, mod_s, mod_spec_s, g_final, tm_s, last)

    def stack(rows, i, shape):
        return jnp.stack([r[i] for r in rows], axis=0).reshape(shape)

    kv_p = (depth, batch, seq, N_KV_A, HEAD_DIM)
    kv_s = (depth, dec_batch, dec_seq, N_KV_A, HEAD_DIM)
    return (xp.reshape(batch, seq, d), xs.reshape(dec_batch, dec_seq, d),
            stack(rows_p, 0, kv_p), stack(rows_p, 1, kv_p), stack(rows_p, 3, kv_p), stack(rows_p, 4, kv_p),
            stack(rows_p, 2, (depth, batch, seq, D_IDX)),
            stack(rows_s, 0, kv_s), stack(rows_s, 1, kv_s), stack(rows_s, 3, kv_s), stack(rows_s, 4, kv_s),
            stack(rows_s, 2, (depth, dec_batch, dec_seq, D_IDX)))
```

```python
import functools

import jax
import jax.numpy as jnp
from jax import lax
from jax.experimental import pallas as pl
from jax.experimental.pallas import tpu as pltpu

HEAD_DIM = 64
N_HEADS_A = 8
N_KV_A = 2
N_IDX_HEADS = 4
D_IDX = 64
IDX_TOPK_MAX = 256
N_HEADS_B = 8
N_KV_B = 2
MOBA_BLOCK = 256
MOBA_TOPK_MAX = 3
N_GROUPS = 4
EXPERTS_PER_GROUP = 8
N_EXPERTS = N_GROUPS * EXPERTS_PER_GROUP
TOPK_IN_GROUP = 2
ROPE_THETA = 10000.0
RMS_EPS = 1e-6

LANES = 128
KV_ROWS = N_KV_A * HEAD_DIM
assert KV_ROWS == LANES == N_KV_B * HEAD_DIM and D_IDX == HEAD_DIM and TOPK_IN_GROUP == 2

VMEM_LIMIT = 56 * 1024 * 1024

F32 = jnp.float32
BF16 = jnp.bfloat16
NEG_INF = float("-inf")
INT_MIN = -(2 ** 31)
NEG_INF_KEY = -2139095041
HIGHEST = lax.Precision.HIGHEST


def _cparams(n_grid):
    return pltpu.CompilerParams(dimension_semantics=("arbitrary",) * n_grid, vmem_limit_bytes=VMEM_LIMIT)


def _dot(a, b, precision=None):
    return jnp.dot(a, b, preferred_element_type=F32, precision=precision)


def _dot_nt(a, b, precision=None):
    return lax.dot_general(a, b, (((1,), (1,)), ((), ())), preferred_element_type=F32, precision=precision)


def _sigmoid(x):
    return 1.0 / (1.0 + jnp.exp(-x))


def _ada_kernel(c_ref, w_ref, b_ref, o_ref):
    c = c_ref[...]
    o_ref[...] = _dot((c * _sigmoid(c)).astype(BF16), w_ref[...].astype(BF16)) + b_ref[...]


def _ada_mod(c_all, w_ada, b_ada):
    depth, d, n6 = w_ada.shape
    m = c_all.shape[0]
    tn = 1024
    return pl.pallas_call(
        _ada_kernel,
        grid=(depth, n6 // tn),
        in_specs=[
            pl.BlockSpec((m, d), lambda l, j: (0, 0)),
            pl.BlockSpec((None, d, tn), lambda l, j: (l, 0, j)),
            pl.BlockSpec((None, 1, tn), lambda l, j: (l, 0, j)),
        ],
        out_specs=pl.BlockSpec((None, m, tn), lambda l, j: (l, 0, j)),
        out_shape=jax.ShapeDtypeStruct((depth, m, n6), F32),
        compiler_params=_cparams(2),
        name="ada_mod",
    )(c_all, w_ada, b_ada.reshape(depth, 1, n6))


_QA_W = N_HEADS_A * HEAD_DIM
_QI_W = N_IDX_HEADS * D_IDX
_QB_W = N_HEADS_B * HEAD_DIM
_ROW_QA, _ROW_QI, _ROW_WI, _ROW_QB = 0, _QA_W, _QA_W + _QI_W, _QA_W + _QI_W + LANES
_ROW_GATE = _ROW_QB + _QB_W
_COL_KA, _COL_VA, _COL_KI, _COL_KB, _COL_VB = 0, KV_ROWS, 2 * KV_ROWS, 2 * KV_ROWS + D_IDX, 3 * KV_ROWS + D_IDX
_COL_END = 4 * KV_ROWS + D_IDX


def _rope_lanes(y, cos, sin, lane):
    first_half = (lane % HEAD_DIM) < (HEAD_DIM // 2)
    rot = jnp.where(first_half, -pltpu.roll(y, LANES - HEAD_DIM // 2, 1), pltpu.roll(y, HEAD_DIM // 2, 1))
    return y * cos + rot * sin


def _rope_rows(y, cos, sin):
    half = HEAD_DIM // 2
    parts = []
    for r0 in range(0, y.shape[0], HEAD_DIM):
        parts += [-y[r0 + half:r0 + HEAD_DIM], y[r0:r0 + half]]
    rot = jnp.concatenate(parts, axis=0)
    return y * cos[:y.shape[0]] + rot * sin[:y.shape[0]]


def _inproj_kernel(x_ref, sc_ref, sh_ref, g_ref, wr_ref, wc_ref, cos_ref, sin_ref, cos_t_ref, sin_t_ref,
                   qa_ref, qi_ref, wi_ref, qb_ref, ga_ref, gb_ref, ka_ref, va_ref, ki_ref, kb_ref, vb_ref,
                   *, d_model):
    x = x_ref[...]
    h = x * lax.rsqrt(jnp.mean(x * x, axis=-1, keepdims=True) + RMS_EPS) * g_ref[...]
    h = h * (1.0 + sc_ref[...]) + sh_ref[...]
    hb = h.astype(BF16)
    cos = cos_ref[...]
    sin = sin_ref[...]
    lane = lax.broadcasted_iota(jnp.int32, cos.shape, 1)

    def proj(c0, width):
        return _dot(hb, wr_ref[:, c0:c0 + width])

    def roped(c0, out_ref):
        for c in range(out_ref.shape[1] // LANES):
            y = _rope_lanes(proj(c0 + c * LANES, LANES), cos, sin, lane)
            out_ref[:, c * LANES:(c + 1) * LANES] = y.astype(out_ref.dtype)

    roped(_ROW_QA, qa_ref)
    roped(_ROW_QI, qi_ref)
    wi_ref[...] = proj(_ROW_WI, LANES)
    roped(_ROW_QB, qb_ref)
    for c in range(d_model // 512):
        ga_ref[:, c * 512:(c + 1) * 512] = _sigmoid(proj(_ROW_GATE + c * 512, 512))
        gb_ref[:, c * 512:(c + 1) * 512] = _sigmoid(proj(_ROW_GATE + d_model + c * 512, 512))

    cos_t = cos_t_ref[...]
    sin_t = sin_t_ref[...]

    def proj_t(r0, r1):
        return _dot_nt(wc_ref[r0:r1, :], hb)

    ka_ref[...] = _rope_rows(proj_t(_COL_KA, _COL_VA), cos_t, sin_t)
    va_ref[...] = proj_t(_COL_VA, _COL_KI)
    ki_ref[...] = _rope_rows(proj_t(_COL_KI, _COL_KB), cos_t, sin_t)
    kb_ref[...] = _rope_rows(proj_t(_COL_KB, _COL_VB), cos_t, sin_t)
    vb_ref[...] = proj_t(_COL_VB, _COL_END)


def _split_w_in(w_in):
    depth, d, d_in = w_in.shape
    widths = (_QA_W, KV_ROWS, KV_ROWS, _QI_W, D_IDX, N_IDX_HEADS, _QB_W, KV_ROWS, KV_ROWS, d, d)
    offs = [0]
    for w in widths:
        offs.append(offs[-1] + w)
    assert offs[-1] == d_in
    w_t = jnp.transpose(w_in, (2, 0, 1))
    qa, ka, va, qi, ki, wi, qb, kb, vb, ga, gb = [w_t[offs[i]:offs[i + 1]] for i in range(11)]
    pad = jnp.zeros((LANES - N_IDX_HEADS, depth, d), w_in.dtype)
    w_row = jnp.transpose(jnp.concatenate([qa, qi, wi, pad, qb, ga, gb], axis=0).astype(BF16), (1, 2, 0))
    w_col = jnp.transpose(jnp.concatenate([ka, va, ki, kb, vb], axis=0).astype(BF16), (1, 0, 2))
    return w_row, w_col


def _inproj(x, layer, mod, mod_spec, g_mix, w_row, w_col, trig, trig_spec, trig_t_spec, n_batch, tm, q_dtype):
    n, d = x.shape
    per_batch = n // n_batch
    nq = per_batch // tm
    cos, sin, cos_t, sin_t = trig
    row = lambda width: pl.BlockSpec((tm, width), lambda i: (i, 0))
    col = lambda rows: pl.BlockSpec((None, rows, tm), lambda i: (i // nq, 0, i % nq))
    row_out = [(_QA_W, q_dtype), (_QI_W, q_dtype), (LANES, F32), (_QB_W, F32), (d, F32), (d, F32)]
    col_out = [KV_ROWS, KV_ROWS, D_IDX, KV_ROWS, KV_ROWS]
    layer_w = lambda a: pl.BlockSpec((None,) + a.shape[1:], lambda i: (layer, 0, 0))
    return pl.pallas_call(
        functools.partial(_inproj_kernel, d_model=d),
        grid=(n // tm,),
        in_specs=[row(d), mod_spec(1), mod_spec(0), pl.BlockSpec((None, 1, d), lambda i: (layer, 0, 0)),
                  layer_w(w_row), layer_w(w_col), trig_spec, trig_spec, trig_t_spec, trig_t_spec],
        out_specs=[row(w) for w, _ in row_out] + [col(r) for r in col_out],
        out_shape=[jax.ShapeDtypeStruct((n, w), dt) for w, dt in row_out]
        + [jax.ShapeDtypeStruct((n_batch, r, per_batch), F32) for r in col_out],
        compiler_params=_cparams(1),
        name="inproj",
    )(x, mod, mod, g_mix.reshape(g_mix.shape[0], 1, d), w_row, w_col, cos, sin, cos_t, sin_t)


def _stack_heads(q, first_head, n_heads, scale=None):
    q = q.astype(F32)
    if scale is not None:
        q = q * scale
    parts = [q[:, (first_head + g) * HEAD_DIM:(first_head + g + 1) * HEAD_DIM] for g in range(n_heads)]
    return jnp.concatenate(parts, axis=0).astype(BF16)


def _sort_key(x):
    bits = lax.bitcast_convert_type(x, jnp.int32)
    return bits ^ ((bits >> 31) & 0x7FFFFFFF)


def _count(mask):
    return jnp.sum(jnp.where(mask, 1.0, 0.0), axis=1, keepdims=True)


def _indexer_scores(qi, wi, ki_t, q_pos):
    t = qi.shape[0]
    logits = _dot(_stack_heads(qi, 0, N_IDX_HEADS), ki_t)
    score = None
    for h in range(N_IDX_HEADS):
        term = jnp.maximum(logits[h * t:(h + 1) * t], 0.0) * wi[:, h:h + 1]
        score = term if score is None else score + term
    k_pos = lax.broadcasted_iota(jnp.int32, score.shape, 1)
    score = jnp.where(score == 0.0, 0.0, score)
    return jnp.where(k_pos <= q_pos, score, NEG_INF)


def _topk_mask(score, n_keep, row_is_real, bias_ref):
    t, s_len = score.shape
    key = _sort_key(score)
    keep = float(n_keep)
    thr = jnp.where(_count(key >= 0) >= keep, 0, INT_MIN).astype(jnp.int32)

    def bit_step(i, thr):
        cand = thr + jnp.left_shift(jnp.int32(1), 30 - i)
        return jnp.where(_count(key >= cand) >= keep, cand, thr)

    thr = lax.fori_loop(0, 31, bit_step, thr)
    at_least = key >= thr
    n_above = _count(key > thr)
    bias_ref[...] = jnp.where(at_least & (key > NEG_INF_KEY), 0.0, NEG_INF)

    tied = (_count(at_least) > keep) & (thr > NEG_INF_KEY) & row_is_real

    @pl.when(jnp.max(jnp.where(tied, 1.0, 0.0)) > 0.0)
    def _():
        free = keep - n_above
        r_i = lax.broadcasted_iota(jnp.int32, (MOBA_BLOCK, MOBA_BLOCK), 0)
        c_i = lax.broadcasted_iota(jnp.int32, (MOBA_BLOCK, MOBA_BLOCK), 1)
        tri = jnp.where(r_i <= c_i, 1.0, 0.0).astype(BF16)
        seen = jnp.zeros((t, 1), F32)
        for c0 in range(0, s_len, MOBA_BLOCK):
            w = min(MOBA_BLOCK, s_len - c0)
            key_c = key[:, c0:c0 + w]
            eq_c = jnp.where(key_c == thr, 1.0, 0.0)
            rank = _dot(eq_c.astype(BF16), tri[:w, :w]) + seen
            keep_c = ((key_c > thr) | ((key_c == thr) & (rank <= free))) & (key_c > NEG_INF_KEY)
            bias_ref[:, c0:c0 + w] = jnp.where(keep_c, 0.0, NEG_INF)
            seen = seen + jnp.sum(eq_c, axis=1, keepdims=True)


def _attend_group(q_g, k_t, v_t, bias):
    s = _dot(q_g, k_t)
    if bias.shape[0] != s.shape[0]:
        g = s.shape[0] // bias.shape[0]
        s = (s.reshape(g, bias.shape[0], s.shape[1]) + bias[None]).reshape(s.shape)
    else:
        s = s + bias
    m = jnp.max(s, axis=1, keepdims=True)
    p = jnp.exp(s - m)
    l = jnp.sum(p, axis=1, keepdims=True)
    return _dot_nt(p.astype(BF16), v_t) / l


def _store_heads(o_g, t, first_head, n_heads, out_ref):
    for g in range(0, n_heads, 2):
        pair = jnp.concatenate([o_g[g * t:(g + 1) * t], o_g[(g + 1) * t:(g + 2) * t]], axis=1)
        c0 = (first_head + g) * HEAD_DIM
        out_ref[:, c0:c0 + 2 * HEAD_DIM] = pair.astype(out_ref.dtype)


def _dsa_attention(qa, ka_t, va_t, bias, out_ref):
    t = qa.shape[0]
    hpk = N_HEADS_A // N_KV_A
    for kv in range(N_KV_A):
        k_t = ka_t[kv * HEAD_DIM:(kv + 1) * HEAD_DIM].astype(BF16)
        v_t = va_t[kv * HEAD_DIM:(kv + 1) * HEAD_DIM].astype(BF16)
        o_g = _attend_group(_stack_heads(qa, kv * hpk, hpk, HEAD_DIM ** -0.5), k_t, v_t, bias)
        _store_heads(o_g, t, kv * hpk, hpk, out_ref)


def _block_means(kb_t, n_full):
    lane = lax.broadcasted_iota(jnp.int32, (KV_ROWS, LANES), 1)
    km = jnp.zeros((KV_ROWS, LANES), F32)
    for n in range(n_full):
        col = jnp.sum(kb_t[:, n * MOBA_BLOCK:(n + 1) * MOBA_BLOCK], axis=1, keepdims=True) * (1.0 / MOBA_BLOCK)
        km = jnp.where(lane == n, col, km)
    return km


def _moba_attention(qb, kb_t, vb_t, k_mean, q_pos, own, n_sel, n_full, bias_ref, out_ref):
    t = qb.shape[0]
    s_len = kb_t.shape[1]
    hpk = N_HEADS_B // N_KV_B
    rows = hpk * t
    lane = lax.broadcasted_iota(jnp.int32, (rows, LANES), 1)
    lane_f = lane.astype(F32)
    q_pos_g = jnp.concatenate([q_pos] * hpk, axis=0)
    for kv in range(N_KV_B):
        q_f = jnp.concatenate([qb[:, (kv * hpk + g) * HEAD_DIM:(kv * hpk + g + 1) * HEAD_DIM] for g in range(hpk)],
                              axis=0)
        gate = _dot(q_f, k_mean[kv * HEAD_DIM:(kv + 1) * HEAD_DIM], precision=HIGHEST)
        gate = jnp.where(lane < jnp.minimum(own, n_full), gate, NEG_INF)
        picked = jnp.zeros((rows, LANES), F32)
        for _ in range(n_sel):
            best = jnp.max(gate, axis=1, keepdims=True)
            first = jnp.min(jnp.where(gate == best, lane_f, float(LANES)), axis=1, keepdims=True)
            hit = (lane_f == first) & (best > NEG_INF)
            picked = jnp.where(hit, 1.0, picked)
            gate = jnp.where(lane_f == first, NEG_INF, gate)
        for c0 in range(0, s_len, MOBA_BLOCK):
            c = c0 // MOBA_BLOCK
            w = min(MOBA_BLOCK, s_len - c0)
            k_pos = c0 + lax.broadcasted_iota(jnp.int32, (rows, w), 1)
            allowed = (k_pos <= q_pos_g) & (own == c)
            if c < n_full:
                allowed = allowed | (picked[:, c:c + 1] > 0.0)
            bias_ref[:, c0:c0 + w] = jnp.where(allowed, 0.0, NEG_INF)
        k_t = kb_t[kv * HEAD_DIM:(kv + 1) * HEAD_DIM].astype(BF16)
        v_t = vb_t[kv * HEAD_DIM:(kv + 1) * HEAD_DIM].astype(BF16)
        o_g = _attend_group((q_f * HEAD_DIM ** -0.5).astype(BF16), k_t, v_t, bias_ref[:, :s_len])
        _store_heads(o_g, t, kv * hpk, hpk, out_ref)


PROMPT_TQ = 128
KEY_SEGMENT = 512


def _per_key_segment(seq, tq, body):
    n_seg = max(1, min(4, seq // KEY_SEGMENT))
    seg_len = seq // n_seg
    assert seg_len % tq == 0 and seg_len % MOBA_BLOCK == 0
    seg = (pl.program_id(1) * tq) // seg_len
    for k in range(n_seg):
        pl.when(seg == k)(functools.partial(body, (k + 1) * seg_len))


def _dsa_prompt_kernel(qi_ref, wi_ref, qa_ref, ki_ref, ka_ref, va_ref, o_ref, bias_ref, *, tq, seq, n_keep):
    q_pos = pl.program_id(1) * tq + lax.broadcasted_iota(jnp.int32, (tq, 1), 0)

    def body(s_len):
        score = _indexer_scores(qi_ref[...], wi_ref[...], ki_ref[:, :s_len].astype(BF16), q_pos)
        _topk_mask(score, n_keep, q_pos >= 0, bias_ref.at[:, :s_len])
        _dsa_attention(qa_ref[...], ka_ref[:, :s_len], va_ref[:, :s_len], bias_ref[:, :s_len], o_ref)

    _per_key_segment(seq, tq, body)


def _dsa_prompt(qi, wi, qa, ki_t, ka_t, va_t, tq):
    batch, _, seq = ka_t.shape
    nq = seq // tq
    qrow = lambda w: pl.BlockSpec((tq, w), lambda b, i: (b * nq + i, 0))
    keys = lambda rows: pl.BlockSpec((None, rows, seq), lambda b, i: (b, 0, 0))
    return pl.pallas_call(
        functools.partial(_dsa_prompt_kernel, tq=tq, seq=seq, n_keep=min(IDX_TOPK_MAX, seq // 4)),
        grid=(batch, nq),
        in_specs=[qrow(qi.shape[1]), qrow(LANES), qrow(qa.shape[1]), keys(D_IDX), keys(KV_ROWS), keys(KV_ROWS)],
        out_specs=qrow(qa.shape[1]),
        out_shape=jax.ShapeDtypeStruct(qa.shape, BF16),
        scratch_shapes=[pltpu.VMEM((tq, seq), F32)],
        compiler_params=_cparams(2),
        name="dsa_prompt",
    )(qi, wi, qa, ki_t, ka_t, va_t)


def _moba_prompt_kernel(qb_ref, kb_ref, vb_ref, o_ref, bias_ref, km_ref, *, tq, seq, n_sel):
    base = pl.program_id(1) * tq
    q_pos = base + lax.broadcasted_iota(jnp.int32, (tq, 1), 0)
    n_full = seq // MOBA_BLOCK

    @pl.when(pl.program_id(1) == 0)
    def _():
        km_ref[...] = _block_means(kb_ref[...], n_full)

    def body(s_len):
        _moba_attention(qb_ref[...], kb_ref[:, :s_len], vb_ref[:, :s_len], km_ref[...], q_pos, base // MOBA_BLOCK,
                        n_sel, n_full, bias_ref, o_ref)

    _per_key_segment(seq, tq, body)


def _moba_prompt(qb, kb_t, vb_t, tq):
    batch, _, seq = kb_t.shape
    assert MOBA_BLOCK % tq == 0 and seq % MOBA_BLOCK == 0
    nq = seq // tq
    qrow = pl.BlockSpec((tq, qb.shape[1]), lambda b, i: (b * nq + i, 0))
    keys = pl.BlockSpec((None, KV_ROWS, seq), lambda b, i: (b, 0, 0))
    return pl.pallas_call(
        functools.partial(_moba_prompt_kernel, tq=tq, seq=seq, n_sel=min(MOBA_TOPK_MAX, seq // MOBA_BLOCK)),
        grid=(batch, nq),
        in_specs=[qrow, keys, keys],
        out_specs=qrow,
        out_shape=jax.ShapeDtypeStruct(qb.shape, BF16),
        scratch_shapes=[pltpu.VMEM((N_HEADS_B // N_KV_B * tq, seq), F32), pltpu.VMEM((KV_ROWS, LANES), F32)],
        compiler_params=_cparams(2),
        name="moba_prompt",
    )(qb, kb_t, vb_t)


SAMPLE_ROWS = 8


def _assemble_keys(page_refs, new_ref, dst_ref, page_size):
    for j, page in enumerate(page_refs):
        dst_ref[:, j * page_size:(j + 1) * page_size] = page[...]
    dst_ref[:, len(page_refs) * page_size:] = new_ref[...]


def _page_specs(layer, n_pages, rows, page_size):
    return [pl.BlockSpec((None, None, rows, page_size), lambda b, pt, j=j: (layer, pt[b, j], 0, 0))
            for j in range(n_pages)]


def _score_sample_kernel(pt_ref, qi_ref, wi_ref, kin_ref, *rest, n_pages, page_size):
    ki_pages = rest[:n_pages]
    o_ref, ki_s = rest[n_pages:]
    _assemble_keys(ki_pages, kin_ref, ki_s, page_size)
    q_pos = n_pages * page_size + lax.broadcasted_iota(jnp.int32, (SAMPLE_ROWS, 1), 0)
    o_ref[...] = _indexer_scores(qi_ref[...], wi_ref[...], ki_s[...].astype(BF16), q_pos)


def _score_sample(qi, wi, ki_new, cache_idx_t, page_table, layer):
    b, n_pages = page_table.shape
    page_size = cache_idx_t.shape[3]
    s_len = n_pages * page_size + LANES
    per_b = lambda r, w: pl.BlockSpec((None, r, w), lambda i, pt: (i, 0, 0))
    grid_spec = pltpu.PrefetchScalarGridSpec(
        num_scalar_prefetch=1,
        grid=(b,),
        in_specs=[per_b(SAMPLE_ROWS, qi.shape[2]), per_b(SAMPLE_ROWS, LANES), per_b(D_IDX, LANES)]
        + _page_specs(layer, n_pages, D_IDX, page_size),
        out_specs=per_b(SAMPLE_ROWS, s_len),
        scratch_shapes=[pltpu.VMEM((D_IDX, s_len), F32)],
    )
    return pl.pallas_call(
        functools.partial(_score_sample_kernel, n_pages=n_pages, page_size=page_size),
        grid_spec=grid_spec,
        out_shape=jax.ShapeDtypeStruct((b, SAMPLE_ROWS, s_len), F32),
        compiler_params=_cparams(1),
        name="score_sample",
    )(page_table, qi, wi, ki_new, *([cache_idx_t] * n_pages))


def _select_kernel(score_ref, bias_ref, *, n_keep, dec_seq):
    rows = score_ref.shape[0]
    row = lax.broadcasted_iota(jnp.int32, (rows, 1), 0)
    _topk_mask(score_ref[...], n_keep, (row % SAMPLE_ROWS) < dec_seq, bias_ref)


def _select_sample(score, n_keep, dec_seq):
    n, s_len = score.shape
    tr = min(256, n)
    assert n % tr == 0 and tr % SAMPLE_ROWS == 0
    spec = pl.BlockSpec((tr, s_len), lambda i: (i, 0))
    return pl.pallas_call(
        functools.partial(_select_kernel, n_keep=n_keep, dec_seq=dec_seq),
        grid=(n // tr,),
        in_specs=[spec],
        out_specs=spec,
        out_shape=jax.ShapeDtypeStruct((n, s_len), F32),
        compiler_params=_cparams(1),
        name="select_sample",
    )(score)


def _attend_sample_kernel(pt_ref, qa_ref, bias_ref, kan_ref, van_ref, qb_ref, kbn_ref, vbn_ref, *rest,
                          n_pages, page_size, n_sel):
    pages = [rest[k * n_pages:(k + 1) * n_pages] for k in range(4)]
    ya_ref, yb_ref, ka_s, va_s, kb_s, vb_s, mask_s = rest[4 * n_pages:]
    for page_refs, new_ref, dst in zip(pages, (kan_ref, van_ref, kbn_ref, vbn_ref), (ka_s, va_s, kb_s, vb_s)):
        _assemble_keys(page_refs, new_ref, dst, page_size)
    past = n_pages * page_size
    _dsa_attention(qa_ref[...], ka_s[...], va_s[...], bias_ref[...], ya_ref)
    q_pos = past + lax.broadcasted_iota(jnp.int32, (SAMPLE_ROWS, 1), 0)
    n_full = past // MOBA_BLOCK
    kb_t = kb_s[...]
    _moba_attention(qb_ref[...], kb_t, vb_s[...], _block_means(kb_t, n_full), q_pos, past // MOBA_BLOCK, n_sel,
                    n_full, mask_s, yb_ref)


def _attend_sample(qa, bias, ka_new, va_new, qb, kb_new, vb_new, caches_t, page_table, layer, dec_seq):
    b, n_pages = page_table.shape
    page_size = caches_t[0].shape[3]
    past = n_pages * page_size
    assert past % MOBA_BLOCK == 0 and dec_seq <= SAMPLE_ROWS
    s_len = past + LANES
    n_blocks = -(-(past + dec_seq) // MOBA_BLOCK)
    per_b = lambda r, w: pl.BlockSpec((None, r, w), lambda i, pt: (i, 0, 0))
    new = per_b(KV_ROWS, LANES)
    q = per_b(SAMPLE_ROWS, qa.shape[2])
    grid_spec = pltpu.PrefetchScalarGridSpec(
        num_scalar_prefetch=1,
        grid=(b,),
        in_specs=[q, per_b(SAMPLE_ROWS, s_len), new, new, q, new, new]
        + [spec for _ in range(4) for spec in _page_specs(layer, n_pages, KV_ROWS, page_size)],
        out_specs=[q, q],
        scratch_shapes=[pltpu.VMEM((KV_ROWS, s_len), F32)] * 4
        + [pltpu.VMEM((N_HEADS_B // N_KV_B * SAMPLE_ROWS, s_len), F32)],
    )
    return pl.pallas_call(
        functools.partial(_attend_sample_kernel, n_pages=n_pages, page_size=page_size,
                          n_sel=min(MOBA_TOPK_MAX, n_blocks)),
        grid_spec=grid_spec,
        out_shape=[jax.ShapeDtypeStruct(qa.shape, F32)] * 2,
        compiler_params=_cparams(1),
        name="attend_sample",
    )(page_table, qa, bias, ka_new, va_new, qb, kb_new, vb_new,
      *[c for c in caches_t for _ in range(n_pages)])


ROUTE_E1, ROUTE_E2, ROUTE_C1, ROUTE_C2 = 0, 1, 2, 3


def _mixout_kernel(ya_ref, yb_ref, ga_ref, gb_ref, x_ref, gt1_ref, sc2_ref, sh2_ref, g_ref,
                   wpa_ref, wpb_ref, wout_ref, wr_ref, br_ref, x1_ref, h2_ref, route_ref):
    merged = ga_ref[...] * _dot(ya_ref[...], wpa_ref[...]) + gb_ref[...] * _dot(yb_ref[...], wpb_ref[...])
    mix = _dot(merged.astype(BF16), wout_ref[...])
    x1 = x_ref[...] + gt1_ref[...] * mix
    x1_ref[...] = x1
    h2 = x1 * lax.rsqrt(jnp.mean(x1 * x1, axis=-1, keepdims=True) + RMS_EPS) * g_ref[...]
    h2 = h2 * (1.0 + sc2_ref[...]) + sh2_ref[...]
    h2_ref[...] = h2

    logits = _dot(h2, wr_ref[...], precision=HIGHEST) + br_ref[...]
    lane = lax.broadcasted_iota(jnp.int32, logits.shape, 1)
    lane_f = lane.astype(F32)
    is_grp = lane < N_GROUPS
    grp = jnp.where(is_grp, logits, NEG_INF)
    g_max = jnp.max(grp, axis=1, keepdims=True)
    g_star = jnp.min(jnp.where(grp == g_max, lane_f, float(N_GROUPS)), axis=1, keepdims=True)
    p_g = 1.0 / jnp.sum(jnp.where(is_grp, jnp.exp(grp - g_max), 0.0), axis=1, keepdims=True)
    e_id = lane_f - float(N_GROUPS)
    in_grp = (e_id >= g_star * EXPERTS_PER_GROUP) & (e_id < (g_star + 1.0) * EXPERTS_PER_GROUP)
    cand = jnp.where(in_grp, logits, NEG_INF)
    l1 = jnp.max(cand, axis=1, keepdims=True)
    e1 = jnp.min(jnp.where(cand == l1, e_id, float(N_EXPERTS)), axis=1, keepdims=True)
    cand = jnp.where(e_id == e1, NEG_INF, cand)
    l2 = jnp.max(cand, axis=1, keepdims=True)
    e2 = jnp.min(jnp.where(cand == l2, e_id, float(N_EXPERTS)), axis=1, keepdims=True)
    t = jnp.exp(l2 - l1)
    c1 = p_g / (1.0 + t)
    c2 = p_g * t / (1.0 + t)
    route = jnp.where(lane == ROUTE_E1, e1, 0.0)
    route = jnp.where(lane == ROUTE_E2, e2, route)
    route = jnp.where(lane == ROUTE_C1, c1, route)
    route_ref[...] = jnp.where(lane == ROUTE_C2, c2, route)


def _mixout(ya, yb, ga, gb, x, layer, mod, mod_spec, g_ffn, w_pa, w_pb, w_out, w_r, b_r, tm):
    n, d = x.shape
    row = lambda width: pl.BlockSpec((tm, width), lambda i: (i, 0))
    layer_w = lambda a: pl.BlockSpec((None,) + a.shape[1:], lambda i: (layer, 0, 0))
    return pl.pallas_call(
        _mixout_kernel,
        grid=(n // tm,),
        in_specs=[row(ya.shape[1]), row(yb.shape[1]), row(d), row(d), row(d),
                  mod_spec(2), mod_spec(4), mod_spec(3), pl.BlockSpec((None, 1, d), lambda i: (layer, 0, 0)),
                  layer_w(w_pa), layer_w(w_pb), layer_w(w_out), layer_w(w_r), layer_w(b_r)],
        out_specs=[row(d), row(d), row(LANES)],
        out_shape=[jax.ShapeDtypeStruct((n, d), F32), jax.ShapeDtypeStruct((n, d), F32),
                   jax.ShapeDtypeStruct((n, LANES), F32)],
        compiler_params=_cparams(1),
        name="mixout",
    )(ya, yb, ga, gb, x, mod, mod, mod, g_ffn.reshape(g_ffn.shape[0], 1, d), w_pa, w_pb, w_out, w_r, b_r)


MOE_ROWS = 256
DMA_UNROLL = 8


def _experts_kernel(blk_e_ref, n_valid_ref, src_ref, order_ref, h_hbm, wg_ref, wu_ref, wd_ref, y_hbm,
                    x_buf, y_buf, gather_sem, scatter_sem, *, n_tokens):
    i = pl.program_id(0)
    n_valid = n_valid_ref[i]
    n_slots = TOPK_IN_GROUP * n_tokens

    def slot_of(r):
        return order_ref[jnp.minimum(src_ref[i] + r, n_slots - 1)]

    def gather_copy(r):
        slot = slot_of(r)
        token = jnp.where(slot >= n_tokens, slot - n_tokens, slot)
        return pltpu.make_async_copy(h_hbm.at[pl.ds(token, 1)], x_buf.at[pl.ds(r, 1)], gather_sem)

    def scatter_copy(r):
        dst = jnp.where(r < n_valid, slot_of(r), n_slots + r)
        return pltpu.make_async_copy(y_buf.at[pl.ds(r, 1)], y_hbm.at[pl.ds(dst, 1)], scatter_sem)

    def start_all(make_copy):
        def step(r, carry):
            make_copy(r).start()
            return carry
        lax.fori_loop(0, MOE_ROWS, step, 0, unroll=DMA_UNROLL)

    @pl.when(i == 0)
    def _():
        y_buf[...] = jnp.zeros_like(y_buf)
        init = pltpu.make_async_copy(y_buf, y_hbm.at[pl.ds(n_slots, MOE_ROWS)], scatter_sem)
        init.start()
        init.wait()

    @pl.when(n_valid > 0)
    def _():
        start_all(gather_copy)
        pltpu.make_async_copy(y_hbm.at[pl.ds(0, MOE_ROWS)], x_buf, gather_sem).wait()
        xb = x_buf[...].astype(BF16)
        gate = _dot(xb, wg_ref[...])
        hidden = gate * _sigmoid(gate) * _dot(xb, wu_ref[...])
        y_buf[...] = _dot(hidden.astype(BF16), wd_ref[...])
        start_all(scatter_copy)
        pltpu.make_async_copy(y_buf, y_hbm.at[pl.ds(0, MOE_ROWS)], scatter_sem).wait()


def _experts(h2, expert_ids, layer, w_gate, w_up, w_down):
    n, d = h2.shape
    m = TOPK_IN_GROUP * n
    n_blocks = -(-m // MOE_ROWS) + N_EXPERTS
    flat_e = expert_ids.T.reshape(m)
    order = jnp.argsort(flat_e).astype(jnp.int32)
    experts = jnp.arange(N_EXPERTS, dtype=jnp.int32)
    counts = jnp.sum((flat_e[:, None] == experts[None, :]).astype(jnp.int32), axis=0)
    padded = (counts + MOE_ROWS - 1) // MOE_ROWS * MOE_ROWS
    pad_end = jnp.cumsum(padded)
    pad_start = pad_end - padded
    start = jnp.cumsum(counts) - counts
    block_start = jnp.arange(n_blocks, dtype=jnp.int32) * MOE_ROWS
    blk_e = jnp.minimum(jnp.sum((pad_end[None, :] <= block_start[:, None]).astype(jnp.int32), axis=1), N_EXPERTS - 1)
    offset = block_start - pad_start[blk_e]
    n_valid = jnp.clip(counts[blk_e] - offset, 0, MOE_ROWS).astype(jnp.int32)
    src = jnp.clip(start[blk_e] + offset, 0, m - 1).astype(jnp.int32)
    blk_e = jnp.where(n_valid > 0, blk_e, jnp.max(jnp.where(n_valid > 0, blk_e, 0))).astype(jnp.int32)

    w_spec = lambda a: pl.BlockSpec((None, None) + a.shape[2:], lambda i, be, nv, sr, od: (layer, be[i], 0, 0))
    grid_spec = pltpu.PrefetchScalarGridSpec(
        num_scalar_prefetch=4,
        grid=(n_blocks,),
        in_specs=[pl.BlockSpec(memory_space=pl.ANY), w_spec(w_gate), w_spec(w_up), w_spec(w_down)],
        out_specs=pl.BlockSpec(memory_space=pl.ANY),
        scratch_shapes=[pltpu.VMEM((MOE_ROWS, d), F32), pltpu.VMEM((MOE_ROWS, d), F32),
                        pltpu.SemaphoreType.DMA(()), pltpu.SemaphoreType.DMA(())],
    )
    y = pl.pallas_call(
        functools.partial(_experts_kernel, n_tokens=n),
        grid_spec=grid_spec,
        out_shape=jax.ShapeDtypeStruct((m + MOE_ROWS, d), F32),
        compiler_params=_cparams(1),
        name="experts",
    )(blk_e, n_valid, src, order, h2, w_gate, w_up, w_down)
    return y


def _combine_kernel(x1_ref, y0_ref, y1_ref, route_ref, gt2_ref, g_ref, o_ref, *, final_norm):
    route = route_ref[...]
    moe = route[:, ROUTE_C1:ROUTE_C1 + 1] * y0_ref[...] + route[:, ROUTE_C2:ROUTE_C2 + 1] * y1_ref[...]
    x2 = x1_ref[...] + gt2_ref[...] * moe
    if final_norm:
        x2 = x2 * lax.rsqrt(jnp.mean(x2 * x2, axis=-1, keepdims=True) + RMS_EPS) * g_ref[...]
    o_ref[...] = x2


def _combine(x1, y, route, mod, mod_spec, g_final, tm, final_norm):
    n, d = x1.shape
    nt = n // tm
    row = lambda width: pl.BlockSpec((tm, width), lambda i: (i, 0))
    return pl.pallas_call(
        functools.partial(_combine_kernel, final_norm=final_norm),
        grid=(nt,),
        in_specs=[row(d), row(d), pl.BlockSpec((tm, d), lambda i: (nt + i, 0)), row(LANES),
                  mod_spec(5), pl.BlockSpec((1, d), lambda i: (0, 0))],
        out_specs=row(d),
        out_shape=jax.ShapeDtypeStruct((n, d), F32),
        compiler_params=_cparams(1),
        name="combine",
    )(x1, y, y, route, mod, g_final.reshape(1, d))


def _rope_tables(pos):
    half = HEAD_DIM // 2
    inv = jnp.power(ROPE_THETA, -jnp.arange(half, dtype=F32) / half)
    ang = pos.astype(F32)[:, None] * inv[None, :]
    cos = jnp.tile(jnp.cos(ang), (1, LANES // half))
    sin = jnp.tile(jnp.sin(ang), (1, LANES // half))
    return cos, sin, cos.T, sin.T


def _sample_rows(a, batch, t):
    return jnp.pad(a.astype(F32).reshape(batch, t, a.shape[-1]), ((0, 0), (0, SAMPLE_ROWS - t), (0, 0)))


def _sample_new_keys(a_t, batch, t):
    r = a_t.shape[1]
    per_b = jnp.transpose(a_t.reshape(r, batch, t), (1, 0, 2))
    return jnp.pad(per_b, ((0, 0), (0, 0), (0, LANES - t)))


def _cache_t(cache):
    depth, n_pool, page = cache.shape[:3]
    return jnp.swapaxes(cache.reshape(depth, n_pool, page, -1), 2, 3)


def kernel(x_prompt, x_sample, c_prompt, c_sample, cache_k_a, cache_v_a, cache_idx_k, cache_k_b, cache_v_b, page_table, w_ada, b_ada, g_mix, w_in, w_proj_a, w_proj_b, w_out, g_ffn, w_router_group, b_router_group, w_router_expert, b_router_expert, w_exp_gate, w_exp_up, w_exp_down, g_final):
    batch, seq, d = x_prompt.shape
    dec_batch, dec_seq, _ = x_sample.shape
    depth = w_in.shape[0]
    past_len = page_table.shape[1] * cache_k_a.shape[2]
    n_p, n_s = batch * seq, dec_batch * dec_seq
    tm_p = min(256, seq)
    tm_s = min(256, n_s)
    assert seq % tm_p == 0 and n_s % tm_s == 0

    mod_all = _ada_mod(jnp.concatenate([c_prompt, c_sample], axis=0), w_ada, b_ada)
    mod_p = mod_all[:, :batch].reshape(depth, batch, 1, 6 * d)
    mod_s = jnp.repeat(mod_all[:, batch:], dec_seq, axis=1)
    w_row, w_col = _split_w_in(w_in)
    w_pa, w_pb, w_o = w_proj_a.astype(BF16), w_proj_b.astype(BF16), w_out.astype(BF16)
    pad_r = LANES - N_GROUPS - N_EXPERTS
    w_r = jnp.pad(jnp.concatenate([w_router_group, w_router_expert], axis=2), ((0, 0), (0, 0), (0, pad_r)))
    b_r = jnp.pad(jnp.concatenate([b_router_group, b_router_expert], axis=1), ((0, 0), (0, pad_r)))
    b_r = b_r.reshape(depth, 1, LANES)
    w_eg, w_eu, w_ed = w_exp_gate.astype(BF16), w_exp_up.astype(BF16), w_exp_down.astype(BF16)
    cache_idx_t = _cache_t(cache_idx_k)
    caches_t = tuple(_cache_t(c) for c in (cache_k_a, cache_v_a, cache_k_b, cache_v_b))

    trig_p = _rope_tables(jnp.arange(seq, dtype=jnp.int32))
    trig_s = _rope_tables(jnp.tile(past_len + jnp.arange(dec_seq, dtype=jnp.int32), dec_batch))
    nq_p = seq // tm_p
    trig_spec_p = pl.BlockSpec((tm_p, LANES), lambda i: (i % nq_p, 0))
    trig_t_spec_p = pl.BlockSpec((LANES, tm_p), lambda i: (0, i % nq_p))
    trig_spec_s = pl.BlockSpec((tm_s, LANES), lambda i: (i, 0))
    trig_t_spec_s = pl.BlockSpec((LANES, tm_s), lambda i: (0, i))

    xp = x_prompt.reshape(n_p, d)
    xs = x_sample.reshape(n_s, d)
    rows_p, rows_s = [], []
    for l in range(depth):
        mod_spec_p = lambda chunk, l=l: pl.BlockSpec((None, None, 1, d), lambda i: (l, i // nq_p, 0, chunk))
        mod_spec_s = lambda chunk, l=l: pl.BlockSpec((None, tm_s, d), lambda i: (l, i, chunk))
        last = l == depth - 1

        qa, qi, wi, qb, ga, gb, ka_t, va_t, ki_t, kb_t, vb_t = _inproj(
            xp, l, mod_p, mod_spec_p, g_mix, w_row, w_col, trig_p, trig_spec_p, trig_t_spec_p, batch, tm_p, BF16)
        rows_p.append((ka_t, va_t, ki_t, kb_t, vb_t))
        ya = _dsa_prompt(qi, wi, qa, ki_t, ka_t, va_t, min(PROMPT_TQ, seq))
        yb = _moba_prompt(qb, kb_t, vb_t, min(PROMPT_TQ, seq))
        x1, h2, route = _mixout(ya, yb, ga, gb, xp, l, mod_p, mod_spec_p, g_ffn, w_pa, w_pb, w_o, w_r, b_r, tm_p)
        y = _experts(h2, route[:, ROUTE_E1:ROUTE_E2 + 1].astype(jnp.int32), l, w_eg, w_eu, w_ed)
        xp = _combine(x1, y, route, mod_p, mod_spec_p, g_final, tm_p, last)

        qa, qi, wi, qb, ga, gb, ka_t, va_t, ki_t, kb_t, vb_t = _inproj(
            xs, l, mod_s, mod_spec_s, g_mix, w_row, w_col, trig_s, trig_spec_s, trig_t_spec_s, 1, tm_s, F32)
        rows_s.append((ka_t, va_t, ki_t, kb_t, vb_t))
        rows8 = lambda a: _sample_rows(a, dec_batch, dec_seq)
        new = lambda a_t: _sample_new_keys(a_t, dec_batch, dec_seq)
        score = _score_sample(rows8(qi), rows8(wi), new(ki_t), cache_idx_t, page_table, l)
        s_len = score.shape[-1]
        bias = _select_sample(score.reshape(dec_batch * SAMPLE_ROWS, s_len),
                              min(IDX_TOPK_MAX, (past_len + dec_seq) // 4), dec_seq)
        ya, yb = _attend_sample(rows8(qa), bias.reshape(dec_batch, SAMPLE_ROWS, s_len), new(ka_t), new(va_t),
                                rows8(qb), new(kb_t), new(vb_t), caches_t, page_table, l, dec_seq)
        unpad = lambda a: a[:, :dec_seq].reshape(n_s, a.shape[-1]).astype(BF16)
        x1, h2, route = _mixout(unpad(ya), unpad(yb), ga, gb, xs, l, mod_s, mod_spec_s, g_ffn, w_pa, w_pb, w_o,
                                w_r, b_r, tm_s)
        y = _experts(h2, route[:, ROUTE_E1:ROUTE_E2 + 1].astype(jnp.int32), l, w_eg, w_eu, w_ed)
        xs = _combine(x1, y, route, mod_s, mod_spec_s, g_final, tm_s, last)

    def stack_p(i, heads):
        a = jnp.stack([r[i] for r in rows_p], axis=0)
        a = jnp.swapaxes(a, 2, 3)
        return a.reshape((depth, batch, seq, heads, HEAD_DIM) if heads else (depth, batch, seq, D_IDX))

    def stack_s(i, heads):
        a = jnp.stack([r[i][0] for r in rows_s], axis=0)
        a = jnp.swapaxes(a, 1, 2)
        return a.reshape((depth, dec_batch, dec_seq, heads, HEAD_DIM) if heads else (depth, dec_batch, dec_seq, D_IDX))

    return (xp.reshape(batch, seq, d), xs.reshape(dec_batch, dec_seq, d),
            stack_p(0, N_KV_A), stack_p(1, N_KV_A), stack_p(3, N_KV_B), stack_p(4, N_KV_B), stack_p(2, 0),
            stack_s(0, N_KV_A), stack_s(1, N_KV_A), stack_s(3, N_KV_B), stack_s(4, N_KV_B), stack_s(2, 0))
```

```python
import functools

import jax
import jax.numpy as jnp
from jax import lax
from jax.experimental import pallas as pl
from jax.experimental.pallas import tpu as pltpu

HEAD_DIM = 64
N_HEADS_A = 8
N_KV_A = 2
N_IDX_HEADS = 4
D_IDX = 64
IDX_TOPK_MAX = 256
N_HEADS_B = 8
N_KV_B = 2
MOBA_BLOCK = 256
MOBA_TOPK_MAX = 3
N_GROUPS = 4
EXPERTS_PER_GROUP = 8
N_EXPERTS = N_GROUPS * EXPERTS_PER_GROUP
TOPK_IN_GROUP = 2
ROPE_THETA = 10000.0
RMS_EPS = 1e-6

LANES = 128
KV_ROWS = N_KV_A * HEAD_DIM
assert KV_ROWS == LANES == N_KV_B * HEAD_DIM and D_IDX == HEAD_DIM and TOPK_IN_GROUP == 2

VMEM_LIMIT = 56 * 1024 * 1024

F32 = jnp.float32
BF16 = jnp.bfloat16
NEG_INF = float("-inf")
INT_MIN = -(2 ** 31)
NEG_INF_KEY = -2139095041
HIGHEST = lax.Precision.HIGHEST


def _cparams(n_grid):
    return pltpu.CompilerParams(dimension_semantics=("arbitrary",) * n_grid, vmem_limit_bytes=VMEM_LIMIT)


def _dot(a, b, precision=None):
    return jnp.dot(a, b, preferred_element_type=F32, precision=precision)


def _dot_nt(a, b, precision=None):
    return lax.dot_general(a, b, (((1,), (1,)), ((), ())), preferred_element_type=F32, precision=precision)


def _sigmoid(x):
    return 1.0 / (1.0 + jnp.exp(-x))


def _ada_kernel(c_ref, w_ref, b_ref, o_ref):
    c = c_ref[...]
    o_ref[...] = _dot((c * _sigmoid(c)).astype(BF16), w_ref[...].astype(BF16)) + b_ref[...]


def _ada_mod(c_all, w_ada, b_ada):
    depth, d, n6 = w_ada.shape
    m = c_all.shape[0]
    tn = 1024
    return pl.pallas_call(
        _ada_kernel,
        grid=(depth, n6 // tn),
        in_specs=[
            pl.BlockSpec((m, d), lambda l, j: (0, 0)),
            pl.BlockSpec((None, d, tn), lambda l, j: (l, 0, j)),
            pl.BlockSpec((None, 1, tn), lambda l, j: (l, 0, j)),
        ],
        out_specs=pl.BlockSpec((None, m, tn), lambda l, j: (l, 0, j)),
        out_shape=jax.ShapeDtypeStruct((depth, m, n6), F32),
        compiler_params=_cparams(2),
        name="ada_mod",
    )(c_all, w_ada, b_ada.reshape(depth, 1, n6))


_QA_W = N_HEADS_A * HEAD_DIM
_QI_W = N_IDX_HEADS * D_IDX
_QB_W = N_HEADS_B * HEAD_DIM
_ROW_QA, _ROW_QI, _ROW_WI, _ROW_QB = 0, _QA_W, _QA_W + _QI_W, _QA_W + _QI_W + LANES
_ROW_GATE = _ROW_QB + _QB_W
_COL_KA, _COL_VA, _COL_KI, _COL_KB, _COL_VB = 0, KV_ROWS, 2 * KV_ROWS, 2 * KV_ROWS + D_IDX, 3 * KV_ROWS + D_IDX
_COL_END = 4 * KV_ROWS + D_IDX


def _rope_lanes(y, cos, sin, lane):
    first_half = (lane % HEAD_DIM) < (HEAD_DIM // 2)
    rot = jnp.where(first_half, -pltpu.roll(y, LANES - HEAD_DIM // 2, 1), pltpu.roll(y, HEAD_DIM // 2, 1))
    return y * cos + rot * sin


def _rope_rows(y, cos, sin):
    half = HEAD_DIM // 2
    parts = []
    for r0 in range(0, y.shape[0], HEAD_DIM):
        parts += [-y[r0 + half:r0 + HEAD_DIM], y[r0:r0 + half]]
    rot = jnp.concatenate(parts, axis=0)
    return y * cos[:y.shape[0]] + rot * sin[:y.shape[0]]


def _inproj_kernel(x_ref, sc_ref, sh_ref, g_ref, wr_ref, wc_ref, cos_ref, sin_ref, cos_t_ref, sin_t_ref,
                   qa_ref, qi_ref, wi_ref, qb_ref, ga_ref, gb_ref, ka_ref, va_ref, ki_ref, kb_ref, vb_ref,
                   *, d_model):
    x = x_ref[...]
    h = x * lax.rsqrt(jnp.mean(x * x, axis=-1, keepdims=True) + RMS_EPS) * g_ref[...]
    h = h * (1.0 + sc_ref[...]) + sh_ref[...]
    hb = h.astype(BF16)
    cos = cos_ref[...]
    sin = sin_ref[...]
    lane = lax.broadcasted_iota(jnp.int32, cos.shape, 1)

    def proj(c0, width):
        return _dot(hb, wr_ref[:, c0:c0 + width])

    def roped(c0, out_ref):
        for c in range(out_ref.shape[1] // LANES):
            y = _rope_lanes(proj(c0 + c * LANES, LANES), cos, sin, lane)
            out_ref[:, c * LANES:(c + 1) * LANES] = y.astype(out_ref.dtype)

    roped(_ROW_QA, qa_ref)
    roped(_ROW_QI, qi_ref)
    wi_ref[...] = proj(_ROW_WI, LANES)
    roped(_ROW_QB, qb_ref)
    for c in range(d_model // 512):
        ga_ref[:, c * 512:(c + 1) * 512] = _sigmoid(proj(_ROW_GATE + c * 512, 512))
        gb_ref[:, c * 512:(c + 1) * 512] = _sigmoid(proj(_ROW_GATE + d_model + c * 512, 512))

    cos_t = cos_t_ref[...]
    sin_t = sin_t_ref[...]

    def proj_t(r0, r1):
        return _dot_nt(wc_ref[r0:r1, :], hb)

    ka_ref[...] = _rope_rows(proj_t(_COL_KA, _COL_VA), cos_t, sin_t)
    va_ref[...] = proj_t(_COL_VA, _COL_KI)
    ki_ref[...] = _rope_rows(proj_t(_COL_KI, _COL_KB), cos_t, sin_t)
    kb_ref[...] = _rope_rows(proj_t(_COL_KB, _COL_VB), cos_t, sin_t)
    vb_ref[...] = proj_t(_COL_VB, _COL_END)


def _split_w_in(w_in):
    depth, d, d_in = w_in.shape
    widths = (_QA_W, KV_ROWS, KV_ROWS, _QI_W, D_IDX, N_IDX_HEADS, _QB_W, KV_ROWS, KV_ROWS, d, d)
    offs = [0]
    for w in widths:
        offs.append(offs[-1] + w)
    assert offs[-1] == d_in
    w_t = jnp.transpose(w_in, (2, 0, 1))
    qa, ka, va, qi, ki, wi, qb, kb, vb, ga, gb = [w_t[offs[i]:offs[i + 1]] for i in range(11)]
    pad = jnp.zeros((LANES - N_IDX_HEADS, depth, d), w_in.dtype)
    w_row = jnp.transpose(jnp.concatenate([qa, qi, wi, pad, qb, ga, gb], axis=0).astype(BF16), (1, 2, 0))
    w_col = jnp.transpose(jnp.concatenate([ka, va, ki, kb, vb], axis=0).astype(BF16), (1, 0, 2))
    return w_row, w_col


def _inproj(x, layer, mod, mod_spec, g_mix, w_row, w_col, trig, trig_spec, trig_t_spec, n_batch, tm, q_dtype):
    n, d = x.shape
    per_batch = n // n_batch
    nq = per_batch // tm
    cos, sin, cos_t, sin_t = trig
    row = lambda width: pl.BlockSpec((tm, width), lambda i: (i, 0))
    col = lambda rows: pl.BlockSpec((None, rows, tm), lambda i: (i // nq, 0, i % nq))
    row_out = [(_QA_W, q_dtype), (_QI_W, q_dtype), (LANES, F32), (_QB_W, F32), (d, F32), (d, F32)]
    col_out = [KV_ROWS, KV_ROWS, D_IDX, KV_ROWS, KV_ROWS]
    layer_w = lambda a: pl.BlockSpec((None,) + a.shape[1:], lambda i: (layer, 0, 0))
    return pl.pallas_call(
        functools.partial(_inproj_kernel, d_model=d),
        grid=(n // tm,),
        in_specs=[row(d), mod_spec(1), mod_spec(0), pl.BlockSpec((None, 1, d), lambda i: (layer, 0, 0)),
                  layer_w(w_row), layer_w(w_col), trig_spec, trig_spec, trig_t_spec, trig_t_spec],
        out_specs=[row(w) for w, _ in row_out] + [col(r) for r in col_out],
        out_shape=[jax.ShapeDtypeStruct((n, w), dt) for w, dt in row_out]
        + [jax.ShapeDtypeStruct((n_batch, r, per_batch), F32) for r in col_out],
        compiler_params=_cparams(1),
        name="inproj",
    )(x, mod, mod, g_mix.reshape(g_mix.shape[0], 1, d), w_row, w_col, cos, sin, cos_t, sin_t)


def _stack_heads(q, first_head, n_heads, scale=None):
    q = q.astype(F32)
    if scale is not None:
        q = q * scale
    parts = [q[:, (first_head + g) * HEAD_DIM:(first_head + g + 1) * HEAD_DIM] for g in range(n_heads)]
    return jnp.concatenate(parts, axis=0).astype(BF16)


def _sort_key(x):
    bits = lax.bitcast_convert_type(x, jnp.int32)
    return bits ^ ((bits >> 31) & 0x7FFFFFFF)


def _count(mask):
    return jnp.sum(jnp.where(mask, 1.0, 0.0), axis=1, keepdims=True)


def _indexer_scores(qi, wi, ki_t, q_pos):
    t = qi.shape[0]
    logits = _dot(_stack_heads(qi, 0, N_IDX_HEADS), ki_t)
    score = None
    for h in range(N_IDX_HEADS):
        term = jnp.maximum(logits[h * t:(h + 1) * t], 0.0) * wi[:, h:h + 1]
        score = term if score is None else score + term
    k_pos = lax.broadcasted_iota(jnp.int32, score.shape, 1)
    score = jnp.where(score == 0.0, 0.0, score)
    return jnp.where(k_pos <= q_pos, score, NEG_INF)


def _topk_mask(score, n_keep, row_is_real, bias_ref):
    t, s_len = score.shape
    key = _sort_key(score)
    keep = float(n_keep)
    thr = jnp.where(_count(key >= 0) >= keep, 0, INT_MIN).astype(jnp.int32)

    def two_bits(i, thr):
        hi = jnp.left_shift(jnp.int32(1), 30 - 2 * i)
        lo = jnp.left_shift(jnp.int32(1), 29 - 2 * i)
        c_hi, c_lo, c_both = thr + hi, thr + lo, thr + hi + lo
        n_hi, n_lo, n_both = _count(key >= c_hi), _count(key >= c_lo), _count(key >= c_both)
        return jnp.where(n_both >= keep, c_both, jnp.where(n_hi >= keep, c_hi, jnp.where(n_lo >= keep, c_lo, thr)))

    thr = lax.fori_loop(0, 15, two_bits, thr)
    thr = jnp.where(_count(key >= thr + 1) >= keep, thr + 1, thr)
    at_least = key >= thr
    n_above = _count(key > thr)
    bias_ref[...] = jnp.where(at_least & (key > NEG_INF_KEY), 0.0, NEG_INF)

    tied = (_count(at_least) > keep) & (thr > NEG_INF_KEY) & row_is_real

    @pl.when(jnp.max(jnp.where(tied, 1.0, 0.0)) > 0.0)
    def _():
        free = keep - n_above
        r_i = lax.broadcasted_iota(jnp.int32, (MOBA_BLOCK, MOBA_BLOCK), 0)
        c_i = lax.broadcasted_iota(jnp.int32, (MOBA_BLOCK, MOBA_BLOCK), 1)
        tri = jnp.where(r_i <= c_i, 1.0, 0.0).astype(BF16)
        seen = jnp.zeros((t, 1), F32)
        for c0 in range(0, s_len, MOBA_BLOCK):
            w = min(MOBA_BLOCK, s_len - c0)
            key_c = key[:, c0:c0 + w]
            eq_c = jnp.where(key_c == thr, 1.0, 0.0)
            rank = _dot(eq_c.astype(BF16), tri[:w, :w]) + seen
            keep_c = ((key_c > thr) | ((key_c == thr) & (rank <= free))) & (key_c > NEG_INF_KEY)
            bias_ref[:, c0:c0 + w] = jnp.where(keep_c, 0.0, NEG_INF)
            seen = seen + jnp.sum(eq_c, axis=1, keepdims=True)


def _softmax_pv(s, v_t):
    m = jnp.max(s, axis=1, keepdims=True)
    p = jnp.exp(s - m)
    l = jnp.sum(p, axis=1, keepdims=True)
    return _dot_nt(p.astype(BF16), v_t) / l


def _attend_group(q_g, k_t, v_t, bias):
    s = _dot(q_g, k_t)
    g = s.shape[0] // bias.shape[0]
    s = (s.reshape(g, bias.shape[0], s.shape[1]) + bias[None]).reshape(s.shape)
    return _softmax_pv(s, v_t)


def _store_heads(o_g, t, first_head, n_heads, out_ref):
    for g in range(0, n_heads, 2):
        pair = jnp.concatenate([o_g[g * t:(g + 1) * t], o_g[(g + 1) * t:(g + 2) * t]], axis=1)
        c0 = (first_head + g) * HEAD_DIM
        out_ref[:, c0:c0 + 2 * HEAD_DIM] = pair.astype(out_ref.dtype)


def _dsa_attention(qa, ka_t, va_t, bias, out_ref):
    t = qa.shape[0]
    hpk = N_HEADS_A // N_KV_A
    for kv in range(N_KV_A):
        k_t = ka_t[kv * HEAD_DIM:(kv + 1) * HEAD_DIM].astype(BF16)
        v_t = va_t[kv * HEAD_DIM:(kv + 1) * HEAD_DIM].astype(BF16)
        o_g = _attend_group(_stack_heads(qa, kv * hpk, hpk, HEAD_DIM ** -0.5), k_t, v_t, bias)
        _store_heads(o_g, t, kv * hpk, hpk, out_ref)


GATE_ROWS = 8


def _block_means(kb_t, n_full):
    lane = lax.broadcasted_iota(jnp.int32, (KV_ROWS, LANES), 1)
    km = jnp.zeros((KV_ROWS, LANES), F32)
    for n in range(n_full):
        col = jnp.sum(kb_t[:, n * MOBA_BLOCK:(n + 1) * MOBA_BLOCK], axis=1, keepdims=True) * (1.0 / MOBA_BLOCK)
        km = jnp.where(lane == n, col, km)
    return km.T


def _moba_attention(qb, kb_t, vb_t, k_mean, q_pos, own, own_min, n_sel, n_full, s_ref, out_ref):
    assert n_full <= GATE_ROWS
    t = qb.shape[0]
    s_len = kb_t.shape[1]
    hpk = N_HEADS_B // N_KV_B
    rows = hpk * t
    rows_pad = -(-rows // LANES) * LANES
    blk = lax.broadcasted_iota(jnp.int32, (GATE_ROWS, rows_pad), 0)
    blk_f = blk.astype(F32)
    q_pos_g = jnp.concatenate([q_pos] * hpk, axis=0)
    for kv in range(N_KV_B):
        q_f = jnp.concatenate([qb[:, (kv * hpk + g) * HEAD_DIM:(kv * hpk + g + 1) * HEAD_DIM] for g in range(hpk)],
                              axis=0)
        q_gate = q_f
        if rows_pad != rows:
            q_gate = jnp.concatenate([q_f, jnp.zeros((rows_pad - rows, HEAD_DIM), F32)], axis=0)
        gate = _dot_nt(k_mean[:GATE_ROWS, kv * HEAD_DIM:(kv + 1) * HEAD_DIM], q_gate, precision=HIGHEST)
        gate = jnp.where(blk < jnp.minimum(own, n_full), gate, NEG_INF)
        picked_t = jnp.zeros((GATE_ROWS, rows_pad), F32)
        for _ in range(n_sel):
            best = jnp.max(gate, axis=0, keepdims=True)
            first = jnp.min(jnp.where(gate == best, blk_f, float(GATE_ROWS)), axis=0, keepdims=True)
            hit = (blk_f == first) & (best > NEG_INF)
            picked_t = jnp.where(hit, 1.0, picked_t)
            gate = jnp.where(blk_f == first, NEG_INF, gate)
        picked_t = jnp.concatenate([picked_t, jnp.zeros((LANES - GATE_ROWS, rows_pad), F32)], axis=0)
        picked = picked_t.T[:rows]
        q_g = (q_f * HEAD_DIM ** -0.5).astype(BF16)
        k_t = kb_t[kv * HEAD_DIM:(kv + 1) * HEAD_DIM].astype(BF16)
        v_t = vb_t[kv * HEAD_DIM:(kv + 1) * HEAD_DIM].astype(BF16)
        for c0 in range(0, s_len, MOBA_BLOCK):
            c = c0 // MOBA_BLOCK
            w = min(MOBA_BLOCK, s_len - c0)
            if c < own_min:
                allowed = picked[:, c:c + 1] > 0.0
            else:
                k_pos = c0 + lax.broadcasted_iota(jnp.int32, (rows, w), 1)
                allowed = (k_pos <= q_pos_g) & (own == c)
                if c < n_full:
                    allowed = allowed | (picked[:, c:c + 1] > 0.0)
            s_ref[:, c0:c0 + w] = jnp.where(allowed, _dot(q_g, k_t[:, c0:c0 + w]), NEG_INF)
        _store_heads(_softmax_pv(s_ref[:, :s_len], v_t), t, kv * hpk, hpk, out_ref)


PROMPT_TQ = 128
KEY_SEGMENT = MOBA_BLOCK


def _per_key_segment(seq, tq, body):
    n_seg = max(1, seq // KEY_SEGMENT)
    seg_len = seq // n_seg
    assert seg_len % tq == 0 and seg_len % MOBA_BLOCK == 0
    seg = (pl.program_id(1) * tq) // seg_len
    for k in range(n_seg):
        pl.when(seg == k)(functools.partial(body, (k + 1) * seg_len, seg_len))


def _dsa_prompt_kernel(qi_ref, wi_ref, qa_ref, ki_ref, ka_ref, va_ref, o_ref, bias_ref, *, tq, seq, n_keep):
    q_pos = pl.program_id(1) * tq + lax.broadcasted_iota(jnp.int32, (tq, 1), 0)

    def body(s_len, seg_len):
        score = _indexer_scores(qi_ref[...], wi_ref[...], ki_ref[:, :s_len].astype(BF16), q_pos)
        _topk_mask(score, n_keep, q_pos >= 0, bias_ref.at[:, :s_len])
        _dsa_attention(qa_ref[...], ka_ref[:, :s_len], va_ref[:, :s_len], bias_ref[:, :s_len], o_ref)

    _per_key_segment(seq, tq, body)


def _dsa_prompt(qi, wi, qa, ki_t, ka_t, va_t, tq):
    batch, _, seq = ka_t.shape
    nq = seq // tq
    qrow = lambda w: pl.BlockSpec((tq, w), lambda b, i: (b * nq + i, 0))
    keys = lambda rows: pl.BlockSpec((None, rows, seq), lambda b, i: (b, 0, 0))
    return pl.pallas_call(
        functools.partial(_dsa_prompt_kernel, tq=tq, seq=seq, n_keep=min(IDX_TOPK_MAX, seq // 4)),
        grid=(batch, nq),
        in_specs=[qrow(qi.shape[1]), qrow(LANES), qrow(qa.shape[1]), keys(D_IDX), keys(KV_ROWS), keys(KV_ROWS)],
        out_specs=qrow(qa.shape[1]),
        out_shape=jax.ShapeDtypeStruct(qa.shape, BF16),
        scratch_shapes=[pltpu.VMEM((tq, seq), F32)],
        compiler_params=_cparams(2),
        name="dsa_prompt",
    )(qi, wi, qa, ki_t, ka_t, va_t)


def _moba_prompt_kernel(qb_ref, kb_ref, vb_ref, o_ref, bias_ref, km_ref, *, tq, seq, n_sel):
    base = pl.program_id(1) * tq
    q_pos = base + lax.broadcasted_iota(jnp.int32, (tq, 1), 0)
    n_full = seq // MOBA_BLOCK

    @pl.when(pl.program_id(1) == 0)
    def _():
        km_ref[...] = _block_means(kb_ref[...], n_full)

    def body(s_len, seg_len):
        _moba_attention(qb_ref[...], kb_ref[:, :s_len], vb_ref[:, :s_len], km_ref[...], q_pos, base // MOBA_BLOCK,
                        (s_len - seg_len) // MOBA_BLOCK, n_sel, n_full, bias_ref, o_ref)

    _per_key_segment(seq, tq, body)


def _moba_prompt(qb, kb_t, vb_t, tq):
    batch, _, seq = kb_t.shape
    assert MOBA_BLOCK % tq == 0 and seq % MOBA_BLOCK == 0
    nq = seq // tq
    qrow = pl.BlockSpec((tq, qb.shape[1]), lambda b, i: (b * nq + i, 0))
    keys = pl.BlockSpec((None, KV_ROWS, seq), lambda b, i: (b, 0, 0))
    return pl.pallas_call(
        functools.partial(_moba_prompt_kernel, tq=tq, seq=seq, n_sel=min(MOBA_TOPK_MAX, seq // MOBA_BLOCK)),
        grid=(batch, nq),
        in_specs=[qrow, keys, keys],
        out_specs=qrow,
        out_shape=jax.ShapeDtypeStruct(qb.shape, BF16),
        scratch_shapes=[pltpu.VMEM((N_HEADS_B // N_KV_B * tq, seq), F32), pltpu.VMEM((KV_ROWS, LANES), F32)],
        compiler_params=_cparams(2),
        name="moba_prompt",
    )(qb, kb_t, vb_t)


SAMPLE_ROWS = 8


def _assemble_keys(page_refs, new_ref, dst_ref, page_size):
    for j, page in enumerate(page_refs):
        dst_ref[:, j * page_size:(j + 1) * page_size] = page[...]
    dst_ref[:, len(page_refs) * page_size:] = new_ref[...]


def _page_specs(layer, n_pages, rows, page_size):
    return [pl.BlockSpec((None, None, rows, page_size), lambda b, pt, j=j: (layer, pt[b, j], 0, 0))
            for j in range(n_pages)]


def _score_sample_kernel(pt_ref, qi_ref, wi_ref, kin_ref, *rest, n_pages, page_size):
    ki_pages = rest[:n_pages]
    o_ref, ki_s = rest[n_pages:]
    _assemble_keys(ki_pages, kin_ref, ki_s, page_size)
    q_pos = n_pages * page_size + lax.broadcasted_iota(jnp.int32, (SAMPLE_ROWS, 1), 0)
    o_ref[...] = _indexer_scores(qi_ref[...], wi_ref[...], ki_s[...].astype(BF16), q_pos)


def _score_sample(qi, wi, ki_new, cache_idx_t, page_table, layer):
    b, n_pages = page_table.shape
    page_size = cache_idx_t.shape[3]
    s_len = n_pages * page_size + LANES
    per_b = lambda r, w: pl.BlockSpec((None, r, w), lambda i, pt: (i, 0, 0))
    grid_spec = pltpu.PrefetchScalarGridSpec(
        num_scalar_prefetch=1,
        grid=(b,),
        in_specs=[per_b(SAMPLE_ROWS, qi.shape[2]), per_b(SAMPLE_ROWS, LANES), per_b(D_IDX, LANES)]
        + _page_specs(layer, n_pages, D_IDX, page_size),
        out_specs=per_b(SAMPLE_ROWS, s_len),
        scratch_shapes=[pltpu.VMEM((D_IDX, s_len), F32)],
    )
    return pl.pallas_call(
        functools.partial(_score_sample_kernel, n_pages=n_pages, page_size=page_size),
        grid_spec=grid_spec,
        out_shape=jax.ShapeDtypeStruct((b, SAMPLE_ROWS, s_len), F32),
        compiler_params=_cparams(1),
        name="score_sample",
    )(page_table, qi, wi, ki_new, *([cache_idx_t] * n_pages))


def _select_kernel(score_ref, bias_ref, *, n_keep, dec_seq):
    rows = score_ref.shape[0]
    row = lax.broadcasted_iota(jnp.int32, (rows, 1), 0)
    _topk_mask(score_ref[...], n_keep, (row % SAMPLE_ROWS) < dec_seq, bias_ref)


def _select_sample(score, n_keep, dec_seq):
    n, s_len = score.shape
    tr = min(256, n)
    assert n % tr == 0 and tr % SAMPLE_ROWS == 0
    spec = pl.BlockSpec((tr, s_len), lambda i: (i, 0))
    return pl.pallas_call(
        functools.partial(_select_kernel, n_keep=n_keep, dec_seq=dec_seq),
        grid=(n // tr,),
        in_specs=[spec],
        out_specs=spec,
        out_shape=jax.ShapeDtypeStruct((n, s_len), F32),
        compiler_params=_cparams(1),
        name="select_sample",
    )(score)


SAMPLE_GROUP = 1


def _attend_sample_kernel(pt_ref, qa_ref, bias_ref, kan_ref, van_ref, qb_ref, kbn_ref, vbn_ref, *rest,
                          n_pages, page_size, n_sel):
    n_page_refs = 4 * SAMPLE_GROUP * n_pages
    ya_ref, yb_ref = rest[n_page_refs:n_page_refs + 2]
    scratch = rest[n_page_refs + 2:]
    past = n_pages * page_size
    q_pos = past + lax.broadcasted_iota(jnp.int32, (SAMPLE_ROWS, 1), 0)
    n_full = past // MOBA_BLOCK
    for g in range(SAMPLE_GROUP):
        ka_s, va_s, kb_s, vb_s, s_ref = scratch[5 * g:5 * g + 5]
        for k, (new_ref, dst) in enumerate(zip((kan_ref, van_ref, kbn_ref, vbn_ref), (ka_s, va_s, kb_s, vb_s))):
            first = (k * SAMPLE_GROUP + g) * n_pages
            _assemble_keys(rest[first:first + n_pages], new_ref.at[g], dst, page_size)
        _dsa_attention(qa_ref[g], ka_s[...], va_s[...], bias_ref[g], ya_ref.at[g])
        kb_t = kb_s[...]
        _moba_attention(qb_ref[g], kb_t, vb_s[...], _block_means(kb_t, n_full), q_pos, n_full, n_full, n_sel,
                        n_full, s_ref, yb_ref.at[g])


def _attend_sample(qa, bias, ka_new, va_new, qb, kb_new, vb_new, caches_t, page_table, layer, dec_seq):
    b, n_pages = page_table.shape
    page_size = caches_t[0].shape[3]
    past = n_pages * page_size
    assert past % MOBA_BLOCK == 0 and dec_seq <= SAMPLE_ROWS and b % SAMPLE_GROUP == 0
    s_len = past + LANES
    n_blocks = -(-(past + dec_seq) // MOBA_BLOCK)
    per_step = lambda r, w: pl.BlockSpec((SAMPLE_GROUP, r, w), lambda i, pt: (i, 0, 0))
    new = per_step(KV_ROWS, LANES)
    q = per_step(SAMPLE_ROWS, qa.shape[2])
    page_specs = [pl.BlockSpec((None, None, KV_ROWS, page_size),
                               lambda i, pt, g=g, j=j: (layer, pt[i * SAMPLE_GROUP + g, j], 0, 0))
                  for _ in range(4) for g in range(SAMPLE_GROUP) for j in range(n_pages)]
    grid_spec = pltpu.PrefetchScalarGridSpec(
        num_scalar_prefetch=1,
        grid=(b // SAMPLE_GROUP,),
        in_specs=[q, per_step(SAMPLE_ROWS, s_len), new, new, q, new, new] + page_specs,
        out_specs=[q, q],
        scratch_shapes=([pltpu.VMEM((KV_ROWS, s_len), F32)] * 4
                        + [pltpu.VMEM((N_HEADS_B // N_KV_B * SAMPLE_ROWS, s_len), F32)]) * SAMPLE_GROUP,
    )
    return pl.pallas_call(
        functools.partial(_attend_sample_kernel, n_pages=n_pages, page_size=page_size,
                          n_sel=min(MOBA_TOPK_MAX, n_blocks)),
        grid_spec=grid_spec,
        out_shape=[jax.ShapeDtypeStruct(qa.shape, F32)] * 2,
        compiler_params=_cparams(1),
        name="attend_sample",
    )(page_table, qa, bias, ka_new, va_new, qb, kb_new, vb_new,
      *[c for c in caches_t for _ in range(SAMPLE_GROUP * n_pages)])


ROUTE_E1, ROUTE_E2, ROUTE_C1, ROUTE_C2 = 0, 1, 2, 3


def _mixout_kernel(ya_ref, yb_ref, ga_ref, gb_ref, x_ref, gt1_ref, sc2_ref, sh2_ref, g_ref,
                   wpa_ref, wpb_ref, wout_ref, wr_ref, br_ref, x1_ref, h2_ref, route_ref):
    merged = ga_ref[...] * _dot(ya_ref[...], wpa_ref[...]) + gb_ref[...] * _dot(yb_ref[...], wpb_ref[...])
    mix = _dot(merged.astype(BF16), wout_ref[...])
    x1 = x_ref[...] + gt1_ref[...] * mix
    x1_ref[...] = x1
    h2 = x1 * lax.rsqrt(jnp.mean(x1 * x1, axis=-1, keepdims=True) + RMS_EPS) * g_ref[...]
    h2 = h2 * (1.0 + sc2_ref[...]) + sh2_ref[...]
    h2_ref[...] = h2

    logits = _dot(h2, wr_ref[...], precision=HIGHEST) + br_ref[...]
    lane = lax.broadcasted_iota(jnp.int32, logits.shape, 1)
    lane_f = lane.astype(F32)
    is_grp = lane < N_GROUPS
    grp = jnp.where(is_grp, logits, NEG_INF)
    g_max = jnp.max(grp, axis=1, keepdims=True)
    g_star = jnp.min(jnp.where(grp == g_max, lane_f, float(N_GROUPS)), axis=1, keepdims=True)
    p_g = 1.0 / jnp.sum(jnp.where(is_grp, jnp.exp(grp - g_max), 0.0), axis=1, keepdims=True)
    e_id = lane_f - float(N_GROUPS)
    in_grp = (e_id >= g_star * EXPERTS_PER_GROUP) & (e_id < (g_star + 1.0) * EXPERTS_PER_GROUP)
    cand = jnp.where(in_grp, logits, NEG_INF)
    l1 = jnp.max(cand, axis=1, keepdims=True)
    e1 = jnp.min(jnp.where(cand == l1, e_id, float(N_EXPERTS)), axis=1, keepdims=True)
    cand = jnp.where(e_id == e1, NEG_INF, cand)
    l2 = jnp.max(cand, axis=1, keepdims=True)
    e2 = jnp.min(jnp.where(cand == l2, e_id, float(N_EXPERTS)), axis=1, keepdims=True)
    t = jnp.exp(l2 - l1)
    c1 = p_g / (1.0 + t)
    c2 = p_g * t / (1.0 + t)
    route = jnp.where(lane == ROUTE_E1, e1, 0.0)
    route = jnp.where(lane == ROUTE_E2, e2, route)
    route = jnp.where(lane == ROUTE_C1, c1, route)
    route_ref[...] = jnp.where(lane == ROUTE_C2, c2, route)


def _mixout(ya, yb, ga, gb, x, layer, mod, mod_spec, g_ffn, w_pa, w_pb, w_out, w_r, b_r, tm):
    n, d = x.shape
    row = lambda width: pl.BlockSpec((tm, width), lambda i: (i, 0))
    layer_w = lambda a: pl.BlockSpec((None,) + a.shape[1:], lambda i: (layer, 0, 0))
    return pl.pallas_call(
        _mixout_kernel,
        grid=(n // tm,),
        in_specs=[row(ya.shape[1]), row(yb.shape[1]), row(d), row(d), row(d),
                  mod_spec(2), mod_spec(4), mod_spec(3), pl.BlockSpec((None, 1, d), lambda i: (layer, 0, 0)),
                  layer_w(w_pa), layer_w(w_pb), layer_w(w_out), layer_w(w_r), layer_w(b_r)],
        out_specs=[row(d), row(d), row(LANES)],
        out_shape=[jax.ShapeDtypeStruct((n, d), F32), jax.ShapeDtypeStruct((n, d), F32),
                   jax.ShapeDtypeStruct((n, LANES), F32)],
        compiler_params=_cparams(1),
        name="mixout",
    )(ya, yb, ga, gb, x, mod, mod, mod, g_ffn.reshape(g_ffn.shape[0], 1, d), w_pa, w_pb, w_out, w_r, b_r)


MOE_ROWS = 256
ROW_CHUNK = 32
DMA_UNROLL = 8
N_BUF = 2


def _experts_kernel(blk_e_ref, n_valid_ref, src_ref, order_ref, h_hbm, wg_ref, wu_ref, wd_ref, y_hbm,
                    x_buf, y_buf, gather_sem, scatter_sem, *, n_tokens, n_blocks):
    i = pl.program_id(0)
    buf = i % N_BUF
    n_slots = TOPK_IN_GROUP * n_tokens

    def n_chunks(b):
        return jnp.right_shift(n_valid_ref[b] + (ROW_CHUNK - 1), ROW_CHUNK.bit_length() - 1)

    def slot_of(b, r):
        return order_ref[jnp.minimum(src_ref[b] + r, n_slots - 1)]

    def for_rows(b, fn):
        def chunk(c, carry):
            def row(u, carry):
                fn(c * ROW_CHUNK + u)
                return carry
            return lax.fori_loop(0, ROW_CHUNK, row, carry, unroll=DMA_UNROLL)
        lax.fori_loop(0, n_chunks(b), chunk, 0)

    def for_chunks(b, fn):
        def chunk(c, carry):
            fn(pl.multiple_of(c * ROW_CHUNK, ROW_CHUNK))
            return carry
        lax.fori_loop(0, n_chunks(b), chunk, 0)

    def start_gather(b, bf):
        def row(r):
            slot = slot_of(b, r)
            token = jnp.where(slot >= n_tokens, slot - n_tokens, slot)
            pltpu.make_async_copy(h_hbm.at[pl.ds(token, 1)], x_buf.at[bf, pl.ds(r, 1)], gather_sem.at[bf]).start()
        for_rows(b, row)

    def wait_gather(b, bf):
        for_chunks(b, lambda r0: pltpu.make_async_copy(
            y_hbm.at[pl.ds(0, ROW_CHUNK)], x_buf.at[bf, pl.ds(r0, ROW_CHUNK)], gather_sem.at[bf]).wait())

    def start_scatter(b, bf):
        n_valid = n_valid_ref[b]

        def row(r):
            dst = jnp.where(r < n_valid, slot_of(b, r), n_slots + bf * MOE_ROWS + r)
            pltpu.make_async_copy(y_buf.at[bf, pl.ds(r, 1)], y_hbm.at[pl.ds(dst, 1)], scatter_sem.at[bf]).start()
        for_rows(b, row)

    def wait_scatter(b, bf):
        for_chunks(b, lambda r0: pltpu.make_async_copy(
            y_buf.at[bf, pl.ds(r0, ROW_CHUNK)], y_hbm.at[pl.ds(0, ROW_CHUNK)], scatter_sem.at[bf]).wait())

    @pl.when(i == 0)
    def _():
        x_buf[...] = jnp.zeros_like(x_buf)
        y_buf[...] = jnp.zeros_like(y_buf)
        for bf in range(N_BUF):
            init = pltpu.make_async_copy(y_buf.at[bf], y_hbm.at[pl.ds(n_slots + bf * MOE_ROWS, MOE_ROWS)],
                                         scatter_sem.at[bf])
            init.start()
            init.wait()
        start_gather(0, 0)

    @pl.when(i + 1 < n_blocks)
    def _():
        start_gather(i + 1, 1 - buf)

    wait_gather(i, buf)

    @pl.when(n_valid_ref[i] > 0)
    def _():
        xb = x_buf[buf].astype(BF16)
        gate = _dot(xb, wg_ref[...].astype(BF16))
        hidden = gate * _sigmoid(gate) * _dot(xb, wu_ref[...].astype(BF16))
        y_buf[buf] = _dot(hidden.astype(BF16), wd_ref[...].astype(BF16))

    start_scatter(i, buf)

    @pl.when(i > 0)
    def _():
        wait_scatter(i - 1, 1 - buf)

    @pl.when(i == n_blocks - 1)
    def _():
        wait_scatter(i, buf)


def _experts(h2, expert_ids, layer, w_gate, w_up, w_down):
    n, d = h2.shape
    m = TOPK_IN_GROUP * n
    n_blocks = -(-m // MOE_ROWS) + N_EXPERTS
    flat_e = expert_ids.T.reshape(m)
    order = jnp.argsort(flat_e).astype(jnp.int32)
    experts = jnp.arange(N_EXPERTS, dtype=jnp.int32)
    counts = jnp.sum((flat_e[:, None] == experts[None, :]).astype(jnp.int32), axis=0)
    padded = (counts + MOE_ROWS - 1) // MOE_ROWS * MOE_ROWS
    pad_end = jnp.cumsum(padded)
    pad_start = pad_end - padded
    start = jnp.cumsum(counts) - counts
    block_start = jnp.arange(n_blocks, dtype=jnp.int32) * MOE_ROWS
    blk_e = jnp.minimum(jnp.sum((pad_end[None, :] <= block_start[:, None]).astype(jnp.int32), axis=1), N_EXPERTS - 1)
    offset = block_start - pad_start[blk_e]
    n_valid = jnp.clip(counts[blk_e] - offset, 0, MOE_ROWS).astype(jnp.int32)
    src = jnp.clip(start[blk_e] + offset, 0, m - 1).astype(jnp.int32)
    blk_e = jnp.where(n_valid > 0, blk_e, jnp.max(jnp.where(n_valid > 0, blk_e, 0))).astype(jnp.int32)

    w_spec = lambda a: pl.BlockSpec((None, None) + a.shape[2:], lambda i, be, nv, sr, od: (layer, be[i], 0, 0))
    grid_spec = pltpu.PrefetchScalarGridSpec(
        num_scalar_prefetch=4,
        grid=(n_blocks,),
        in_specs=[pl.BlockSpec(memory_space=pl.ANY), w_spec(w_gate), w_spec(w_up), w_spec(w_down)],
        out_specs=pl.BlockSpec(memory_space=pl.ANY),
        scratch_shapes=[pltpu.VMEM((N_BUF, MOE_ROWS, d), F32), pltpu.VMEM((N_BUF, MOE_ROWS, d), F32),
                        pltpu.SemaphoreType.DMA((N_BUF,)), pltpu.SemaphoreType.DMA((N_BUF,))],
    )
    return pl.pallas_call(
        functools.partial(_experts_kernel, n_tokens=n, n_blocks=n_blocks),
        grid_spec=grid_spec,
        out_shape=jax.ShapeDtypeStruct((m + N_BUF * MOE_ROWS, d), F32),
        compiler_params=_cparams(1),
        name="experts",
    )(blk_e, n_valid, src, order, h2, w_gate, w_up, w_down)


def _combine_kernel(x1_ref, y0_ref, y1_ref, route_ref, gt2_ref, g_ref, o_ref, *, final_norm):
    route = route_ref[...]
    moe = route[:, ROUTE_C1:ROUTE_C1 + 1] * y0_ref[...] + route[:, ROUTE_C2:ROUTE_C2 + 1] * y1_ref[...]
    x2 = x1_ref[...] + gt2_ref[...] * moe
    if final_norm:
        x2 = x2 * lax.rsqrt(jnp.mean(x2 * x2, axis=-1, keepdims=True) + RMS_EPS) * g_ref[...]
    o_ref[...] = x2


def _combine(x1, y, route, mod, mod_spec, g_final, tm, final_norm):
    n, d = x1.shape
    nt = n // tm
    row = lambda width: pl.BlockSpec((tm, width), lambda i: (i, 0))
    return pl.pallas_call(
        functools.partial(_combine_kernel, final_norm=final_norm),
        grid=(nt,),
        in_specs=[row(d), row(d), pl.BlockSpec((tm, d), lambda i: (nt + i, 0)), row(LANES),
                  mod_spec(5), pl.BlockSpec((1, d), lambda i: (0, 0))],
        out_specs=row(d),
        out_shape=jax.ShapeDtypeStruct((n, d), F32),
        compiler_params=_cparams(1),
        name="combine",
    )(x1, y, y, route, mod, g_final.reshape(1, d))


def _rope_tables(pos):
    half = HEAD_DIM // 2
    inv = jnp.power(ROPE_THETA, -jnp.arange(half, dtype=F32) / half)
    ang = pos.astype(F32)[:, None] * inv[None, :]
    cos = jnp.tile(jnp.cos(ang), (1, LANES // half))
    sin = jnp.tile(jnp.sin(ang), (1, LANES // half))
    return cos, sin, cos.T, sin.T


def _sample_rows(a, batch, t):
    return jnp.pad(a.astype(F32).reshape(batch, t, a.shape[-1]), ((0, 0), (0, SAMPLE_ROWS - t), (0, 0)))


def _sample_new_keys(a_t, batch, t):
    r = a_t.shape[1]
    per_b = jnp.transpose(a_t.reshape(r, batch, t), (1, 0, 2))
    return jnp.pad(per_b, ((0, 0), (0, 0), (0, LANES - t)))


def _cache_t(cache):
    depth, n_pool, page = cache.shape[:3]
    return jnp.swapaxes(cache.reshape(depth, n_pool, page, -1), 2, 3)


def kernel(x_prompt, x_sample, c_prompt, c_sample, cache_k_a, cache_v_a, cache_idx_k, cache_k_b, cache_v_b, page_table, w_ada, b_ada, g_mix, w_in, w_proj_a, w_proj_b, w_out, g_ffn, w_router_group, b_router_group, w_router_expert, b_router_expert, w_exp_gate, w_exp_up, w_exp_down, g_final):
    batch, seq, d = x_prompt.shape
    dec_batch, dec_seq, _ = x_sample.shape
    depth = w_in.shape[0]
    past_len = page_table.shape[1] * cache_k_a.shape[2]
    n_p, n_s = batch * seq, dec_batch * dec_seq
    tm_p = min(256, seq)
    tm_s = min(256, n_s)
    assert seq % tm_p == 0 and n_s % tm_s == 0

    mod_all = _ada_mod(jnp.concatenate([c_prompt, c_sample], axis=0), w_ada, b_ada)
    mod_p = mod_all[:, :batch].reshape(depth, batch, 1, 6 * d)
    mod_s = jnp.repeat(mod_all[:, batch:], dec_seq, axis=1)
    w_row, w_col = _split_w_in(w_in)
    w_pa, w_pb, w_o = w_proj_a.astype(BF16), w_proj_b.astype(BF16), w_out.astype(BF16)
    pad_r = LANES - N_GROUPS - N_EXPERTS
    w_r = jnp.pad(jnp.concatenate([w_router_group, w_router_expert], axis=2), ((0, 0), (0, 0), (0, pad_r)))
    b_r = jnp.pad(jnp.concatenate([b_router_group, b_router_expert], axis=1), ((0, 0), (0, pad_r)))
    b_r = b_r.reshape(depth, 1, LANES)
    cache_idx_t = _cache_t(cache_idx_k)
    caches_t = tuple(_cache_t(c) for c in (cache_k_a, cache_v_a, cache_k_b, cache_v_b))

    trig_p = _rope_tables(jnp.arange(seq, dtype=jnp.int32))
    trig_s = _rope_tables(jnp.tile(past_len + jnp.arange(dec_seq, dtype=jnp.int32), dec_batch))
    nq_p = seq // tm_p
    trig_spec_p = pl.BlockSpec((tm_p, LANES), lambda i: (i % nq_p, 0))
    trig_t_spec_p = pl.BlockSpec((LANES, tm_p), lambda i: (0, i % nq_p))
    trig_spec_s = pl.BlockSpec((tm_s, LANES), lambda i: (i, 0))
    trig_t_spec_s = pl.BlockSpec((LANES, tm_s), lambda i: (0, i))

    xp = x_prompt.reshape(n_p, d)
    xs = x_sample.reshape(n_s, d)
    rows_p, rows_s = [], []
    for l in range(depth):
        mod_spec_p = lambda chunk, l=l: pl.BlockSpec((None, None, 1, d), lambda i: (l, i // nq_p, 0, chunk))
        mod_spec_s = lambda chunk, l=l: pl.BlockSpec((None, tm_s, d), lambda i: (l, i, chunk))
        last = l == depth - 1

        qa, qi, wi, qb, ga, gb, ka_t, va_t, ki_t, kb_t, vb_t = _inproj(
            xp, l, mod_p, mod_spec_p, g_mix, w_row, w_col, trig_p, trig_spec_p, trig_t_spec_p, batch, tm_p, BF16)
        rows_p.append((ka_t, va_t, ki_t, kb_t, vb_t))
        ya = _dsa_prompt(qi, wi, qa, ki_t, ka_t, va_t, min(PROMPT_TQ, seq))
        yb = _moba_prompt(qb, kb_t, vb_t, min(PROMPT_TQ, seq))
        x1, h2, route = _mixout(ya, yb, ga, gb, xp, l, mod_p, mod_spec_p, g_ffn, w_pa, w_pb, w_o, w_r, b_r, tm_p)
        y = _experts(h2, route[:, ROUTE_E1:ROUTE_E2 + 1].astype(jnp.int32), l, w_exp_gate, w_exp_up, w_exp_down)
        xp = _combine(x1, y, route, mod_p, mod_spec_p, g_final, tm_p, last)

        qa, qi, wi, qb, ga, gb, ka_t, va_t, ki_t, kb_t, vb_t = _inproj(
            xs, l, mod_s, mod_spec_s, g_mix, w_row, w_col, trig_s, trig_spec_s, trig_t_spec_s, 1, tm_s, F32)
        rows_s.append((ka_t, va_t, ki_t, kb_t, vb_t))
        rows8 = lambda a: _sample_rows(a, dec_batch, dec_seq)
        new = lambda a_t: _sample_new_keys(a_t, dec_batch, dec_seq)
        score = _score_sample(rows8(qi), rows8(wi), new(ki_t), cache_idx_t, page_table, l)
        s_len = score.shape[-1]
        bias = _select_sample(score.reshape(dec_batch * SAMPLE_ROWS, s_len),
                              min(IDX_TOPK_MAX, (past_len + dec_seq) // 4), dec_seq)
        ya, yb = _attend_sample(rows8(qa), bias.reshape(dec_batch, SAMPLE_ROWS, s_len), new(ka_t), new(va_t),
                                rows8(qb), new(kb_t), new(vb_t), caches_t, page_table, l, dec_seq)
        unpad = lambda a: a[:, :dec_seq].reshape(n_s, a.shape[-1]).astype(BF16)
        x1, h2, route = _mixout(unpad(ya), unpad(yb), ga, gb, xs, l, mod_s, mod_spec_s, g_ffn, w_pa, w_pb, w_o,
                                w_r, b_r, tm_s)
        y = _experts(h2, route[:, ROUTE_E1:ROUTE_E2 + 1].astype(jnp.int32), l, w_exp_gate, w_exp_up, w_exp_down)
        xs = _combine(x1, y, route, mod_s, mod_spec_s, g_final, tm_s, last)

    def stack_p(i, heads):
        a = jnp.stack([r[i] for r in rows_p], axis=0)
        a = jnp.swapaxes(a, 2, 3)
        return a.reshape((depth, batch, seq, heads, HEAD_DIM) if heads else (depth, batch, seq, D_IDX))

    def stack_s(i, heads):
        a = jnp.stack([r[i][0] for r in rows_s], axis=0)
        a = jnp.swapaxes(a, 1, 2)
        return a.reshape((depth, dec_batch, dec_seq, heads, HEAD_DIM) if heads else (depth, dec_batch, dec_seq, D_IDX))

    return (xp.reshape(batch, seq, d), xs.reshape(dec_batch, dec_seq, d),
            stack_p(0, N_KV_A), stack_p(1, N_KV_A), stack_p(3, N_KV_B), stack_p(4, N_KV_B), stack_p(2, 0),
            stack_s(0, N_KV_A), stack_s(1, N_KV_A), stack_s(3, N_KV_B), stack_s(4, N_KV_B), stack_s(2, 0))
```

```python
import functools

import jax
import jax.numpy as jnp
from jax import lax
from jax.experimental import pallas as pl
from jax.experimental.pallas import tpu as pltpu

HEAD_DIM = 64
N_HEADS_A = 8
N_KV_A = 2
N_IDX_HEADS = 4
D_IDX = 64
IDX_TOPK_MAX = 256
N_HEADS_B = 8
N_KV_B = 2
MOBA_BLOCK = 256
MOBA_TOPK_MAX = 3
N_GROUPS = 4
EXPERTS_PER_GROUP = 8
N_EXPERTS = N_GROUPS * EXPERTS_PER_GROUP
TOPK_IN_GROUP = 2
ROPE_THETA = 10000.0
RMS_EPS = 1e-6

LANES = 128
KV_ROWS = N_KV_A * HEAD_DIM
assert KV_ROWS == LANES == N_KV_B * HEAD_DIM and D_IDX == HEAD_DIM and TOPK_IN_GROUP == 2

VMEM_LIMIT = 56 * 1024 * 1024

F32 = jnp.float32
BF16 = jnp.bfloat16
NEG_INF = float("-inf")
INT_MIN = -(2 ** 31)
NEG_INF_KEY = -2139095041
HIGHEST = lax.Precision.HIGHEST
SCORE_SCALE = HEAD_DIM ** -0.5 * 1.4426950408889634


def _cparams(n_grid):
    return pltpu.CompilerParams(dimension_semantics=("arbitrary",) * n_grid, vmem_limit_bytes=VMEM_LIMIT)


def _dot(a, b, precision=None):
    return jnp.dot(a, b, preferred_element_type=F32, precision=precision)


def _dot_nt(a, b, precision=None):
    return lax.dot_general(a, b, (((1,), (1,)), ((), ())), preferred_element_type=F32, precision=precision)


def _sigmoid(x):
    return 1.0 / (1.0 + jnp.exp(-x))


def _ada_kernel(c_ref, w_ref, b_ref, o_ref):
    c = c_ref[...]
    o_ref[...] = _dot((c * _sigmoid(c)).astype(BF16), w_ref[...].astype(BF16)) + b_ref[...]


def _ada_mod(c_all, w_ada, b_ada):
    depth, d, n6 = w_ada.shape
    m = c_all.shape[0]
    tn = 1024
    return pl.pallas_call(
        _ada_kernel,
        grid=(depth, n6 // tn),
        in_specs=[
            pl.BlockSpec((m, d), lambda l, j: (0, 0)),
            pl.BlockSpec((None, d, tn), lambda l, j: (l, 0, j)),
            pl.BlockSpec((None, 1, tn), lambda l, j: (l, 0, j)),
        ],
        out_specs=pl.BlockSpec((None, m, tn), lambda l, j: (l, 0, j)),
        out_shape=jax.ShapeDtypeStruct((depth, m, n6), F32),
        compiler_params=_cparams(2),
        name="ada_mod",
    )(c_all, w_ada, b_ada.reshape(depth, 1, n6))


_QA_W = N_HEADS_A * HEAD_DIM
_QI_W = N_IDX_HEADS * D_IDX
_QB_W = N_HEADS_B * HEAD_DIM
_ROW_QA, _ROW_QI, _ROW_WI, _ROW_QB = 0, _QA_W, _QA_W + _QI_W, _QA_W + _QI_W + LANES
_ROW_GATE = _ROW_QB + _QB_W
_COL_KA, _COL_VA, _COL_KI, _COL_KB, _COL_VB = 0, KV_ROWS, 2 * KV_ROWS, 2 * KV_ROWS + D_IDX, 3 * KV_ROWS + D_IDX
_COL_END = 4 * KV_ROWS + D_IDX


def _rope_lanes(y, cos, sin, lane):
    first_half = (lane % HEAD_DIM) < (HEAD_DIM // 2)
    rot = jnp.where(first_half, -pltpu.roll(y, LANES - HEAD_DIM // 2, 1), pltpu.roll(y, HEAD_DIM // 2, 1))
    return y * cos + rot * sin


def _rope_rows(y, cos, sin):
    half = HEAD_DIM // 2
    parts = []
    for r0 in range(0, y.shape[0], HEAD_DIM):
        parts += [-y[r0 + half:r0 + HEAD_DIM], y[r0:r0 + half]]
    rot = jnp.concatenate(parts, axis=0)
    return y * cos[:y.shape[0]] + rot * sin[:y.shape[0]]


def _inproj_kernel(x_ref, sc_ref, sh_ref, g_ref, wr_ref, wc_ref, cos_ref, sin_ref, cos_t_ref, sin_t_ref,
                   qa_ref, qi_ref, wi_ref, qb_ref, ga_ref, gb_ref, ka_ref, va_ref, ki_ref, kb_ref, vb_ref,
                   *, d_model):
    x = x_ref[...]
    h = x * lax.rsqrt(jnp.mean(x * x, axis=-1, keepdims=True) + RMS_EPS) * g_ref[...]
    h = h * (1.0 + sc_ref[...]) + sh_ref[...]
    hb = h.astype(BF16)
    cos = cos_ref[...]
    sin = sin_ref[...]
    lane = lax.broadcasted_iota(jnp.int32, cos.shape, 1)

    def proj(c0, width):
        return _dot(hb, wr_ref[:, c0:c0 + width])

    def roped(c0, out_ref, scale=None):
        for c in range(out_ref.shape[1] // LANES):
            y = _rope_lanes(proj(c0 + c * LANES, LANES), cos, sin, lane)
            if scale is not None:
                y = y * scale
            out_ref[:, c * LANES:(c + 1) * LANES] = y.astype(out_ref.dtype)

    roped(_ROW_QA, qa_ref, SCORE_SCALE)
    roped(_ROW_QI, qi_ref)
    wik = proj(_ROW_WI, LANES)
    wi_ref[...] = jnp.where(lane >= D_IDX, _rope_lanes(wik, cos, sin, lane), wik)
    roped(_ROW_QB, qb_ref)
    for c in range(d_model // 512):
        ga_ref[:, c * 512:(c + 1) * 512] = _sigmoid(proj(_ROW_GATE + c * 512, 512))
        gb_ref[:, c * 512:(c + 1) * 512] = _sigmoid(proj(_ROW_GATE + d_model + c * 512, 512))

    cos_t = cos_t_ref[...]
    sin_t = sin_t_ref[...]

    def proj_t(r0, r1):
        return _dot_nt(wc_ref[r0:r1, :], hb)

    ka_ref[...] = _rope_rows(proj_t(_COL_KA, _COL_VA), cos_t, sin_t)
    va_ref[...] = proj_t(_COL_VA, _COL_KI)
    ki_ref[...] = _rope_rows(proj_t(_COL_KI, _COL_KB), cos_t, sin_t)
    kb_ref[...] = _rope_rows(proj_t(_COL_KB, _COL_VB), cos_t, sin_t)
    vb_ref[...] = proj_t(_COL_VB, _COL_END)


def _split_w_in(w_in):
    depth, d, d_in = w_in.shape
    widths = (_QA_W, KV_ROWS, KV_ROWS, _QI_W, D_IDX, N_IDX_HEADS, _QB_W, KV_ROWS, KV_ROWS, d, d)
    offs = [0]
    for w in widths:
        offs.append(offs[-1] + w)
    assert offs[-1] == d_in
    w_t = jnp.transpose(w_in, (2, 0, 1))
    qa, ka, va, qi, ki, wi, qb, kb, vb, ga, gb = [w_t[offs[i]:offs[i + 1]] for i in range(11)]
    pad = jnp.zeros((LANES - N_IDX_HEADS - D_IDX, depth, d), w_in.dtype)
    w_row = jnp.transpose(jnp.concatenate([qa, qi, wi, pad, ki, qb, ga, gb], axis=0).astype(BF16), (1, 2, 0))
    w_col = jnp.transpose(jnp.concatenate([ka, va, ki, kb, vb], axis=0).astype(BF16), (1, 0, 2))
    return w_row, w_col


def _inproj(x, layer, mod, mod_spec, g_mix, w_row, w_col, trig, trig_spec, trig_t_spec, n_batch, tm, q_dtype):
    n, d = x.shape
    per_batch = n // n_batch
    nq = per_batch // tm
    cos, sin, cos_t, sin_t = trig
    row = lambda width: pl.BlockSpec((tm, width), lambda i: (i, 0))
    col = lambda rows: pl.BlockSpec((None, rows, tm), lambda i: (i // nq, 0, i % nq))
    row_out = [(_QA_W, q_dtype), (_QI_W, q_dtype), (LANES, F32), (_QB_W, F32), (d, F32), (d, F32)]
    col_out = [KV_ROWS, KV_ROWS, D_IDX, KV_ROWS, KV_ROWS]
    layer_w = lambda a: pl.BlockSpec((None,) + a.shape[1:], lambda i: (layer, 0, 0))
    return pl.pallas_call(
        functools.partial(_inproj_kernel, d_model=d),
        grid=(n // tm,),
        in_specs=[row(d), mod_spec(1), mod_spec(0), pl.BlockSpec((None, 1, d), lambda i: (layer, 0, 0)),
                  layer_w(w_row), layer_w(w_col), trig_spec, trig_spec, trig_t_spec, trig_t_spec],
        out_specs=[row(w) for w, _ in row_out] + [col(r) for r in col_out],
        out_shape=[jax.ShapeDtypeStruct((n, w), dt) for w, dt in row_out]
        + [jax.ShapeDtypeStruct((n_batch, r, per_batch), F32) for r in col_out],
        compiler_params=_cparams(1),
        name="inproj",
    )(x, mod, mod, g_mix.reshape(g_mix.shape[0], 1, d), w_row, w_col, cos, sin, cos_t, sin_t)


def _stack_heads(q, first_head, n_heads, scale=None):
    q = q.astype(F32)
    if scale is not None:
        q = q * scale
    parts = [q[:, (first_head + g) * HEAD_DIM:(first_head + g + 1) * HEAD_DIM] for g in range(n_heads)]
    return jnp.concatenate(parts, axis=0).astype(BF16)


def _sort_key(x):
    bits = lax.bitcast_convert_type(x, jnp.int32)
    return bits ^ ((bits >> 31) & 0x7FFFFFFF)


def _count(mask):
    return jnp.sum(jnp.where(mask, 1.0, 0.0), axis=1, keepdims=True)


def _indexer_scores(qi, wi, ki_t, q_pos):
    t = qi.shape[0]
    logits = _dot(_stack_heads(qi, 0, N_IDX_HEADS), ki_t)
    score = None
    for h in range(N_IDX_HEADS):
        term = jnp.maximum(logits[h * t:(h + 1) * t], 0.0) * wi[:, h:h + 1]
        score = term if score is None else score + term
    k_pos = lax.broadcasted_iota(jnp.int32, score.shape, 1)
    score = jnp.where(score == 0.0, 0.0, score)
    return jnp.where(k_pos <= q_pos, score, NEG_INF)


def _topk_mask(score, n_keep, row_is_real, bias_ref):
    t, s_len = score.shape
    key = _sort_key(score)
    keep = float(n_keep)
    thr = jnp.where(_count(key >= 0) >= keep, 0, INT_MIN).astype(jnp.int32)

    def two_bits(i, thr):
        hi = jnp.left_shift(jnp.int32(1), 30 - 2 * i)
        lo = jnp.left_shift(jnp.int32(1), 29 - 2 * i)
        c_hi, c_lo, c_both = thr + hi, thr + lo, thr + hi + lo
        n_hi, n_lo, n_both = _count(key >= c_hi), _count(key >= c_lo), _count(key >= c_both)
        return jnp.where(n_both >= keep, c_both, jnp.where(n_hi >= keep, c_hi, jnp.where(n_lo >= keep, c_lo, thr)))

    thr = lax.fori_loop(0, 15, two_bits, thr)
    thr = jnp.where(_count(key >= thr + 1) >= keep, thr + 1, thr)
    at_least = key >= thr
    n_above = _count(key > thr)
    bias_ref[...] = jnp.where(at_least & (key > NEG_INF_KEY), 0.0, NEG_INF)

    tied = (_count(at_least) > keep) & (thr > NEG_INF_KEY) & row_is_real

    @pl.when(jnp.max(jnp.where(tied, 1.0, 0.0)) > 0.0)
    def _():
        free = keep - n_above
        r_i = lax.broadcasted_iota(jnp.int32, (MOBA_BLOCK, MOBA_BLOCK), 0)
        c_i = lax.broadcasted_iota(jnp.int32, (MOBA_BLOCK, MOBA_BLOCK), 1)
        tri = jnp.where(r_i <= c_i, 1.0, 0.0).astype(BF16)
        seen = jnp.zeros((t, 1), F32)
        for c0 in range(0, s_len, MOBA_BLOCK):
            w = min(MOBA_BLOCK, s_len - c0)
            key_c = key[:, c0:c0 + w]
            eq_c = jnp.where(key_c == thr, 1.0, 0.0)
            rank = _dot(eq_c.astype(BF16), tri[:w, :w]) + seen
            keep_c = ((key_c > thr) | ((key_c == thr) & (rank <= free))) & (key_c > NEG_INF_KEY)
            bias_ref[:, c0:c0 + w] = jnp.where(keep_c, 0.0, NEG_INF)
            seen = seen + jnp.sum(eq_c, axis=1, keepdims=True)


def _indexer_scores_t(kik, kiw_q, qi, q_pos_t):
    t = qi.shape[0]
    q = qi.astype(F32)
    lane = lax.broadcasted_iota(jnp.int32, (t, LANES), 1)
    parts = []
    for h in range(N_IDX_HEADS):
        pair = q[:, (h // 2) * LANES:(h // 2 + 1) * LANES]
        if h % 2:
            parts.append(jnp.where(lane >= D_IDX, pair, 0.0))
        else:
            parts.append(pltpu.roll(jnp.where(lane < D_IDX, pair, 0.0), D_IDX, 1))
    logits = _dot_nt(kik, jnp.concatenate(parts, axis=0).astype(BF16))
    wi_t = kiw_q.T
    score = None
    for h in range(N_IDX_HEADS):
        term = jnp.maximum(logits[:, h * t:(h + 1) * t], 0.0) * wi_t[h:h + 1, :]
        score = term if score is None else score + term
    k_pos = lax.broadcasted_iota(jnp.int32, score.shape, 0)
    score = jnp.where(score == 0.0, 0.0, score)
    return jnp.where(k_pos <= q_pos_t, score, NEG_INF)


COUNT_SLAB = 64


def _count_t(mask):
    ones = jnp.where(mask, 1.0, 0.0)
    s_len, t = ones.shape
    if s_len % COUNT_SLAB == 0 and s_len > COUNT_SLAB:
        ones = jnp.sum(ones.reshape(s_len // COUNT_SLAB, COUNT_SLAB, t), axis=0)
    return jnp.sum(ones, axis=0, keepdims=True)


def _store_transposed(mask_t, c0, bias_ref):
    for j in range(0, mask_t.shape[0], LANES):
        bias_ref[:, c0 + j:c0 + j + LANES] = mask_t[j:j + LANES, :].T


def _topk_mask_t(score_t, n_keep, bias_ref):
    s_len, t = score_t.shape
    key = _sort_key(score_t)
    keep = float(n_keep)
    thr = jnp.where(_count_t(key >= 0) >= keep, 0, INT_MIN).astype(jnp.int32)

    def bit_step(i, thr):
        cand = thr + jnp.left_shift(jnp.int32(1), 30 - i)
        return jnp.where(_count_t(key >= cand) >= keep, cand, thr)

    thr = lax.fori_loop(0, 31, bit_step, thr)
    at_least = key >= thr
    n_above = _count_t(key > thr)
    _store_transposed(jnp.where(at_least & (key > NEG_INF_KEY), 0.0, NEG_INF), 0, bias_ref)

    tied = (_count_t(at_least) > keep) & (thr > NEG_INF_KEY)

    @pl.when(jnp.max(jnp.where(tied, 1.0, 0.0)) > 0.0)
    def _():
        free = keep - n_above
        r_i = lax.broadcasted_iota(jnp.int32, (MOBA_BLOCK, MOBA_BLOCK), 0)
        c_i = lax.broadcasted_iota(jnp.int32, (MOBA_BLOCK, MOBA_BLOCK), 1)
        tri = jnp.where(c_i <= r_i, 1.0, 0.0).astype(BF16)
        seen = jnp.zeros((1, t), F32)
        for c0 in range(0, s_len, MOBA_BLOCK):
            w = min(MOBA_BLOCK, s_len - c0)
            key_c = key[c0:c0 + w, :]
            eq_c = jnp.where(key_c == thr, 1.0, 0.0)
            rank = _dot(tri[:w, :w], eq_c.astype(BF16)) + seen
            keep_c = ((key_c > thr) | ((key_c == thr) & (rank <= free))) & (key_c > NEG_INF_KEY)
            _store_transposed(jnp.where(keep_c, 0.0, NEG_INF), c0, bias_ref)
            seen = seen + jnp.sum(eq_c, axis=0, keepdims=True)


def _softmax_pv(s, v_t):
    m = jnp.max(s, axis=1, keepdims=True)
    p = jnp.exp2(s - m)
    l = jnp.sum(p, axis=1, keepdims=True)
    return _dot_nt(p.astype(BF16), v_t) / l


def _attend_group(q_g, k_t, v_t, bias):
    s = _dot(q_g, k_t)
    g = s.shape[0] // bias.shape[0]
    s = (s.reshape(g, bias.shape[0], s.shape[1]) + bias[None]).reshape(s.shape)
    return _softmax_pv(s, v_t)


def _store_heads(o_g, t, first_head, n_heads, out_ref):
    for g in range(0, n_heads, 2):
        pair = jnp.concatenate([o_g[g * t:(g + 1) * t], o_g[(g + 1) * t:(g + 2) * t]], axis=1)
        c0 = (first_head + g) * HEAD_DIM
        out_ref[:, c0:c0 + 2 * HEAD_DIM] = pair.astype(out_ref.dtype)


def _dsa_attention(qa, ka_t, va_t, bias, out_ref):
    t = qa.shape[0]
    hpk = N_HEADS_A // N_KV_A
    for kv in range(N_KV_A):
        k_t = ka_t[kv * HEAD_DIM:(kv + 1) * HEAD_DIM].astype(BF16)
        v_t = va_t[kv * HEAD_DIM:(kv + 1) * HEAD_DIM].astype(BF16)
        o_g = _attend_group(_stack_heads(qa, kv * hpk, hpk), k_t, v_t, bias)
        _store_heads(o_g, t, kv * hpk, hpk, out_ref)


GATE_ROWS = 8


def _block_means(kb_t, n_full):
    lane = lax.broadcasted_iota(jnp.int32, (KV_ROWS, LANES), 1)
    km = jnp.zeros((KV_ROWS, LANES), F32)
    for n in range(n_full):
        col = jnp.sum(kb_t[:, n * MOBA_BLOCK:(n + 1) * MOBA_BLOCK], axis=1, keepdims=True) * (1.0 / MOBA_BLOCK)
        km = jnp.where(lane == n, col, km)
    return km.T


def _moba_queries(qb, kv):
    hpk = N_HEADS_B // N_KV_B
    return jnp.concatenate([qb[:, (kv * hpk + g) * HEAD_DIM:(kv * hpk + g + 1) * HEAD_DIM] for g in range(hpk)],
                           axis=0)


def _moba_pick(q_f, k_mean_kv, own, n_sel, n_full):
    assert n_full <= GATE_ROWS
    rows = q_f.shape[0]
    rows_pad = -(-rows // LANES) * LANES
    if rows_pad != rows:
        q_f = jnp.concatenate([q_f, jnp.zeros((rows_pad - rows, HEAD_DIM), F32)], axis=0)
    blk = lax.broadcasted_iota(jnp.int32, (GATE_ROWS, rows_pad), 0)
    blk_f = blk.astype(F32)
    gate = _dot_nt(k_mean_kv, q_f, precision=HIGHEST)
    gate = jnp.where(blk < jnp.minimum(own, n_full), gate, NEG_INF)
    picked_t = jnp.zeros((GATE_ROWS, rows_pad), F32)
    for _ in range(n_sel):
        best = jnp.max(gate, axis=0, keepdims=True)
        first = jnp.min(jnp.where(gate == best, blk_f, float(GATE_ROWS)), axis=0, keepdims=True)
        hit = (blk_f == first) & (best > NEG_INF)
        picked_t = jnp.where(hit, 1.0, picked_t)
        gate = jnp.where(blk_f == first, NEG_INF, gate)
    picked_t = jnp.concatenate([picked_t, jnp.zeros((LANES - GATE_ROWS, rows_pad), F32)], axis=0)
    return picked_t.T[:rows]


def _moba_scores(q_g, k_t, picked, q_pos_g, own, own_min, n_full, s_ref):
    rows, s_len = q_g.shape[0], k_t.shape[1]
    for c0 in range(0, s_len, MOBA_BLOCK):
        c = c0 // MOBA_BLOCK
        w = min(MOBA_BLOCK, s_len - c0)
        if c < own_min:
            allowed = picked[:, c:c + 1] > 0.0
        else:
            k_pos = c0 + lax.broadcasted_iota(jnp.int32, (rows, w), 1)
            allowed = (k_pos <= q_pos_g) & (own == c)
            if c < n_full:
                allowed = allowed | (picked[:, c:c + 1] > 0.0)
        s_ref[:, c0:c0 + w] = jnp.where(allowed, _dot(q_g, k_t[:, c0:c0 + w]), NEG_INF)


def _moba_attention(qb, kb_t, vb_t, k_mean, q_pos, own, own_min, n_sel, n_full, s_ref, out_ref):
    t = qb.shape[0]
    s_len = kb_t.shape[1]
    hpk = N_HEADS_B // N_KV_B
    q_pos_g = jnp.concatenate([q_pos] * hpk, axis=0)
    for kv in range(N_KV_B):
        q_f = _moba_queries(qb, kv)
        picked = _moba_pick(q_f, k_mean[:GATE_ROWS, kv * HEAD_DIM:(kv + 1) * HEAD_DIM], own, n_sel, n_full)
        k_t = kb_t[kv * HEAD_DIM:(kv + 1) * HEAD_DIM].astype(BF16)
        v_t = vb_t[kv * HEAD_DIM:(kv + 1) * HEAD_DIM].astype(BF16)
        _moba_scores((q_f * SCORE_SCALE).astype(BF16), k_t, picked, q_pos_g, own, own_min, n_full,
                     s_ref.at[:, :s_len])
        _store_heads(_softmax_pv(s_ref[:, :s_len], v_t), t, kv * hpk, hpk, out_ref)


PROMPT_TQ = 128
KEY_SEGMENT = MOBA_BLOCK


def _per_key_segment(seq, tq, body):
    n_seg = max(1, seq // KEY_SEGMENT)
    seg_len = seq // n_seg
    assert seg_len % tq == 0 and seg_len % MOBA_BLOCK == 0
    seg = (pl.program_id(1) * tq) // seg_len
    for k in range(n_seg):
        pl.when(seg == k)(functools.partial(body, (k + 1) * seg_len, seg_len))


def _dsa_prompt_kernel(qi_ref, wik_q_ref, qa_ref, wik_k_ref, ka_ref, va_ref, o_ref, bias_ref, kik_ref,
                       *, tq, seq, n_keep):
    q_pos_t = pl.program_id(1) * tq + lax.broadcasted_iota(jnp.int32, (1, tq), 1)

    @pl.when(pl.program_id(1) == 0)
    def _():
        kik_ref[...] = wik_k_ref[...].astype(BF16)

    def body(s_len, seg_len):
        score_t = _indexer_scores_t(kik_ref[:s_len, :], wik_q_ref[...], qi_ref[...], q_pos_t)
        _topk_mask_t(score_t, n_keep, bias_ref.at[:, :s_len])
        _dsa_attention(qa_ref[...], ka_ref[:, :s_len], va_ref[:, :s_len], bias_ref[:, :s_len], o_ref)

    _per_key_segment(seq, tq, body)


def _dsa_prompt(qi, wik, qa, ka_t, va_t, tq):
    batch, _, seq = ka_t.shape
    assert tq == LANES
    nq = seq // tq
    qrow = lambda w: pl.BlockSpec((tq, w), lambda b, i: (b * nq + i, 0))
    keys_t = pl.BlockSpec((None, KV_ROWS, seq), lambda b, i: (b, 0, 0))
    return pl.pallas_call(
        functools.partial(_dsa_prompt_kernel, tq=tq, seq=seq, n_keep=min(IDX_TOPK_MAX, seq // 4)),
        grid=(batch, nq),
        in_specs=[qrow(qi.shape[1]), qrow(LANES), qrow(qa.shape[1]),
                  pl.BlockSpec((seq, LANES), lambda b, i: (b, 0)), keys_t, keys_t],
        out_specs=qrow(qa.shape[1]),
        out_shape=jax.ShapeDtypeStruct(qa.shape, BF16),
        scratch_shapes=[pltpu.VMEM((tq, seq), F32), pltpu.VMEM((seq, LANES), BF16)],
        compiler_params=_cparams(2),
        name="dsa_prompt",
    )(qi, wik, qa, wik, ka_t, va_t)


def _moba_prompt_kernel(qb_ref, kb_ref, vb_ref, o_ref, bias_ref, km_ref, *, tq, seq, n_sel):
    base = pl.program_id(1) * tq
    q_pos = base + lax.broadcasted_iota(jnp.int32, (tq, 1), 0)
    n_full = seq // MOBA_BLOCK

    @pl.when(pl.program_id(1) == 0)
    def _():
        km_ref[...] = _block_means(kb_ref[...], n_full)

    def body(s_len, seg_len):
        _moba_attention(qb_ref[...], kb_ref[:, :s_len], vb_ref[:, :s_len], km_ref[...], q_pos, base // MOBA_BLOCK,
                        (s_len - seg_len) // MOBA_BLOCK, n_sel, n_full, bias_ref, o_ref)

    _per_key_segment(seq, tq, body)


def _moba_prompt(qb, kb_t, vb_t, tq):
    batch, _, seq = kb_t.shape
    assert MOBA_BLOCK % tq == 0 and seq % MOBA_BLOCK == 0
    nq = seq // tq
    qrow = pl.BlockSpec((tq, qb.shape[1]), lambda b, i: (b * nq + i, 0))
    keys = pl.BlockSpec((None, KV_ROWS, seq), lambda b, i: (b, 0, 0))
    return pl.pallas_call(
        functools.partial(_moba_prompt_kernel, tq=tq, seq=seq, n_sel=min(MOBA_TOPK_MAX, seq // MOBA_BLOCK)),
        grid=(batch, nq),
        in_specs=[qrow, keys, keys],
        out_specs=qrow,
        out_shape=jax.ShapeDtypeStruct(qb.shape, BF16),
        scratch_shapes=[pltpu.VMEM((N_HEADS_B // N_KV_B * tq, seq), F32), pltpu.VMEM((KV_ROWS, LANES), F32)],
        compiler_params=_cparams(2),
        name="moba_prompt",
    )(qb, kb_t, vb_t)


SAMPLE_ROWS = 8


def _assemble_keys(page_refs, new_ref, dst_ref, page_size):
    for j, page in enumerate(page_refs):
        dst_ref[:, j * page_size:(j + 1) * page_size] = page[...]
    dst_ref[:, len(page_refs) * page_size:] = new_ref[...]


def _page_specs(layer, n_pages, rows, page_size):
    return [pl.BlockSpec((None, None, rows, page_size), lambda b, pt, j=j: (layer, pt[b, j], 0, 0))
            for j in range(n_pages)]


def _score_sample_kernel(pt_ref, qi_ref, wi_ref, kin_ref, *rest, n_pages, page_size):
    ki_pages = rest[:n_pages]
    o_ref, ki_s = rest[n_pages:]
    _assemble_keys(ki_pages, kin_ref, ki_s, page_size)
    q_pos = n_pages * page_size + lax.broadcasted_iota(jnp.int32, (SAMPLE_ROWS, 1), 0)
    o_ref[...] = _indexer_scores(qi_ref[...], wi_ref[...], ki_s[...].astype(BF16), q_pos)


def _score_sample(qi, wi, ki_new, cache_idx_t, page_table, layer):
    b, n_pages = page_table.shape
    page_size = cache_idx_t.shape[3]
    s_len = n_pages * page_size + LANES
    per_b = lambda r, w: pl.BlockSpec((None, r, w), lambda i, pt: (i, 0, 0))
    grid_spec = pltpu.PrefetchScalarGridSpec(
        num_scalar_prefetch=1,
        grid=(b,),
        in_specs=[per_b(SAMPLE_ROWS, qi.shape[2]), per_b(SAMPLE_ROWS, LANES), per_b(D_IDX, LANES)]
        + _page_specs(layer, n_pages, D_IDX, page_size),
        out_specs=per_b(SAMPLE_ROWS, s_len),
        scratch_shapes=[pltpu.VMEM((D_IDX, s_len), F32)],
    )
    return pl.pallas_call(
        functools.partial(_score_sample_kernel, n_pages=n_pages, page_size=page_size),
        grid_spec=grid_spec,
        out_shape=jax.ShapeDtypeStruct((b, SAMPLE_ROWS, s_len), F32),
        compiler_params=_cparams(1),
        name="score_sample",
    )(page_table, qi, wi, ki_new, *([cache_idx_t] * n_pages))


def _select_kernel(score_ref, bias_ref, *, n_keep, dec_seq):
    rows = score_ref.shape[0]
    row = lax.broadcasted_iota(jnp.int32, (rows, 1), 0)
    _topk_mask(score_ref[...], n_keep, (row % SAMPLE_ROWS) < dec_seq, bias_ref)


def _select_sample(score, n_keep, dec_seq):
    n, s_len = score.shape
    tr = min(256, n)
    assert n % tr == 0 and tr % SAMPLE_ROWS == 0
    spec = pl.BlockSpec((tr, s_len), lambda i: (i, 0))
    return pl.pallas_call(
        functools.partial(_select_kernel, n_keep=n_keep, dec_seq=dec_seq),
        grid=(n // tr,),
        in_specs=[spec],
        out_specs=spec,
        out_shape=jax.ShapeDtypeStruct((n, s_len), F32),
        compiler_params=_cparams(1),
        name="select_sample",
    )(score)


SAMPLE_GROUP = 1


def _attend_sample_kernel(pt_ref, qa_ref, bias_ref, kan_ref, van_ref, qb_ref, kbn_ref, vbn_ref, *rest,
                          n_pages, page_size, n_sel):
    n_page_refs = 4 * SAMPLE_GROUP * n_pages
    ya_ref, yb_ref = rest[n_page_refs:n_page_refs + 2]
    scratch = rest[n_page_refs + 2:]
    past = n_pages * page_size
    n_full = past // MOBA_BLOCK
    hpk = N_HEADS_A // N_KV_A
    rows = hpk * SAMPLE_ROWS
    q_pos_g = past + lax.broadcasted_iota(jnp.int32, (rows, 1), 0) % SAMPLE_ROWS
    for g in range(SAMPLE_GROUP):
        ka_s, va_s, kb_s, vb_s, s_ref = scratch[5 * g:5 * g + 5]
        for k, (new_ref, dst) in enumerate(zip((kan_ref, van_ref, kbn_ref, vbn_ref), (ka_s, va_s, kb_s, vb_s))):
            first = (k * SAMPLE_GROUP + g) * n_pages
            _assemble_keys(rest[first:first + n_pages], new_ref.at[g], dst, page_size)
        qa, qb = qa_ref[g], qb_ref[g]
        bias_g = jnp.concatenate([bias_ref[g]] * hpk, axis=0)
        kb_t = kb_s[...]
        k_mean = _block_means(kb_t, n_full)
        for kv in range(N_KV_A):
            k_t = ka_s[kv * HEAD_DIM:(kv + 1) * HEAD_DIM, :].astype(BF16)
            s_ref[kv * rows:(kv + 1) * rows, :] = _dot(_stack_heads(qa, kv * hpk, hpk), k_t) + bias_g
            q_f = _moba_queries(qb, kv)
            picked = _moba_pick(q_f, k_mean[:GATE_ROWS, kv * HEAD_DIM:(kv + 1) * HEAD_DIM], n_full, n_sel, n_full)
            _moba_scores((q_f * SCORE_SCALE).astype(BF16), kb_t[kv * HEAD_DIM:(kv + 1) * HEAD_DIM].astype(BF16),
                         picked, q_pos_g, n_full, n_full, n_full,
                         s_ref.at[(N_KV_A + kv) * rows:(N_KV_A + kv + 1) * rows, :])
        s = s_ref[...]
        p = jnp.exp2(s - jnp.max(s, axis=1, keepdims=True))
        l = jnp.sum(p, axis=1, keepdims=True)
        p = p.astype(BF16)
        groups = ((va_s, ya_ref, 0), (va_s, ya_ref, 1), (vb_s, yb_ref, 0), (vb_s, yb_ref, 1))
        for idx, (v_s, out_ref, kv) in enumerate(groups):
            r0 = idx * rows
            v_t = v_s[kv * HEAD_DIM:(kv + 1) * HEAD_DIM, :].astype(BF16)
            o_g = _dot_nt(p[r0:r0 + rows], v_t) / l[r0:r0 + rows]
            _store_heads(o_g, SAMPLE_ROWS, kv * hpk, hpk, out_ref.at[g])


def _attend_sample(qa, bias, ka_new, va_new, qb, kb_new, vb_new, caches_t, page_table, layer, dec_seq):
    b, n_pages = page_table.shape
    page_size = caches_t[0].shape[3]
    past = n_pages * page_size
    assert past % MOBA_BLOCK == 0 and dec_seq <= SAMPLE_ROWS and b % SAMPLE_GROUP == 0
    s_len = past + LANES
    n_blocks = -(-(past + dec_seq) // MOBA_BLOCK)
    per_step = lambda r, w: pl.BlockSpec((SAMPLE_GROUP, r, w), lambda i, pt: (i, 0, 0))
    new = per_step(KV_ROWS, LANES)
    q = per_step(SAMPLE_ROWS, qa.shape[2])
    page_specs = [pl.BlockSpec((None, None, KV_ROWS, page_size),
                               lambda i, pt, g=g, j=j: (layer, pt[i * SAMPLE_GROUP + g, j], 0, 0))
                  for _ in range(4) for g in range(SAMPLE_GROUP) for j in range(n_pages)]
    grid_spec = pltpu.PrefetchScalarGridSpec(
        num_scalar_prefetch=1,
        grid=(b // SAMPLE_GROUP,),
        in_specs=[q, per_step(SAMPLE_ROWS, s_len), new, new, q, new, new] + page_specs,
        out_specs=[q, q],
        scratch_shapes=([pltpu.VMEM((KV_ROWS, s_len), F32)] * 4
                        + [pltpu.VMEM(((N_HEADS_A + N_HEADS_B) * SAMPLE_ROWS, s_len), F32)]) * SAMPLE_GROUP,
    )
    return pl.pallas_call(
        functools.partial(_attend_sample_kernel, n_pages=n_pages, page_size=page_size,
                          n_sel=min(MOBA_TOPK_MAX, n_blocks)),
        grid_spec=grid_spec,
        out_shape=[jax.ShapeDtypeStruct(qa.shape, F32)] * 2,
        compiler_params=_cparams(1),
        name="attend_sample",
    )(page_table, qa, bias, ka_new, va_new, qb, kb_new, vb_new,
      *[c for c in caches_t for _ in range(SAMPLE_GROUP * n_pages)])


ROUTE_E1, ROUTE_E2, ROUTE_C1, ROUTE_C2 = 0, 1, 2, 3


def _mixout_kernel(ya_ref, yb_ref, ga_ref, gb_ref, x_ref, gt1_ref, sc2_ref, sh2_ref, g_ref,
                   wpa_ref, wpb_ref, wout_ref, wr_ref, br_ref, x1_ref, h2_ref, route_ref):
    merged = ga_ref[...] * _dot(ya_ref[...], wpa_ref[...]) + gb_ref[...] * _dot(yb_ref[...], wpb_ref[...])
    mix = _dot(merged.astype(BF16), wout_ref[...])
    x1 = x_ref[...] + gt1_ref[...] * mix
    x1_ref[...] = x1
    h2 = x1 * lax.rsqrt(jnp.mean(x1 * x1, axis=-1, keepdims=True) + RMS_EPS) * g_ref[...]
    h2 = h2 * (1.0 + sc2_ref[...]) + sh2_ref[...]
    h2_ref[...] = h2

    logits = _dot(h2, wr_ref[...], precision=HIGHEST) + br_ref[...]
    lane = lax.broadcasted_iota(jnp.int32, logits.shape, 1)
    lane_f = lane.astype(F32)
    is_grp = lane < N_GROUPS
    grp = jnp.where(is_grp, logits, NEG_INF)
    g_max = jnp.max(grp, axis=1, keepdims=True)
    g_star = jnp.min(jnp.where(grp == g_max, lane_f, float(N_GROUPS)), axis=1, keepdims=True)
    p_g = 1.0 / jnp.sum(jnp.where(is_grp, jnp.exp(grp - g_max), 0.0), axis=1, keepdims=True)
    e_id = lane_f - float(N_GROUPS)
    in_grp = (e_id >= g_star * EXPERTS_PER_GROUP) & (e_id < (g_star + 1.0) * EXPERTS_PER_GROUP)
    cand = jnp.where(in_grp, logits, NEG_INF)
    l1 = jnp.max(cand, axis=1, keepdims=True)
    e1 = jnp.min(jnp.where(cand == l1, e_id, float(N_EXPERTS)), axis=1, keepdims=True)
    cand = jnp.where(e_id == e1, NEG_INF, cand)
    l2 = jnp.max(cand, axis=1, keepdims=True)
    e2 = jnp.min(jnp.where(cand == l2, e_id, float(N_EXPERTS)), axis=1, keepdims=True)
    t = jnp.exp(l2 - l1)
    c1 = p_g / (1.0 + t)
    c2 = p_g * t / (1.0 + t)
    route = jnp.where(lane == ROUTE_E1, e1, 0.0)
    route = jnp.where(lane == ROUTE_E2, e2, route)
    route = jnp.where(lane == ROUTE_C1, c1, route)
    route_ref[...] = jnp.where(lane == ROUTE_C2, c2, route)


def _mixout(ya, yb, ga, gb, x, layer, mod, mod_spec, g_ffn, w_pa, w_pb, w_out, w_r, b_r, tm):
    n, d = x.shape
    row = lambda width: pl.BlockSpec((tm, width), lambda i: (i, 0))
    layer_w = lambda a: pl.BlockSpec((None,) + a.shape[1:], lambda i: (layer, 0, 0))
    return pl.pallas_call(
        _mixout_kernel,
        grid=(n // tm,),
        in_specs=[row(ya.shape[1]), row(yb.shape[1]), row(d), row(d), row(d),
                  mod_spec(2), mod_spec(4), mod_spec(3), pl.BlockSpec((None, 1, d), lambda i: (layer, 0, 0)),
                  layer_w(w_pa), layer_w(w_pb), layer_w(w_out), layer_w(w_r), layer_w(b_r)],
        out_specs=[row(d), row(d), row(LANES)],
        out_shape=[jax.ShapeDtypeStruct((n, d), F32), jax.ShapeDtypeStruct((n, d), F32),
                   jax.ShapeDtypeStruct((n, LANES), F32)],
        compiler_params=_cparams(1),
        name="mixout",
    )(ya, yb, ga, gb, x, mod, mod, mod, g_ffn.reshape(g_ffn.shape[0], 1, d), w_pa, w_pb, w_out, w_r, b_r)


MOE_ROWS = 256
ROW_CHUNK = 32
DMA_UNROLL = 8
N_BUF = 2


def _experts_kernel(blk_e_ref, n_valid_ref, src_ref, order_ref, h_hbm, wg_ref, wu_ref, wd_ref, y_hbm,
                    x_buf, y_buf, gather_sem, scatter_sem, *, n_tokens, n_blocks):
    i = pl.program_id(0)
    buf = i % N_BUF
    n_slots = TOPK_IN_GROUP * n_tokens

    def n_chunks(b):
        return jnp.right_shift(n_valid_ref[b] + (ROW_CHUNK - 1), ROW_CHUNK.bit_length() - 1)

    def slot_of(b, r):
        return order_ref[jnp.minimum(src_ref[b] + r, n_slots - 1)]

    def for_rows(b, fn):
        def chunk(c, carry):
            def row(u, carry):
                fn(c * ROW_CHUNK + u)
                return carry
            return lax.fori_loop(0, ROW_CHUNK, row, carry, unroll=DMA_UNROLL)
        lax.fori_loop(0, n_chunks(b), chunk, 0)

    def for_chunks(b, fn):
        def chunk(c, carry):
            fn(pl.multiple_of(c * ROW_CHUNK, ROW_CHUNK))
            return carry
        lax.fori_loop(0, n_chunks(b), chunk, 0)

    def start_gather(b, bf):
        def row(r):
            slot = slot_of(b, r)
            token = jnp.where(slot >= n_tokens, slot - n_tokens, slot)
            pltpu.make_async_copy(h_hbm.at[pl.ds(token, 1)], x_buf.at[bf, pl.ds(r, 1)], gather_sem.at[bf]).start()
        for_rows(b, row)

    def wait_gather(b, bf):
        for_chunks(b, lambda r0: pltpu.make_async_copy(
            y_hbm.at[pl.ds(0, ROW_CHUNK)], x_buf.at[bf, pl.ds(r0, ROW_CHUNK)], gather_sem.at[bf]).wait())

    def start_scatter(b, bf):
        n_valid = n_valid_ref[b]

        def row(r):
            dst = jnp.where(r < n_valid, slot_of(b, r), n_slots + bf * MOE_ROWS + r)
            pltpu.make_async_copy(y_buf.at[bf, pl.ds(r, 1)], y_hbm.at[pl.ds(dst, 1)], scatter_sem.at[bf]).start()
        for_rows(b, row)

    def wait_scatter(b, bf):
        for_chunks(b, lambda r0: pltpu.make_async_copy(
            y_buf.at[bf, pl.ds(r0, ROW_CHUNK)], y_hbm.at[pl.ds(0, ROW_CHUNK)], scatter_sem.at[bf]).wait())

    @pl.when(i == 0)
    def _():
        x_buf[...] = jnp.zeros_like(x_buf)
        y_buf[...] = jnp.zeros_like(y_buf)
        for bf in range(N_BUF):
            init = pltpu.make_async_copy(y_buf.at[bf], y_hbm.at[pl.ds(n_slots + bf * MOE_ROWS, MOE_ROWS)],
                                         scatter_sem.at[bf])
            init.start()
            init.wait()
        start_gather(0, 0)

    @pl.when(i + 1 < n_blocks)
    def _():
        start_gather(i + 1, 1 - buf)

    wait_gather(i, buf)

    @pl.when(n_valid_ref[i] > 0)
    def _():
        xb = x_buf[buf].astype(BF16)
        gate = _dot(xb, wg_ref[...].astype(BF16))
        hidden = gate * _sigmoid(gate) * _dot(xb, wu_ref[...].astype(BF16))
        y_buf[buf] = _dot(hidden.astype(BF16), wd_ref[...].astype(BF16))

    start_scatter(i, buf)

    @pl.when(i > 0)
    def _():
        wait_scatter(i - 1, 1 - buf)

    @pl.when(i == n_blocks - 1)
    def _():
        wait_scatter(i, buf)


def _experts(h2, expert_ids, layer, w_gate, w_up, w_down):
    n, d = h2.shape
    m = TOPK_IN_GROUP * n
    n_blocks = -(-m // MOE_ROWS) + N_EXPERTS
    flat_e = expert_ids.T.reshape(m)
    order = jnp.argsort(flat_e).astype(jnp.int32)
    experts = jnp.arange(N_EXPERTS, dtype=jnp.int32)
    counts = jnp.sum((flat_e[:, None] == experts[None, :]).astype(jnp.int32), axis=0)
    padded = (counts + MOE_ROWS - 1) // MOE_ROWS * MOE_ROWS
    pad_end = jnp.cumsum(padded)
    pad_start = pad_end - padded
    start = jnp.cumsum(counts) - counts
    block_start = jnp.arange(n_blocks, dtype=jnp.int32) * MOE_ROWS
    blk_e = jnp.minimum(jnp.sum((pad_end[None, :] <= block_start[:, None]).astype(jnp.int32), axis=1), N_EXPERTS - 1)
    offset = block_start - pad_start[blk_e]
    n_valid = jnp.clip(counts[blk_e] - offset, 0, MOE_ROWS).astype(jnp.int32)
    src = jnp.clip(start[blk_e] + offset, 0, m - 1).astype(jnp.int32)
    blk_e = jnp.where(n_valid > 0, blk_e, jnp.max(jnp.where(n_valid > 0, blk_e, 0))).astype(jnp.int32)

    w_spec = lambda a: pl.BlockSpec((None, None) + a.shape[2:], lambda i, be, nv, sr, od: (layer, be[i], 0, 0))
    grid_spec = pltpu.PrefetchScalarGridSpec(
        num_scalar_prefetch=4,
        grid=(n_blocks,),
        in_specs=[pl.BlockSpec(memory_space=pl.ANY), w_spec(w_gate), w_spec(w_up), w_spec(w_down)],
        out_specs=pl.BlockSpec(memory_space=pl.ANY),
        scratch_shapes=[pltpu.VMEM((N_BUF, MOE_ROWS, d), F32), pltpu.VMEM((N_BUF, MOE_ROWS, d), F32),
                        pltpu.SemaphoreType.DMA((N_BUF,)), pltpu.SemaphoreType.DMA((N_BUF,))],
    )
    return pl.pallas_call(
        functools.partial(_experts_kernel, n_tokens=n, n_blocks=n_blocks),
        grid_spec=grid_spec,
        out_shape=jax.ShapeDtypeStruct((m + N_BUF * MOE_ROWS, d), F32),
        compiler_params=_cparams(1),
        name="experts",
    )(blk_e, n_valid, src, order, h2, w_gate, w_up, w_down)


def _combine_kernel(x1_ref, y0_ref, y1_ref, route_ref, gt2_ref, g_ref, o_ref, *, final_norm):
    route = route_ref[...]
    moe = route[:, ROUTE_C1:ROUTE_C1 + 1] * y0_ref[...] + route[:, ROUTE_C2:ROUTE_C2 + 1] * y1_ref[...]
    x2 = x1_ref[...] + gt2_ref[...] * moe
    if final_norm:
        x2 = x2 * lax.rsqrt(jnp.mean(x2 * x2, axis=-1, keepdims=True) + RMS_EPS) * g_ref[...]
    o_ref[...] = x2


def _combine(x1, y, route, mod, mod_spec, g_final, tm, final_norm):
    n, d = x1.shape
    nt = n // tm
    row = lambda width: pl.BlockSpec((tm, width), lambda i: (i, 0))
    return pl.pallas_call(
        functools.partial(_combine_kernel, final_norm=final_norm),
        grid=(nt,),
        in_specs=[row(d), row(d), pl.BlockSpec((tm, d), lambda i: (nt + i, 0)), row(LANES),
                  mod_spec(5), pl.BlockSpec((1, d), lambda i: (0, 0))],
        out_specs=row(d),
        out_shape=jax.ShapeDtypeStruct((n, d), F32),
        compiler_params=_cparams(1),
        name="combine",
    )(x1, y, y, route, mod, g_final.reshape(1, d))


def _rope_tables(pos):
    half = HEAD_DIM // 2
    inv = jnp.power(ROPE_THETA, -jnp.arange(half, dtype=F32) / half)
    ang = pos.astype(F32)[:, None] * inv[None, :]
    cos = jnp.tile(jnp.cos(ang), (1, LANES // half))
    sin = jnp.tile(jnp.sin(ang), (1, LANES // half))
    return cos, sin, cos.T, sin.T


def _sample_rows(a, batch, t):
    return jnp.pad(a.astype(F32).reshape(batch, t, a.shape[-1]), ((0, 0), (0, SAMPLE_ROWS - t), (0, 0)))


def _sample_new_keys(a_t, batch, t):
    r = a_t.shape[1]
    per_b = jnp.transpose(a_t.reshape(r, batch, t), (1, 0, 2))
    return jnp.pad(per_b, ((0, 0), (0, 0), (0, LANES - t)))


def _cache_t(cache):
    depth, n_pool, page = cache.shape[:3]
    return jnp.swapaxes(cache.reshape(depth, n_pool, page, -1), 2, 3)


def kernel(x_prompt, x_sample, c_prompt, c_sample, cache_k_a, cache_v_a, cache_idx_k, cache_k_b, cache_v_b, page_table, w_ada, b_ada, g_mix, w_in, w_proj_a, w_proj_b, w_out, g_ffn, w_router_group, b_router_group, w_router_expert, b_router_expert, w_exp_gate, w_exp_up, w_exp_down, g_final):
    batch, seq, d = x_prompt.shape
    dec_batch, dec_seq, _ = x_sample.shape
    depth = w_in.shape[0]
    past_len = page_table.shape[1] * cache_k_a.shape[2]
    n_p, n_s = batch * seq, dec_batch * dec_seq
    tm_p = min(256, seq)
    tm_s = min(256, n_s)
    assert seq % tm_p == 0 and n_s % tm_s == 0

    mod_all = _ada_mod(jnp.concatenate([c_prompt, c_sample], axis=0), w_ada, b_ada)
    mod_p = mod_all[:, :batch].reshape(depth, batch, 1, 6 * d)
    mod_s = jnp.repeat(mod_all[:, batch:], dec_seq, axis=1)
    w_row, w_col = _split_w_in(w_in)
    w_pa, w_pb, w_o = w_proj_a.astype(BF16), w_proj_b.astype(BF16), w_out.astype(BF16)
    pad_r = LANES - N_GROUPS - N_EXPERTS
    w_r = jnp.pad(jnp.concatenate([w_router_group, w_router_expert], axis=2), ((0, 0), (0, 0), (0, pad_r)))
    b_r = jnp.pad(jnp.concatenate([b_router_group, b_router_expert], axis=1), ((0, 0), (0, pad_r)))
    b_r = b_r.reshape(depth, 1, LANES)
    cache_idx_t = _cache_t(cache_idx_k)
    caches_t = tuple(_cache_t(c) for c in (cache_k_a, cache_v_a, cache_k_b, cache_v_b))

    trig_p = _rope_tables(jnp.arange(seq, dtype=jnp.int32))
    trig_s = _rope_tables(jnp.tile(past_len + jnp.arange(dec_seq, dtype=jnp.int32), dec_batch))
    nq_p = seq // tm_p
    trig_spec_p = pl.BlockSpec((tm_p, LANES), lambda i: (i % nq_p, 0))
    trig_t_spec_p = pl.BlockSpec((LANES, tm_p), lambda i: (0, i % nq_p))
    trig_spec_s = pl.BlockSpec((tm_s, LANES), lambda i: (i, 0))
    trig_t_spec_s = pl.BlockSpec((LANES, tm_s), lambda i: (0, i))

    xp = x_prompt.reshape(n_p, d)
    xs = x_sample.reshape(n_s, d)
    rows_p, rows_s = [], []
    for l in range(depth):
        mod_spec_p = lambda chunk, l=l: pl.BlockSpec((None, None, 1, d), lambda i: (l, i // nq_p, 0, chunk))
        mod_spec_s = lambda chunk, l=l: pl.BlockSpec((None, tm_s, d), lambda i: (l, i, chunk))
        last = l == depth - 1

        qa, qi, wi, qb, ga, gb, ka_t, va_t, ki_t, kb_t, vb_t = _inproj(
            xp, l, mod_p, mod_spec_p, g_mix, w_row, w_col, trig_p, trig_spec_p, trig_t_spec_p, batch, tm_p, BF16)
        rows_p.append((ka_t, va_t, ki_t, kb_t, vb_t))
        ya = _dsa_prompt(qi, wi, qa, ka_t, va_t, min(PROMPT_TQ, seq))
        yb = _moba_prompt(qb, kb_t, vb_t, min(PROMPT_TQ, seq))
        x1, h2, route = _mixout(ya, yb, ga, gb, xp, l, mod_p, mod_spec_p, g_ffn, w_pa, w_pb, w_o, w_r, b_r, tm_p)
        y = _experts(h2, route[:, ROUTE_E1:ROUTE_E2 + 1].astype(jnp.int32), l, w_exp_gate, w_exp_up, w_exp_down)
        xp = _combine(x1, y, route, mod_p, mod_spec_p, g_final, tm_p, last)

        qa, qi, wi, qb, ga, gb, ka_t, va_t, ki_t, kb_t, vb_t = _inproj(
            xs, l, mod_s, mod_spec_s, g_mix, w_row, w_col, trig_s, trig_spec_s, trig_t_spec_s, 1, tm_s, F32)
        rows_s.append((ka_t, va_t, ki_t, kb_t, vb_t))
        rows8 = lambda a: _sample_rows(a, dec_batch, dec_seq)
        new = lambda a_t: _sample_new_keys(a_t, dec_batch, dec_seq)
        score = _score_sample(rows8(qi), rows8(wi), new(ki_t), cache_idx_t, page_table, l)
        s_len = score.shape[-1]
        bias = _select_sample(score.reshape(dec_batch * SAMPLE_ROWS, s_len),
                              min(IDX_TOPK_MAX, (past_len + dec_seq) // 4), dec_seq)
        ya, yb = _attend_sample(rows8(qa), bias.reshape(dec_batch, SAMPLE_ROWS, s_len), new(ka_t), new(va_t),
                                rows8(qb), new(kb_t), new(vb_t), caches_t, page_table, l, dec_seq)
        unpad = lambda a: a[:, :dec_seq].reshape(n_s, a.shape[-1]).astype(BF16)
        x1, h2, route = _mixout(unpad(ya), unpad(yb), ga, gb, xs, l, mod_s, mod_spec_s, g_ffn, w_pa, w_pb, w_o,
                                w_r, b_r, tm_s)
        y = _experts(h2, route[:, ROUTE_E1:ROUTE_E2 + 1].astype(jnp.int32), l, w_exp_gate, w_exp_up, w_exp_down)
        xs = _combine(x1, y, route, mod_s, mod_spec_s, g_final, tm_s, last)

    def stack_p(i, heads):
        a = jnp.stack([r[i] for r in rows_p], axis=0)
        a = jnp.swapaxes(a, 2, 3)
        return a.reshape((depth, batch, seq, heads, HEAD_DIM) if heads else (depth, batch, seq, D_IDX))

    def stack_s(i, heads):
        a = jnp.stack([r[i][0] for r in rows_s], axis=0)
        a = jnp.swapaxes(a, 1, 2)
        return a.reshape((depth, dec_batch, dec_seq, heads, HEAD_DIM) if heads else (depth, dec_batch, dec_seq, D_IDX))

    return (xp.reshape(batch, seq, d), xs.reshape(dec_batch, dec_seq, d),
            stack_p(0, N_KV_A), stack_p(1, N_KV_A), stack_p(3, N_KV_B), stack_p(4, N_KV_B), stack_p(2, 0),
            stack_s(0, N_KV_A), stack_s(1, N_KV_A), stack_s(3, N_KV_B), stack_s(4, N_KV_B), stack_s(2, 0))
```

```python
import functools

import jax
import jax.numpy as jnp
from jax import lax
from jax.experimental import pallas as pl
from jax.experimental.pallas import tpu as pltpu

HEAD_DIM = 64
N_HEADS_A = 8
N_KV_A = 2
N_IDX_HEADS = 4
D_IDX = 64
IDX_TOPK_MAX = 256
N_HEADS_B = 8
N_KV_B = 2
MOBA_BLOCK = 256
MOBA_TOPK_MAX = 3
N_GROUPS = 4
EXPERTS_PER_GROUP = 8
N_EXPERTS = N_GROUPS * EXPERTS_PER_GROUP
TOPK_IN_GROUP = 2
ROPE_THETA = 10000.0
RMS_EPS = 1e-6

LANES = 128
KV_ROWS = N_KV_A * HEAD_DIM
assert KV_ROWS == LANES == N_KV_B * HEAD_DIM and D_IDX == HEAD_DIM and TOPK_IN_GROUP == 2

VMEM_LIMIT = 56 * 1024 * 1024

F32 = jnp.float32
BF16 = jnp.bfloat16
NEG_INF = float("-inf")
INT_MIN = -(2 ** 31)
NEG_INF_KEY = -2139095041
HIGHEST = lax.Precision.HIGHEST
SCORE_SCALE = HEAD_DIM ** -0.5 * 1.4426950408889634


def _cparams(n_grid):
    return pltpu.CompilerParams(dimension_semantics=("arbitrary",) * n_grid, vmem_limit_bytes=VMEM_LIMIT)


def _dot(a, b, precision=None):
    return jnp.dot(a, b, preferred_element_type=F32, precision=precision)


def _dot_nt(a, b, precision=None):
    return lax.dot_general(a, b, (((1,), (1,)), ((), ())), preferred_element_type=F32, precision=precision)


def _sigmoid(x):
    return 1.0 / (1.0 + jnp.exp(-x))


def _ada_kernel(c_ref, w_ref, b_ref, o_ref):
    c = c_ref[...]
    o_ref[...] = _dot((c * _sigmoid(c)).astype(BF16), w_ref[...].astype(BF16)) + b_ref[...]


def _ada_mod(c_all, w_ada, b_ada):
    depth, d, n6 = w_ada.shape
    m = c_all.shape[0]
    tn = 1024
    return pl.pallas_call(
        _ada_kernel,
        grid=(depth, n6 // tn),
        in_specs=[
            pl.BlockSpec((m, d), lambda l, j: (0, 0)),
            pl.BlockSpec((None, d, tn), lambda l, j: (l, 0, j)),
            pl.BlockSpec((None, 1, tn), lambda l, j: (l, 0, j)),
        ],
        out_specs=pl.BlockSpec((None, m, tn), lambda l, j: (l, 0, j)),
        out_shape=jax.ShapeDtypeStruct((depth, m, n6), F32),
        compiler_params=_cparams(2),
        name="ada_mod",
    )(c_all, w_ada, b_ada.reshape(depth, 1, n6))


_QA_W = N_HEADS_A * HEAD_DIM
_QI_W = N_IDX_HEADS * D_IDX
_QB_W = N_HEADS_B * HEAD_DIM
_ROW_QA, _ROW_QI, _ROW_WI, _ROW_QB = 0, _QA_W, _QA_W + _QI_W, _QA_W + _QI_W + LANES
_ROW_GATE = _ROW_QB + _QB_W
_COL_KA, _COL_VA, _COL_KI, _COL_KB, _COL_VB = 0, KV_ROWS, 2 * KV_ROWS, 2 * KV_ROWS + D_IDX, 3 * KV_ROWS + D_IDX
_COL_END = 4 * KV_ROWS + D_IDX


def _rope_lanes(y, cos, sin, lane):
    first_half = (lane % HEAD_DIM) < (HEAD_DIM // 2)
    rot = jnp.where(first_half, -pltpu.roll(y, LANES - HEAD_DIM // 2, 1), pltpu.roll(y, HEAD_DIM // 2, 1))
    return y * cos + rot * sin


def _rope_rows(y, cos, sin):
    half = HEAD_DIM // 2
    parts = []
    for r0 in range(0, y.shape[0], HEAD_DIM):
        parts += [-y[r0 + half:r0 + HEAD_DIM], y[r0:r0 + half]]
    rot = jnp.concatenate(parts, axis=0)
    return y * cos[:y.shape[0]] + rot * sin[:y.shape[0]]


def _inproj_kernel(x_ref, sc_ref, sh_ref, g_ref, wr_ref, wc_ref, cos_ref, sin_ref, cos_t_ref, sin_t_ref,
                   qa_ref, qi_ref, wi_ref, qb_ref, ga_ref, gb_ref, ka_ref, va_ref, ki_ref, kb_ref, vb_ref,
                   *, d_model):
    x = x_ref[...]
    h = x * lax.rsqrt(jnp.mean(x * x, axis=-1, keepdims=True) + RMS_EPS) * g_ref[...]
    h = h * (1.0 + sc_ref[...]) + sh_ref[...]
    hb = h.astype(BF16)
    cos = cos_ref[...]
    sin = sin_ref[...]
    lane = lax.broadcasted_iota(jnp.int32, cos.shape, 1)

    def proj(c0, width):
        return _dot(hb, wr_ref[:, c0:c0 + width])

    def roped(c0, out_ref, scale=None):
        for c in range(out_ref.shape[1] // LANES):
            y = _rope_lanes(proj(c0 + c * LANES, LANES), cos, sin, lane)
            if scale is not None:
                y = y * scale
            out_ref[:, c * LANES:(c + 1) * LANES] = y.astype(out_ref.dtype)

    roped(_ROW_QA, qa_ref, SCORE_SCALE)
    roped(_ROW_QI, qi_ref)
    wik = proj(_ROW_WI, LANES)
    wi_ref[...] = jnp.where(lane >= D_IDX, _rope_lanes(wik, cos, sin, lane), wik)
    roped(_ROW_QB, qb_ref)
    for c in range(d_model // 512):
        ga_ref[:, c * 512:(c + 1) * 512] = _sigmoid(proj(_ROW_GATE + c * 512, 512)).astype(ga_ref.dtype)
        gb_ref[:, c * 512:(c + 1) * 512] = _sigmoid(proj(_ROW_GATE + d_model + c * 512, 512)).astype(gb_ref.dtype)

    cos_t = cos_t_ref[...]
    sin_t = sin_t_ref[...]

    def proj_t(r0, r1):
        return _dot_nt(wc_ref[r0:r1, :], hb)

    ka_ref[...] = _rope_rows(proj_t(_COL_KA, _COL_VA), cos_t, sin_t)
    va_ref[...] = proj_t(_COL_VA, _COL_KI)
    ki_ref[...] = _rope_rows(proj_t(_COL_KI, _COL_KB), cos_t, sin_t)
    kb_ref[...] = _rope_rows(proj_t(_COL_KB, _COL_VB), cos_t, sin_t)
    vb_ref[...] = proj_t(_COL_VB, _COL_END)


def _split_w_in(w_in):
    depth, d, d_in = w_in.shape
    widths = (_QA_W, KV_ROWS, KV_ROWS, _QI_W, D_IDX, N_IDX_HEADS, _QB_W, KV_ROWS, KV_ROWS, d, d)
    offs = [0]
    for w in widths:
        offs.append(offs[-1] + w)
    assert offs[-1] == d_in
    w_t = jnp.transpose(w_in, (2, 0, 1))
    qa, ka, va, qi, ki, wi, qb, kb, vb, ga, gb = [w_t[offs[i]:offs[i + 1]] for i in range(11)]
    pad = jnp.zeros((LANES - N_IDX_HEADS - D_IDX, depth, d), w_in.dtype)
    w_row = jnp.transpose(jnp.concatenate([qa, qi, wi, pad, ki, qb, ga, gb], axis=0).astype(BF16), (1, 2, 0))
    w_col = jnp.transpose(jnp.concatenate([ka, va, ki, kb, vb], axis=0).astype(BF16), (1, 0, 2))
    return w_row, w_col


def _inproj(x, layer, mod, mod_spec, g_mix, w_row, w_col, trig, trig_spec, trig_t_spec, n_batch, tm, q_dtype):
    n, d = x.shape
    per_batch = n // n_batch
    nq = per_batch // tm
    cos, sin, cos_t, sin_t = trig
    row = lambda width: pl.BlockSpec((tm, width), lambda i: (i, 0))
    col = lambda rows: pl.BlockSpec((None, rows, tm), lambda i: (i // nq, 0, i % nq))
    row_out = [(_QA_W, q_dtype), (_QI_W, q_dtype), (LANES, F32), (_QB_W, F32), (d, q_dtype), (d, q_dtype)]
    col_out = [KV_ROWS, KV_ROWS, D_IDX, KV_ROWS, KV_ROWS]
    layer_w = lambda a: pl.BlockSpec((None,) + a.shape[1:], lambda i: (layer, 0, 0))
    return pl.pallas_call(
        functools.partial(_inproj_kernel, d_model=d),
        grid=(n // tm,),
        in_specs=[row(d), mod_spec(1), mod_spec(0), pl.BlockSpec((None, 1, d), lambda i: (layer, 0, 0)),
                  layer_w(w_row), layer_w(w_col), trig_spec, trig_spec, trig_t_spec, trig_t_spec],
        out_specs=[row(w) for w, _ in row_out] + [col(r) for r in col_out],
        out_shape=[jax.ShapeDtypeStruct((n, w), dt) for w, dt in row_out]
        + [jax.ShapeDtypeStruct((n_batch, r, per_batch), F32) for r in col_out],
        compiler_params=_cparams(1),
        name="inproj",
    )(x, mod, mod, g_mix.reshape(g_mix.shape[0], 1, d), w_row, w_col, cos, sin, cos_t, sin_t)


def _stack_heads(q, first_head, n_heads, scale=None):
    q = q.astype(F32)
    if scale is not None:
        q = q * scale
    parts = [q[:, (first_head + g) * HEAD_DIM:(first_head + g + 1) * HEAD_DIM] for g in range(n_heads)]
    return jnp.concatenate(parts, axis=0).astype(BF16)


def _sort_key(x):
    bits = lax.bitcast_convert_type(x, jnp.int32)
    return bits ^ ((bits >> 31) & 0x7FFFFFFF)


def _count(mask):
    return jnp.sum(jnp.where(mask, 1.0, 0.0), axis=1, keepdims=True)


def _indexer_scores(qi, wi, ki_t, q_pos):
    t = qi.shape[0]
    logits = _dot(_stack_heads(qi, 0, N_IDX_HEADS), ki_t)
    score = None
    for h in range(N_IDX_HEADS):
        term = jnp.maximum(logits[h * t:(h + 1) * t], 0.0) * wi[:, h:h + 1]
        score = term if score is None else score + term
    k_pos = lax.broadcasted_iota(jnp.int32, score.shape, 1)
    score = jnp.where(score == 0.0, 0.0, score)
    return jnp.where(k_pos <= q_pos, score, NEG_INF)


def _topk_mask(score, n_keep, row_is_real, bias_ref):
    t, s_len = score.shape
    key = _sort_key(score)
    keep = float(n_keep)
    thr = jnp.where(_count(key >= 0) >= keep, 0, INT_MIN).astype(jnp.int32)

    def two_bits(i, thr):
        hi = jnp.left_shift(jnp.int32(1), 30 - 2 * i)
        lo = jnp.left_shift(jnp.int32(1), 29 - 2 * i)
        c_hi, c_lo, c_both = thr + hi, thr + lo, thr + hi + lo
        n_hi, n_lo, n_both = _count(key >= c_hi), _count(key >= c_lo), _count(key >= c_both)
        return jnp.where(n_both >= keep, c_both, jnp.where(n_hi >= keep, c_hi, jnp.where(n_lo >= keep, c_lo, thr)))

    thr = lax.fori_loop(0, 15, two_bits, thr)
    thr = jnp.where(_count(key >= thr + 1) >= keep, thr + 1, thr)
    at_least = key >= thr
    n_above = _count(key > thr)
    bias_ref[...] = jnp.where(at_least & (key > NEG_INF_KEY), 0.0, NEG_INF)

    tied = (_count(at_least) > keep) & (thr > NEG_INF_KEY) & row_is_real

    @pl.when(jnp.max(jnp.where(tied, 1.0, 0.0)) > 0.0)
    def _():
        free = keep - n_above
        r_i = lax.broadcasted_iota(jnp.int32, (MOBA_BLOCK, MOBA_BLOCK), 0)
        c_i = lax.broadcasted_iota(jnp.int32, (MOBA_BLOCK, MOBA_BLOCK), 1)
        tri = jnp.where(r_i <= c_i, 1.0, 0.0).astype(BF16)
        seen = jnp.zeros((t, 1), F32)
        for c0 in range(0, s_len, MOBA_BLOCK):
            w = min(MOBA_BLOCK, s_len - c0)
            key_c = key[:, c0:c0 + w]
            eq_c = jnp.where(key_c == thr, 1.0, 0.0)
            rank = _dot(eq_c.astype(BF16), tri[:w, :w]) + seen
            keep_c = ((key_c > thr) | ((key_c == thr) & (rank <= free))) & (key_c > NEG_INF_KEY)
            bias_ref[:, c0:c0 + w] = jnp.where(keep_c, 0.0, NEG_INF)
            seen = seen + jnp.sum(eq_c, axis=1, keepdims=True)


def _indexer_scores_t(kik, kiw_q, qi, q_pos_t):
    t = qi.shape[0]
    q = qi.astype(F32)
    lane = lax.broadcasted_iota(jnp.int32, (t, LANES), 1)
    parts = []
    for h in range(N_IDX_HEADS):
        pair = q[:, (h // 2) * LANES:(h // 2 + 1) * LANES]
        if h % 2:
            parts.append(jnp.where(lane >= D_IDX, pair, 0.0))
        else:
            parts.append(pltpu.roll(jnp.where(lane < D_IDX, pair, 0.0), D_IDX, 1))
    logits = _dot_nt(kik, jnp.concatenate(parts, axis=0).astype(BF16))
    wi_t = kiw_q.T
    score = None
    for h in range(N_IDX_HEADS):
        term = jnp.maximum(logits[:, h * t:(h + 1) * t], 0.0) * wi_t[h:h + 1, :]
        score = term if score is None else score + term
    k_pos = lax.broadcasted_iota(jnp.int32, score.shape, 0)
    score = jnp.where(score == 0.0, 0.0, score)
    return jnp.where(k_pos <= q_pos_t, score, NEG_INF)


COUNT_SLAB = 64


def _count_t(mask):
    ones = jnp.where(mask, 1.0, 0.0)
    s_len, t = ones.shape
    if s_len % COUNT_SLAB == 0 and s_len > COUNT_SLAB:
        ones = jnp.sum(ones.reshape(s_len // COUNT_SLAB, COUNT_SLAB, t), axis=0)
    return jnp.sum(ones, axis=0, keepdims=True)


def _store_transposed(mask_t, c0, bias_ref):
    for j in range(0, mask_t.shape[0], LANES):
        bias_ref[:, c0 + j:c0 + j + LANES] = mask_t[j:j + LANES, :].T


def _topk_mask_t(score_t, n_keep, bias_ref):
    s_len, t = score_t.shape
    key = _sort_key(score_t)
    keep = float(n_keep)
    thr = jnp.where(_count_t(key >= 0) >= keep, 0, INT_MIN).astype(jnp.int32)

    def bit_step(i, thr):
        cand = thr + jnp.left_shift(jnp.int32(1), 30 - i)
        return jnp.where(_count_t(key >= cand) >= keep, cand, thr)

    thr = lax.fori_loop(0, 31, bit_step, thr)
    at_least = key >= thr
    n_above = _count_t(key > thr)
    _store_transposed(jnp.where(at_least & (key > NEG_INF_KEY), 0.0, NEG_INF), 0, bias_ref)

    tied = (_count_t(at_least) > keep) & (thr > NEG_INF_KEY)

    @pl.when(jnp.max(jnp.where(tied, 1.0, 0.0)) > 0.0)
    def _():
        free = keep - n_above
        r_i = lax.broadcasted_iota(jnp.int32, (MOBA_BLOCK, MOBA_BLOCK), 0)
        c_i = lax.broadcasted_iota(jnp.int32, (MOBA_BLOCK, MOBA_BLOCK), 1)
        tri = jnp.where(c_i <= r_i, 1.0, 0.0).astype(BF16)
        seen = jnp.zeros((1, t), F32)
        for c0 in range(0, s_len, MOBA_BLOCK):
            w = min(MOBA_BLOCK, s_len - c0)
            key_c = key[c0:c0 + w, :]
            eq_c = jnp.where(key_c == thr, 1.0, 0.0)
            rank = _dot(tri[:w, :w], eq_c.astype(BF16)) + seen
            keep_c = ((key_c > thr) | ((key_c == thr) & (rank <= free))) & (key_c > NEG_INF_KEY)
            _store_transposed(jnp.where(keep_c, 0.0, NEG_INF), c0, bias_ref)
            seen = seen + jnp.sum(eq_c, axis=0, keepdims=True)


ONES_ROWS = 16


def _with_ones(v_t):
    return jnp.concatenate([v_t, jnp.ones((ONES_ROWS, v_t.shape[1]), BF16)], axis=0)


def _normalised_pv(p, v_t):
    o = _dot_nt(p, _with_ones(v_t))
    return o[:, :HEAD_DIM] / o[:, HEAD_DIM:HEAD_DIM + 1]


def _softmax_pv(s, v_t):
    m = jnp.max(s, axis=1, keepdims=True)
    return _normalised_pv(jnp.exp2(s - m).astype(BF16), v_t)


def _attend_group(q_g, k_t, v_t, bias):
    s = _dot(q_g, k_t)
    g = s.shape[0] // bias.shape[0]
    s = (s.reshape(g, bias.shape[0], s.shape[1]) + bias[None]).reshape(s.shape)
    return _softmax_pv(s, v_t)


def _store_heads(o_g, t, first_head, n_heads, out_ref):
    for g in range(0, n_heads, 2):
        pair = jnp.concatenate([o_g[g * t:(g + 1) * t], o_g[(g + 1) * t:(g + 2) * t]], axis=1)
        c0 = (first_head + g) * HEAD_DIM
        out_ref[:, c0:c0 + 2 * HEAD_DIM] = pair.astype(out_ref.dtype)


def _dsa_attention(qa, ka_t, va_t, bias, out_ref):
    t = qa.shape[0]
    hpk = N_HEADS_A // N_KV_A
    for kv in range(N_KV_A):
        k_t = ka_t[kv * HEAD_DIM:(kv + 1) * HEAD_DIM].astype(BF16)
        v_t = va_t[kv * HEAD_DIM:(kv + 1) * HEAD_DIM].astype(BF16)
        o_g = _attend_group(_stack_heads(qa, kv * hpk, hpk), k_t, v_t, bias)
        _store_heads(o_g, t, kv * hpk, hpk, out_ref)


GATE_ROWS = 8


def _block_means(kb_t, n_full):
    lane = lax.broadcasted_iota(jnp.int32, (KV_ROWS, LANES), 1)
    km = jnp.zeros((KV_ROWS, LANES), F32)
    for n in range(n_full):
        col = jnp.sum(kb_t[:, n * MOBA_BLOCK:(n + 1) * MOBA_BLOCK], axis=1, keepdims=True) * (1.0 / MOBA_BLOCK)
        km = jnp.where(lane == n, col, km)
    return km.T


def _moba_queries(qb, kv):
    hpk = N_HEADS_B // N_KV_B
    return jnp.concatenate([qb[:, (kv * hpk + g) * HEAD_DIM:(kv * hpk + g + 1) * HEAD_DIM] for g in range(hpk)],
                           axis=0)


def _moba_pick(q_f, k_mean_kv, own, n_sel, n_full):
    assert n_full <= GATE_ROWS
    rows = q_f.shape[0]
    rows_pad = -(-rows // LANES) * LANES
    if rows_pad != rows:
        q_f = jnp.concatenate([q_f, jnp.zeros((rows_pad - rows, HEAD_DIM), F32)], axis=0)
    blk = lax.broadcasted_iota(jnp.int32, (GATE_ROWS, rows_pad), 0)
    blk_f = blk.astype(F32)
    gate = _dot_nt(k_mean_kv, q_f, precision=HIGHEST)
    gate = jnp.where(blk < jnp.minimum(own, n_full), gate, NEG_INF)
    picked_t = jnp.zeros((GATE_ROWS, rows_pad), F32)
    for _ in range(n_sel):
        best = jnp.max(gate, axis=0, keepdims=True)
        first = jnp.min(jnp.where(gate == best, blk_f, float(GATE_ROWS)), axis=0, keepdims=True)
        hit = (blk_f == first) & (best > NEG_INF)
        picked_t = jnp.where(hit, 1.0, picked_t)
        gate = jnp.where(blk_f == first, NEG_INF, gate)
    picked_t = jnp.concatenate([picked_t, jnp.zeros((LANES - GATE_ROWS, rows_pad), F32)], axis=0)
    return picked_t.T[:rows]


def _moba_scores(q_g, k_t, picked, q_pos_g, own, own_min, n_full, s_ref):
    rows, s_len = q_g.shape[0], k_t.shape[1]
    for c0 in range(0, s_len, MOBA_BLOCK):
        c = c0 // MOBA_BLOCK
        w = min(MOBA_BLOCK, s_len - c0)
        if c < own_min:
            allowed = picked[:, c:c + 1] > 0.0
        else:
            k_pos = c0 + lax.broadcasted_iota(jnp.int32, (rows, w), 1)
            allowed = (k_pos <= q_pos_g) & (own == c)
            if c < n_full:
                allowed = allowed | (picked[:, c:c + 1] > 0.0)
        s_ref[:, c0:c0 + w] = jnp.where(allowed, _dot(q_g, k_t[:, c0:c0 + w]), NEG_INF)


def _moba_attention(qb, kb_t, vb_t, k_mean, q_pos, own, own_min, n_sel, n_full, s_ref, out_ref):
    t = qb.shape[0]
    s_len = kb_t.shape[1]
    hpk = N_HEADS_B // N_KV_B
    q_pos_g = jnp.concatenate([q_pos] * hpk, axis=0)
    for kv in range(N_KV_B):
        q_f = _moba_queries(qb, kv)
        picked = _moba_pick(q_f, k_mean[:GATE_ROWS, kv * HEAD_DIM:(kv + 1) * HEAD_DIM], own, n_sel, n_full)
        k_t = kb_t[kv * HEAD_DIM:(kv + 1) * HEAD_DIM].astype(BF16)
        v_t = vb_t[kv * HEAD_DIM:(kv + 1) * HEAD_DIM].astype(BF16)
        _moba_scores((q_f * SCORE_SCALE).astype(BF16), k_t, picked, q_pos_g, own, own_min, n_full,
                     s_ref.at[:, :s_len])
        _store_heads(_softmax_pv(s_ref[:, :s_len], v_t), t, kv * hpk, hpk, out_ref)


PROMPT_TQ = 128
KEY_SEGMENT = MOBA_BLOCK


def _per_key_segment(seq, tq, body):
    n_seg = max(1, seq // KEY_SEGMENT)
    seg_len = seq // n_seg
    assert seg_len % tq == 0 and seg_len % MOBA_BLOCK == 0
    seg = (pl.program_id(1) * tq) // seg_len
    for k in range(n_seg):
        pl.when(seg == k)(functools.partial(body, (k + 1) * seg_len, seg_len))


def _dsa_prompt_kernel(qi_ref, wik_q_ref, qa_ref, wik_k_ref, ka_ref, va_ref, o_ref, bias_ref, kik_ref,
                       *, tq, seq, n_keep):
    q_pos_t = pl.program_id(1) * tq + lax.broadcasted_iota(jnp.int32, (1, tq), 1)

    @pl.when(pl.program_id(1) == 0)
    def _():
        kik_ref[...] = wik_k_ref[...].astype(BF16)

    def body(s_len, seg_len):
        score_t = _indexer_scores_t(kik_ref[:s_len, :], wik_q_ref[...], qi_ref[...], q_pos_t)
        _topk_mask_t(score_t, n_keep, bias_ref.at[:, :s_len])
        _dsa_attention(qa_ref[...], ka_ref[:, :s_len], va_ref[:, :s_len], bias_ref[:, :s_len], o_ref)

    _per_key_segment(seq, tq, body)


def _dsa_prompt(qi, wik, qa, ka_t, va_t, tq):
    batch, _, seq = ka_t.shape
    assert tq == LANES
    nq = seq // tq
    qrow = lambda w: pl.BlockSpec((tq, w), lambda b, i: (b * nq + i, 0))
    keys_t = pl.BlockSpec((None, KV_ROWS, seq), lambda b, i: (b, 0, 0))
    return pl.pallas_call(
        functools.partial(_dsa_prompt_kernel, tq=tq, seq=seq, n_keep=min(IDX_TOPK_MAX, seq // 4)),
        grid=(batch, nq),
        in_specs=[qrow(qi.shape[1]), qrow(LANES), qrow(qa.shape[1]),
                  pl.BlockSpec((seq, LANES), lambda b, i: (b, 0)), keys_t, keys_t],
        out_specs=qrow(qa.shape[1]),
        out_shape=jax.ShapeDtypeStruct(qa.shape, BF16),
        scratch_shapes=[pltpu.VMEM((tq, seq), F32), pltpu.VMEM((seq, LANES), BF16)],
        compiler_params=_cparams(2),
        name="dsa_prompt",
    )(qi, wik, qa, wik, ka_t, va_t)


def _moba_prompt_kernel(qb_ref, kb_ref, vb_ref, o_ref, bias_ref, km_ref, *, tq, seq, n_sel):
    base = pl.program_id(1) * tq
    q_pos = base + lax.broadcasted_iota(jnp.int32, (tq, 1), 0)
    n_full = seq // MOBA_BLOCK

    @pl.when(pl.program_id(1) == 0)
    def _():
        km_ref[...] = _block_means(kb_ref[...], n_full)

    def body(s_len, seg_len):
        _moba_attention(qb_ref[...], kb_ref[:, :s_len], vb_ref[:, :s_len], km_ref[...], q_pos, base // MOBA_BLOCK,
                        (s_len - seg_len) // MOBA_BLOCK, n_sel, n_full, bias_ref, o_ref)

    _per_key_segment(seq, tq, body)


def _moba_prompt(qb, kb_t, vb_t, tq):
    batch, _, seq = kb_t.shape
    assert MOBA_BLOCK % tq == 0 and seq % MOBA_BLOCK == 0
    nq = seq // tq
    qrow = pl.BlockSpec((tq, qb.shape[1]), lambda b, i: (b * nq + i, 0))
    keys = pl.BlockSpec((None, KV_ROWS, seq), lambda b, i: (b, 0, 0))
    return pl.pallas_call(
        functools.partial(_moba_prompt_kernel, tq=tq, seq=seq, n_sel=min(MOBA_TOPK_MAX, seq // MOBA_BLOCK)),
        grid=(batch, nq),
        in_specs=[qrow, keys, keys],
        out_specs=qrow,
        out_shape=jax.ShapeDtypeStruct(qb.shape, BF16),
        scratch_shapes=[pltpu.VMEM((N_HEADS_B // N_KV_B * tq, seq), F32), pltpu.VMEM((KV_ROWS, LANES), F32)],
        compiler_params=_cparams(2),
        name="moba_prompt",
    )(qb, kb_t, vb_t)


SAMPLE_ROWS = 8


def _assemble_keys(page_refs, new_ref, dst_ref, page_size):
    for j, page in enumerate(page_refs):
        dst_ref[:, j * page_size:(j + 1) * page_size] = page[...]
    dst_ref[:, len(page_refs) * page_size:] = new_ref[...]


def _page_specs(layer, n_pages, rows, page_size):
    return [pl.BlockSpec((None, None, rows, page_size), lambda b, pt, j=j: (layer, pt[b, j], 0, 0))
            for j in range(n_pages)]


def _score_sample_kernel(pt_ref, qi_ref, wi_ref, kin_ref, *rest, n_pages, page_size):
    ki_pages = rest[:n_pages]
    o_ref, ki_s = rest[n_pages:]
    _assemble_keys(ki_pages, kin_ref, ki_s, page_size)
    q_pos = n_pages * page_size + lax.broadcasted_iota(jnp.int32, (SAMPLE_ROWS, 1), 0)
    o_ref[...] = _indexer_scores(qi_ref[...], wi_ref[...], ki_s[...].astype(BF16), q_pos)


def _score_sample(qi, wi, ki_new, cache_idx_t, page_table, layer):
    b, n_pages = page_table.shape
    page_size = cache_idx_t.shape[3]
    s_len = n_pages * page_size + LANES
    per_b = lambda r, w: pl.BlockSpec((None, r, w), lambda i, pt: (i, 0, 0))
    grid_spec = pltpu.PrefetchScalarGridSpec(
        num_scalar_prefetch=1,
        grid=(b,),
        in_specs=[per_b(SAMPLE_ROWS, qi.shape[2]), per_b(SAMPLE_ROWS, LANES), per_b(D_IDX, LANES)]
        + _page_specs(layer, n_pages, D_IDX, page_size),
        out_specs=per_b(SAMPLE_ROWS, s_len),
        scratch_shapes=[pltpu.VMEM((D_IDX, s_len), F32)],
    )
    return pl.pallas_call(
        functools.partial(_score_sample_kernel, n_pages=n_pages, page_size=page_size),
        grid_spec=grid_spec,
        out_shape=jax.ShapeDtypeStruct((b, SAMPLE_ROWS, s_len), F32),
        compiler_params=_cparams(1),
        name="score_sample",
    )(page_table, qi, wi, ki_new, *([cache_idx_t] * n_pages))


def _select_kernel(score_ref, bias_ref, *, n_keep, dec_seq):
    rows = score_ref.shape[0]
    row = lax.broadcasted_iota(jnp.int32, (rows, 1), 0)
    _topk_mask(score_ref[...], n_keep, (row % SAMPLE_ROWS) < dec_seq, bias_ref)


def _select_sample(score, n_keep, dec_seq):
    n, s_len = score.shape
    tr = min(256, n)
    assert n % tr == 0 and tr % SAMPLE_ROWS == 0
    spec = pl.BlockSpec((tr, s_len), lambda i: (i, 0))
    return pl.pallas_call(
        functools.partial(_select_kernel, n_keep=n_keep, dec_seq=dec_seq),
        grid=(n // tr,),
        in_specs=[spec],
        out_specs=spec,
        out_shape=jax.ShapeDtypeStruct((n, s_len), F32),
        compiler_params=_cparams(1),
        name="select_sample",
    )(score)


SAMPLE_GROUP = 1


def _attend_sample_kernel(pt_ref, qa_ref, bias_ref, kan_ref, van_ref, qb_ref, kbn_ref, vbn_ref, *rest,
                          n_pages, page_size, n_sel):
    n_page_refs = 4 * SAMPLE_GROUP * n_pages
    ya_ref, yb_ref = rest[n_page_refs:n_page_refs + 2]
    scratch = rest[n_page_refs + 2:]
    past = n_pages * page_size
    n_full = past // MOBA_BLOCK
    hpk = N_HEADS_A // N_KV_A
    rows = hpk * SAMPLE_ROWS
    q_pos_g = past + lax.broadcasted_iota(jnp.int32, (rows, 1), 0) % SAMPLE_ROWS
    for g in range(SAMPLE_GROUP):
        ka_s, va_s, kb_s, vb_s, s_ref = scratch[5 * g:5 * g + 5]
        for k, (new_ref, dst) in enumerate(zip((kan_ref, van_ref, kbn_ref, vbn_ref), (ka_s, va_s, kb_s, vb_s))):
            first = (k * SAMPLE_GROUP + g) * n_pages
            _assemble_keys(rest[first:first + n_pages], new_ref.at[g], dst, page_size)
        qa, qb = qa_ref[g], qb_ref[g]
        bias_g = jnp.concatenate([bias_ref[g]] * hpk, axis=0)
        kb_t = kb_s[...]
        k_mean = _block_means(kb_t, n_full)
        for kv in range(N_KV_A):
            k_t = ka_s[kv * HEAD_DIM:(kv + 1) * HEAD_DIM, :].astype(BF16)
            s_ref[kv * rows:(kv + 1) * rows, :] = _dot(_stack_heads(qa, kv * hpk, hpk), k_t) + bias_g
            q_f = _moba_queries(qb, kv)
            picked = _moba_pick(q_f, k_mean[:GATE_ROWS, kv * HEAD_DIM:(kv + 1) * HEAD_DIM], n_full, n_sel, n_full)
            _moba_scores((q_f * SCORE_SCALE).astype(BF16), kb_t[kv * HEAD_DIM:(kv + 1) * HEAD_DIM].astype(BF16),
                         picked, q_pos_g, n_full, n_full, n_full,
                         s_ref.at[(N_KV_A + kv) * rows:(N_KV_A + kv + 1) * rows, :])
        s = s_ref[...]
        p = jnp.exp2(s - jnp.max(s, axis=1, keepdims=True))
        l = jnp.sum(p, axis=1, keepdims=True)
        p = p.astype(BF16)
        groups = ((va_s, ya_ref, 0), (va_s, ya_ref, 1), (vb_s, yb_ref, 0), (vb_s, yb_ref, 1))
        for idx, (v_s, out_ref, kv) in enumerate(groups):
            r0 = idx * rows
            v_t = v_s[kv * HEAD_DIM:(kv + 1) * HEAD_DIM, :].astype(BF16)
            o_g = _dot_nt(p[r0:r0 + rows], v_t) / l[r0:r0 + rows]
            _store_heads(o_g, SAMPLE_ROWS, kv * hpk, hpk, out_ref.at[g])


def _attend_sample(qa, bias, ka_new, va_new, qb, kb_new, vb_new, caches_t, page_table, layer, dec_seq):
    b, n_pages = page_table.shape
    page_size = caches_t[0].shape[3]
    past = n_pages * page_size
    assert past % MOBA_BLOCK == 0 and dec_seq <= SAMPLE_ROWS and b % SAMPLE_GROUP == 0
    s_len = past + LANES
    n_blocks = -(-(past + dec_seq) // MOBA_BLOCK)
    per_step = lambda r, w: pl.BlockSpec((SAMPLE_GROUP, r, w), lambda i, pt: (i, 0, 0))
    new = per_step(KV_ROWS, LANES)
    q = per_step(SAMPLE_ROWS, qa.shape[2])
    page_specs = [pl.BlockSpec((None, None, KV_ROWS, page_size),
                               lambda i, pt, g=g, j=j: (layer, pt[i * SAMPLE_GROUP + g, j], 0, 0))
                  for _ in range(4) for g in range(SAMPLE_GROUP) for j in range(n_pages)]
    grid_spec = pltpu.PrefetchScalarGridSpec(
        num_scalar_prefetch=1,
        grid=(b // SAMPLE_GROUP,),
        in_specs=[q, per_step(SAMPLE_ROWS, s_len), new, new, q, new, new] + page_specs,
        out_specs=[q, q],
        scratch_shapes=([pltpu.VMEM((KV_ROWS, s_len), F32)] * 4
                        + [pltpu.VMEM(((N_HEADS_A + N_HEADS_B) * SAMPLE_ROWS, s_len), F32)]) * SAMPLE_GROUP,
    )
    return pl.pallas_call(
        functools.partial(_attend_sample_kernel, n_pages=n_pages, page_size=page_size,
                          n_sel=min(MOBA_TOPK_MAX, n_blocks)),
        grid_spec=grid_spec,
        out_shape=[jax.ShapeDtypeStruct(qa.shape, F32)] * 2,
        compiler_params=_cparams(1),
        name="attend_sample",
    )(page_table, qa, bias, ka_new, va_new, qb, kb_new, vb_new,
      *[c for c in caches_t for _ in range(SAMPLE_GROUP * n_pages)])


ROUTE_E1, ROUTE_E2, ROUTE_C1, ROUTE_C2 = 0, 1, 2, 3


def _mixout_kernel(ya_ref, yb_ref, ga_ref, gb_ref, x_ref, gt1_ref, sc2_ref, sh2_ref, g_ref,
                   wpa_ref, wpb_ref, wout_ref, wr_ref, br_ref, x1_ref, h2_ref, route_ref):
    merged = (ga_ref[...].astype(F32) * _dot(ya_ref[...], wpa_ref[...])
              + gb_ref[...].astype(F32) * _dot(yb_ref[...], wpb_ref[...]))
    mix = _dot(merged.astype(BF16), wout_ref[...])
    x1 = x_ref[...] + gt1_ref[...] * mix
    x1_ref[...] = x1
    h2 = x1 * lax.rsqrt(jnp.mean(x1 * x1, axis=-1, keepdims=True) + RMS_EPS) * g_ref[...]
    h2 = h2 * (1.0 + sc2_ref[...]) + sh2_ref[...]
    h2_ref[...] = h2

    logits = _dot(h2, wr_ref[...], precision=HIGHEST) + br_ref[...]
    lane = lax.broadcasted_iota(jnp.int32, logits.shape, 1)
    lane_f = lane.astype(F32)
    is_grp = lane < N_GROUPS
    grp = jnp.where(is_grp, logits, NEG_INF)
    g_max = jnp.max(grp, axis=1, keepdims=True)
    g_star = jnp.min(jnp.where(grp == g_max, lane_f, float(N_GROUPS)), axis=1, keepdims=True)
    p_g = 1.0 / jnp.sum(jnp.where(is_grp, jnp.exp(grp - g_max), 0.0), axis=1, keepdims=True)
    e_id = lane_f - float(N_GROUPS)
    in_grp = (e_id >= g_star * EXPERTS_PER_GROUP) & (e_id < (g_star + 1.0) * EXPERTS_PER_GROUP)
    cand = jnp.where(in_grp, logits, NEG_INF)
    l1 = jnp.max(cand, axis=1, keepdims=True)
    e1 = jnp.min(jnp.where(cand == l1, e_id, float(N_EXPERTS)), axis=1, keepdims=True)
    cand = jnp.where(e_id == e1, NEG_INF, cand)
    l2 = jnp.max(cand, axis=1, keepdims=True)
    e2 = jnp.min(jnp.where(cand == l2, e_id, float(N_EXPERTS)), axis=1, keepdims=True)
    t = jnp.exp(l2 - l1)
    c1 = p_g / (1.0 + t)
    c2 = p_g * t / (1.0 + t)
    route = jnp.where(lane == ROUTE_E1, e1, 0.0)
    route = jnp.where(lane == ROUTE_E2, e2, route)
    route = jnp.where(lane == ROUTE_C1, c1, route)
    route_ref[...] = jnp.where(lane == ROUTE_C2, c2, route)


def _mixout(ya, yb, ga, gb, x, layer, mod, mod_spec, g_ffn, w_pa, w_pb, w_out, w_r, b_r, tm):
    n, d = x.shape
    row = lambda width: pl.BlockSpec((tm, width), lambda i: (i, 0))
    layer_w = lambda a: pl.BlockSpec((None,) + a.shape[1:], lambda i: (layer, 0, 0))
    return pl.pallas_call(
        _mixout_kernel,
        grid=(n // tm,),
        in_specs=[row(ya.shape[1]), row(yb.shape[1]), row(d), row(d), row(d),
                  mod_spec(2), mod_spec(4), mod_spec(3), pl.BlockSpec((None, 1, d), lambda i: (layer, 0, 0)),
                  layer_w(w_pa), layer_w(w_pb), layer_w(w_out), layer_w(w_r), layer_w(b_r)],
        out_specs=[row(d), row(d), row(LANES)],
        out_shape=[jax.ShapeDtypeStruct((n, d), F32), jax.ShapeDtypeStruct((n, d), F32),
                   jax.ShapeDtypeStruct((n, LANES), F32)],
        compiler_params=_cparams(1),
        name="mixout",
    )(ya, yb, ga, gb, x, mod, mod, mod, g_ffn.reshape(g_ffn.shape[0], 1, d), w_pa, w_pb, w_out, w_r, b_r)


MOE_ROWS = 256
ROW_CHUNK = 32
SUBLANES = 8
N_BUF = 2


def _experts_kernel(blk_e_ref, n_valid_ref, src_ref, order_ref, h_hbm, wg_ref, wu_ref, wd_ref, y_hbm,
                    x_buf, y_buf, gather_sem, scatter_sem, *, n_tokens, n_blocks):
    i = pl.program_id(0)
    buf = i % N_BUF
    n_slots = TOPK_IN_GROUP * n_tokens

    def n_chunks(b):
        return jnp.right_shift(n_valid_ref[b] + (ROW_CHUNK - 1), ROW_CHUNK.bit_length() - 1)

    def slot_of(b, r):
        return order_ref[jnp.minimum(src_ref[b] + r, n_slots - 1)]

    tiles_per_chunk = ROW_CHUNK // SUBLANES

    def row_view(buf_ref, bf, c, u):
        return buf_ref.at[bf, c * tiles_per_chunk + u // SUBLANES, pl.ds(u % SUBLANES, 1)]

    def chunk_view(buf_ref, bf, c):
        return buf_ref.at[bf, pl.ds(c * tiles_per_chunk, tiles_per_chunk)]

    def for_rows(b, fn):
        def chunk(c, carry):
            for u in range(ROW_CHUNK):
                fn(c, u)
            return carry
        lax.fori_loop(0, n_chunks(b), chunk, 0)

    def wait_chunks(b, buf_ref, bf, sem):
        def chunk(c, carry):
            view = chunk_view(buf_ref, bf, c)
            pltpu.make_async_copy(view, view, sem.at[bf]).wait()
            return carry
        lax.fori_loop(0, n_chunks(b), chunk, 0)

    def start_gather(b, bf):
        def row(c, u):
            slot = slot_of(b, c * ROW_CHUNK + u)
            token = jnp.where(slot >= n_tokens, slot - n_tokens, slot)
            pltpu.make_async_copy(h_hbm.at[pl.ds(token, 1)], row_view(x_buf, bf, c, u), gather_sem.at[bf]).start()
        for_rows(b, row)

    def start_scatter(b, bf):
        n_valid = n_valid_ref[b]

        def row(c, u):
            r = c * ROW_CHUNK + u
            dst = jnp.where(r < n_valid, slot_of(b, r), n_slots + bf * MOE_ROWS + r)
            pltpu.make_async_copy(row_view(y_buf, bf, c, u), y_hbm.at[pl.ds(dst, 1)], scatter_sem.at[bf]).start()
        for_rows(b, row)

    @pl.when(i == 0)
    def _():
        x_buf[...] = jnp.zeros_like(x_buf)
        y_buf[...] = jnp.zeros_like(y_buf)
        for bf in range(N_BUF):
            for tile in range(MOE_ROWS // SUBLANES):
                pltpu.make_async_copy(
                    y_buf.at[bf, tile], y_hbm.at[pl.ds(n_slots + bf * MOE_ROWS + tile * SUBLANES, SUBLANES)],
                    scatter_sem.at[bf]).start()
            pltpu.make_async_copy(y_buf.at[bf], y_buf.at[bf], scatter_sem.at[bf]).wait()
        start_gather(0, 0)

    @pl.when(i + 1 < n_blocks)
    def _():
        start_gather(i + 1, 1 - buf)

    wait_chunks(i, x_buf, buf, gather_sem)

    @pl.when(n_valid_ref[i] > 0)
    def _():
        d_model = x_buf.shape[-1]
        xb = x_buf[buf].reshape(MOE_ROWS, d_model).astype(BF16)
        gate = _dot(xb, wg_ref[...].astype(BF16))
        hidden = gate * _sigmoid(gate) * _dot(xb, wu_ref[...].astype(BF16))
        y = _dot(hidden.astype(BF16), wd_ref[...].astype(BF16))
        y_buf[buf] = y.reshape(MOE_ROWS // SUBLANES, SUBLANES, d_model)

    start_scatter(i, buf)

    @pl.when(i > 0)
    def _():
        wait_chunks(i - 1, y_buf, 1 - buf, scatter_sem)

    @pl.when(i == n_blocks - 1)
    def _():
        wait_chunks(i, y_buf, buf, scatter_sem)


def _experts(h2, expert_ids, layer, w_gate, w_up, w_down):
    n, d = h2.shape
    m = TOPK_IN_GROUP * n
    n_blocks = -(-m // MOE_ROWS) + N_EXPERTS
    flat_e = expert_ids.T.reshape(m)
    order = jnp.argsort(flat_e).astype(jnp.int32)
    experts = jnp.arange(N_EXPERTS, dtype=jnp.int32)
    counts = jnp.sum((flat_e[:, None] == experts[None, :]).astype(jnp.int32), axis=0)
    padded = (counts + MOE_ROWS - 1) // MOE_ROWS * MOE_ROWS
    pad_end = jnp.cumsum(padded)
    pad_start = pad_end - padded
    start = jnp.cumsum(counts) - counts
    block_start = jnp.arange(n_blocks, dtype=jnp.int32) * MOE_ROWS
    blk_e = jnp.minimum(jnp.sum((pad_end[None, :] <= block_start[:, None]).astype(jnp.int32), axis=1), N_EXPERTS - 1)
    offset = block_start - pad_start[blk_e]
    n_valid = jnp.clip(counts[blk_e] - offset, 0, MOE_ROWS).astype(jnp.int32)
    src = jnp.clip(start[blk_e] + offset, 0, m - 1).astype(jnp.int32)
    blk_e = jnp.where(n_valid > 0, blk_e, jnp.max(jnp.where(n_valid > 0, blk_e, 0))).astype(jnp.int32)

    w_spec = lambda a: pl.BlockSpec((None, None) + a.shape[2:], lambda i, be, nv, sr, od: (layer, be[i], 0, 0))
    grid_spec = pltpu.PrefetchScalarGridSpec(
        num_scalar_prefetch=4,
        grid=(n_blocks,),
        in_specs=[pl.BlockSpec(memory_space=pl.ANY), w_spec(w_gate), w_spec(w_up), w_spec(w_down)],
        out_specs=pl.BlockSpec(memory_space=pl.ANY),
        scratch_shapes=[pltpu.VMEM((N_BUF, MOE_ROWS // SUBLANES, SUBLANES, d), F32)] * 2
        + [pltpu.SemaphoreType.DMA((N_BUF,))] * 2,
    )
    return pl.pallas_call(
        functools.partial(_experts_kernel, n_tokens=n, n_blocks=n_blocks),
        grid_spec=grid_spec,
        out_shape=jax.ShapeDtypeStruct((m + N_BUF * MOE_ROWS, d), F32),
        compiler_params=_cparams(1),
        name="experts",
    )(blk_e, n_valid, src, order, h2, w_gate, w_up, w_down)


def _combine_kernel(x1_ref, y0_ref, y1_ref, route_ref, gt2_ref, g_ref, o_ref, *, final_norm):
    route = route_ref[...]
    moe = route[:, ROUTE_C1:ROUTE_C1 + 1] * y0_ref[...] + route[:, ROUTE_C2:ROUTE_C2 + 1] * y1_ref[...]
    x2 = x1_ref[...] + gt2_ref[...] * moe
    if final_norm:
        x2 = x2 * lax.rsqrt(jnp.mean(x2 * x2, axis=-1, keepdims=True) + RMS_EPS) * g_ref[...]
    o_ref[...] = x2


def _combine(x1, y, route, mod, mod_spec, g_final, tm, final_norm):
    n, d = x1.shape
    nt = n // tm
    row = lambda width: pl.BlockSpec((tm, width), lambda i: (i, 0))
    return pl.pallas_call(
        functools.partial(_combine_kernel, final_norm=final_norm),
        grid=(nt,),
        in_specs=[row(d), row(d), pl.BlockSpec((tm, d), lambda i: (nt + i, 0)), row(LANES),
                  mod_spec(5), pl.BlockSpec((1, d), lambda i: (0, 0))],
        out_specs=row(d),
        out_shape=jax.ShapeDtypeStruct((n, d), F32),
        compiler_params=_cparams(1),
        name="combine",
    )(x1, y, y, route, mod, g_final.reshape(1, d))


def _rope_tables(pos):
    half = HEAD_DIM // 2
    inv = jnp.power(ROPE_THETA, -jnp.arange(half, dtype=F32) / half)
    ang = pos.astype(F32)[:, None] * inv[None, :]
    cos = jnp.tile(jnp.cos(ang), (1, LANES // half))
    sin = jnp.tile(jnp.sin(ang), (1, LANES // half))
    return cos, sin, cos.T, sin.T


def _sample_rows(a, batch, t):
    return jnp.pad(a.astype(F32).reshape(batch, t, a.shape[-1]), ((0, 0), (0, SAMPLE_ROWS - t), (0, 0)))


def _sample_new_keys(a_t, batch, t):
    r = a_t.shape[1]
    per_b = jnp.transpose(a_t.reshape(r, batch, t), (1, 0, 2))
    return jnp.pad(per_b, ((0, 0), (0, 0), (0, LANES - t)))


def _cache_t(cache):
    depth, n_pool, page = cache.shape[:3]
    return jnp.swapaxes(cache.reshape(depth, n_pool, page, -1), 2, 3)


def kernel(x_prompt, x_sample, c_prompt, c_sample, cache_k_a, cache_v_a, cache_idx_k, cache_k_b, cache_v_b, page_table, w_ada, b_ada, g_mix, w_in, w_proj_a, w_proj_b, w_out, g_ffn, w_router_group, b_router_group, w_router_expert, b_router_expert, w_exp_gate, w_exp_up, w_exp_down, g_final):
    batch, seq, d = x_prompt.shape
    dec_batch, dec_seq, _ = x_sample.shape
    depth = w_in.shape[0]
    past_len = page_table.shape[1] * cache_k_a.shape[2]
    n_p, n_s = batch * seq, dec_batch * dec_seq
    tm_p = min(512, seq)
    tm_s = min(256, n_s)
    assert seq % tm_p == 0 and n_s % tm_s == 0

    mod_all = _ada_mod(jnp.concatenate([c_prompt, c_sample], axis=0), w_ada, b_ada)
    mod_p = mod_all[:, :batch].reshape(depth, batch, 1, 6 * d)
    mod_s = jnp.repeat(mod_all[:, batch:], dec_seq, axis=1)
    w_row, w_col = _split_w_in(w_in)
    w_pa, w_pb, w_o = w_proj_a.astype(BF16), w_proj_b.astype(BF16), w_out.astype(BF16)
    pad_r = LANES - N_GROUPS - N_EXPERTS
    w_r = jnp.pad(jnp.concatenate([w_router_group, w_router_expert], axis=2), ((0, 0), (0, 0), (0, pad_r)))
    b_r = jnp.pad(jnp.concatenate([b_router_group, b_router_expert], axis=1), ((0, 0), (0, pad_r)))
    b_r = b_r.reshape(depth, 1, LANES)
    cache_idx_t = _cache_t(cache_idx_k)
    caches_t = tuple(_cache_t(c) for c in (cache_k_a, cache_v_a, cache_k_b, cache_v_b))

    trig_p = _rope_tables(jnp.arange(seq, dtype=jnp.int32))
    trig_s = _rope_tables(jnp.tile(past_len + jnp.arange(dec_seq, dtype=jnp.int32), dec_batch))
    nq_p = seq // tm_p
    trig_spec_p = pl.BlockSpec((tm_p, LANES), lambda i: (i % nq_p, 0))
    trig_t_spec_p = pl.BlockSpec((LANES, tm_p), lambda i: (0, i % nq_p))
    trig_spec_s = pl.BlockSpec((tm_s, LANES), lambda i: (i, 0))
    trig_t_spec_s = pl.BlockSpec((LANES, tm_s), lambda i: (0, i))

    xp = x_prompt.reshape(n_p, d)
    xs = x_sample.reshape(n_s, d)
    rows_p, rows_s = [], []
    for l in range(depth):
        mod_spec_p = lambda chunk, l=l: pl.BlockSpec((None, None, 1, d), lambda i: (l, i // nq_p, 0, chunk))
        mod_spec_s = lambda chunk, l=l: pl.BlockSpec((None, tm_s, d), lambda i: (l, i, chunk))
        last = l == depth - 1

        qa, qi, wi, qb, ga, gb, ka_t, va_t, ki_t, kb_t, vb_t = _inproj(
            xp, l, mod_p, mod_spec_p, g_mix, w_row, w_col, trig_p, trig_spec_p, trig_t_spec_p, batch, tm_p, BF16)
        rows_p.append((ka_t, va_t, ki_t, kb_t, vb_t))
        ya = _dsa_prompt(qi, wi, qa, ka_t, va_t, min(PROMPT_TQ, seq))
        yb = _moba_prompt(qb, kb_t, vb_t, min(PROMPT_TQ, seq))
        x1, h2, route = _mixout(ya, yb, ga, gb, xp, l, mod_p, mod_spec_p, g_ffn, w_pa, w_pb, w_o, w_r, b_r, tm_p)
        y = _experts(h2, route[:, ROUTE_E1:ROUTE_E2 + 1].astype(jnp.int32), l, w_exp_gate, w_exp_up, w_exp_down)
        xp = _combine(x1, y, route, mod_p, mod_spec_p, g_final, tm_p, last)

        qa, qi, wi, qb, ga, gb, ka_t, va_t, ki_t, kb_t, vb_t = _inproj(
            xs, l, mod_s, mod_spec_s, g_mix, w_row, w_col, trig_s, trig_spec_s, trig_t_spec_s, 1, tm_s, F32)
        rows_s.append((ka_t, va_t, ki_t, kb_t, vb_t))
        rows8 = lambda a: _sample_rows(a, dec_batch, dec_seq)
        new = lambda a_t: _sample_new_keys(a_t, dec_batch, dec_seq)
        score = _score_sample(rows8(qi), rows8(wi), new(ki_t), cache_idx_t, page_table, l)
        s_len = score.shape[-1]
        bias = _select_sample(score.reshape(dec_batch * SAMPLE_ROWS, s_len),
                              min(IDX_TOPK_MAX, (past_len + dec_seq) // 4), dec_seq)
        ya, yb = _attend_sample(rows8(qa), bias.reshape(dec_batch, SAMPLE_ROWS, s_len), new(ka_t), new(va_t),
                                rows8(qb), new(kb_t), new(vb_t), caches_t, page_table, l, dec_seq)
        unpad = lambda a: a[:, :dec_seq].reshape(n_s, a.shape[-1]).astype(BF16)
        x1, h2, route = _mixout(unpad(ya), unpad(yb), ga, gb, xs, l, mod_s, mod_spec_s, g_ffn, w_pa, w_pb, w_o,
                                w_r, b_r, tm_s)
        y = _experts(h2, route[:, ROUTE_E1:ROUTE_E2 + 1].astype(jnp.int32), l, w_exp_gate, w_exp_up, w_exp_down)
        xs = _combine(x1, y, route, mod_s, mod_spec_s, g_final, tm_s, last)

    def stack_p(i, heads):
        a = jnp.stack([r[i] for r in rows_p], axis=0)
        a = jnp.swapaxes(a, 2, 3)
        return a.reshape((depth, batch, seq, heads, HEAD_DIM) if heads else (depth, batch, seq, D_IDX))

    def stack_s(i, heads):
        a = jnp.stack([r[i][0] for r in rows_s], axis=0)
        a = jnp.swapaxes(a, 1, 2)
        return a.reshape((depth, dec_batch, dec_seq, heads, HEAD_DIM) if heads else (depth, dec_batch, dec_seq, D_IDX))

    return (xp.reshape(batch, seq, d), xs.reshape(dec_batch, dec_seq, d),
            stack_p(0, N_KV_A), stack_p(1, N_KV_A), stack_p(3, N_KV_B), stack_p(4, N_KV_B), stack_p(2, 0),
            stack_s(0, N_KV_A), stack_s(1, N_KV_A), stack_s(3, N_KV_B), stack_s(4, N_KV_B), stack_s(2, 0))
```

```python
import functools

import jax
import jax.numpy as jnp
from jax import lax
from jax.experimental import pallas as pl
from jax.experimental.pallas import tpu as pltpu

HEAD_DIM = 64
N_HEADS_A = 8
N_KV_A = 2
N_IDX_HEADS = 4
D_IDX = 64
IDX_TOPK_MAX = 256
N_HEADS_B = 8
N_KV_B = 2
MOBA_BLOCK = 256
MOBA_TOPK_MAX = 3
N_GROUPS = 4
EXPERTS_PER_GROUP = 8
N_EXPERTS = N_GROUPS * EXPERTS_PER_GROUP
TOPK_IN_GROUP = 2
ROPE_THETA = 10000.0
RMS_EPS = 1e-6

LANES = 128
KV_ROWS = N_KV_A * HEAD_DIM
assert KV_ROWS == LANES == N_KV_B * HEAD_DIM and D_IDX == HEAD_DIM and TOPK_IN_GROUP == 2

VMEM_LIMIT = 56 * 1024 * 1024

F32 = jnp.float32
BF16 = jnp.bfloat16
NEG_INF = float("-inf")
INT_MIN = -(2 ** 31)
NEG_INF_KEY = -2139095041
HIGHEST = lax.Precision.HIGHEST
SCORE_SCALE = HEAD_DIM ** -0.5 * 1.4426950408889634


def _cparams(n_grid):
    return pltpu.CompilerParams(dimension_semantics=("arbitrary",) * n_grid, vmem_limit_bytes=VMEM_LIMIT)


def _dot(a, b, precision=None):
    return jnp.dot(a, b, preferred_element_type=F32, precision=precision)


def _dot_nt(a, b, precision=None):
    return lax.dot_general(a, b, (((1,), (1,)), ((), ())), preferred_element_type=F32, precision=precision)


def _sigmoid(x):
    return 1.0 / (1.0 + jnp.exp(-x))


def _ada_kernel(c_ref, w_ref, b_ref, o_ref):
    c = c_ref[...]
    o_ref[...] = _dot((c * _sigmoid(c)).astype(BF16), w_ref[...].astype(BF16)) + b_ref[...]


def _ada_mod(c_all, w_ada, b_ada):
    depth, d, n6 = w_ada.shape
    m = c_all.shape[0]
    tn = 1024
    return pl.pallas_call(
        _ada_kernel,
        grid=(depth, n6 // tn),
        in_specs=[
            pl.BlockSpec((m, d), lambda l, j: (0, 0)),
            pl.BlockSpec((None, d, tn), lambda l, j: (l, 0, j)),
            pl.BlockSpec((None, 1, tn), lambda l, j: (l, 0, j)),
        ],
        out_specs=pl.BlockSpec((None, m, tn), lambda l, j: (l, 0, j)),
        out_shape=jax.ShapeDtypeStruct((depth, m, n6), F32),
        compiler_params=_cparams(2),
        name="ada_mod",
    )(c_all, w_ada, b_ada.reshape(depth, 1, n6))


_QA_W = N_HEADS_A * HEAD_DIM
_QI_W = N_IDX_HEADS * D_IDX
_QB_W = N_HEADS_B * HEAD_DIM
_ROW_QA, _ROW_QI, _ROW_WI, _ROW_QB = 0, _QA_W, _QA_W + _QI_W, _QA_W + _QI_W + LANES
_ROW_GATE = _ROW_QB + _QB_W
_COL_KA, _COL_VA, _COL_KI, _COL_KB, _COL_VB = 0, KV_ROWS, 2 * KV_ROWS, 2 * KV_ROWS + D_IDX, 3 * KV_ROWS + D_IDX
_COL_END = 4 * KV_ROWS + D_IDX


def _rope_lanes(y, cos, sin, lane):
    first_half = (lane % HEAD_DIM) < (HEAD_DIM // 2)
    rot = jnp.where(first_half, -pltpu.roll(y, LANES - HEAD_DIM // 2, 1), pltpu.roll(y, HEAD_DIM // 2, 1))
    return y * cos + rot * sin


def _rope_rows(y, cos, sin):
    half = HEAD_DIM // 2
    parts = []
    for r0 in range(0, y.shape[0], HEAD_DIM):
        parts += [-y[r0 + half:r0 + HEAD_DIM], y[r0:r0 + half]]
    rot = jnp.concatenate(parts, axis=0)
    return y * cos[:y.shape[0]] + rot * sin[:y.shape[0]]


def _inproj_kernel(x_ref, sc_ref, sh_ref, g_ref, wr_ref, wc_ref, cos_ref, sin_ref, cos_t_ref, sin_t_ref,
                   qa_ref, qi_ref, wi_ref, qb_ref, ga_ref, gb_ref, ka_ref, va_ref, ki_ref, kb_ref, vb_ref,
                   *, d_model):
    x = x_ref[...]
    h = x * lax.rsqrt(jnp.mean(x * x, axis=-1, keepdims=True) + RMS_EPS) * g_ref[...]
    h = h * (1.0 + sc_ref[...]) + sh_ref[...]
    hb = h.astype(BF16)
    cos = cos_ref[...]
    sin = sin_ref[...]
    lane = lax.broadcasted_iota(jnp.int32, cos.shape, 1)

    def proj(c0, width):
        return _dot(hb, wr_ref[:, c0:c0 + width])

    def roped(c0, out_ref, scale=None):
        for c in range(out_ref.shape[1] // LANES):
            y = _rope_lanes(proj(c0 + c * LANES, LANES), cos, sin, lane)
            if scale is not None:
                y = y * scale
            out_ref[:, c * LANES:(c + 1) * LANES] = y.astype(out_ref.dtype)

    roped(_ROW_QA, qa_ref, SCORE_SCALE)
    roped(_ROW_QI, qi_ref)
    wik = proj(_ROW_WI, LANES)
    wi_ref[...] = jnp.where(lane >= D_IDX, _rope_lanes(wik, cos, sin, lane), wik)
    roped(_ROW_QB, qb_ref)
    for c in range(d_model // 512):
        ga_ref[:, c * 512:(c + 1) * 512] = _sigmoid(proj(_ROW_GATE + c * 512, 512)).astype(ga_ref.dtype)
        gb_ref[:, c * 512:(c + 1) * 512] = _sigmoid(proj(_ROW_GATE + d_model + c * 512, 512)).astype(gb_ref.dtype)

    cos_t = cos_t_ref[...]
    sin_t = sin_t_ref[...]

    def proj_t(r0, r1):
        return _dot_nt(wc_ref[r0:r1, :], hb)

    ka_ref[...] = _rope_rows(proj_t(_COL_KA, _COL_VA), cos_t, sin_t)
    va_ref[...] = proj_t(_COL_VA, _COL_KI)
    ki_ref[...] = _rope_rows(proj_t(_COL_KI, _COL_KB), cos_t, sin_t)
    kb_ref[...] = _rope_rows(proj_t(_COL_KB, _COL_VB), cos_t, sin_t)
    vb_ref[...] = proj_t(_COL_VB, _COL_END)


def _split_w_in(w_in):
    depth, d, d_in = w_in.shape
    widths = (_QA_W, KV_ROWS, KV_ROWS, _QI_W, D_IDX, N_IDX_HEADS, _QB_W, KV_ROWS, KV_ROWS, d, d)
    offs = [0]
    for w in widths:
        offs.append(offs[-1] + w)
    assert offs[-1] == d_in
    w_t = jnp.transpose(w_in, (2, 0, 1))
    qa, ka, va, qi, ki, wi, qb, kb, vb, ga, gb = [w_t[offs[i]:offs[i + 1]] for i in range(11)]
    pad = jnp.zeros((LANES - N_IDX_HEADS - D_IDX, depth, d), w_in.dtype)
    w_row = jnp.transpose(jnp.concatenate([qa, qi, wi, pad, ki, qb, ga, gb], axis=0).astype(BF16), (1, 2, 0))
    w_col = jnp.transpose(jnp.concatenate([ka, va, ki, kb, vb], axis=0).astype(BF16), (1, 0, 2))
    return w_row, w_col


def _inproj(x, layer, mod, mod_spec, g_mix, w_row, w_col, trig, trig_spec, trig_t_spec, n_batch, tm, q_dtype):
    n, d = x.shape
    per_batch = n // n_batch
    nq = per_batch // tm
    cos, sin, cos_t, sin_t = trig
    row = lambda width: pl.BlockSpec((tm, width), lambda i: (i, 0))
    col = lambda rows: pl.BlockSpec((None, rows, tm), lambda i: (i // nq, 0, i % nq))
    row_out = [(_QA_W, q_dtype), (_QI_W, q_dtype), (LANES, F32), (_QB_W, F32), (d, q_dtype), (d, q_dtype)]
    col_out = [KV_ROWS, KV_ROWS, D_IDX, KV_ROWS, KV_ROWS]
    layer_w = lambda a: pl.BlockSpec((None,) + a.shape[1:], lambda i: (layer, 0, 0))
    return pl.pallas_call(
        functools.partial(_inproj_kernel, d_model=d),
        grid=(n // tm,),
        in_specs=[row(d), mod_spec(1), mod_spec(0), pl.BlockSpec((None, 1, d), lambda i: (layer, 0, 0)),
                  layer_w(w_row), layer_w(w_col), trig_spec, trig_spec, trig_t_spec, trig_t_spec],
        out_specs=[row(w) for w, _ in row_out] + [col(r) for r in col_out],
        out_shape=[jax.ShapeDtypeStruct((n, w), dt) for w, dt in row_out]
        + [jax.ShapeDtypeStruct((n_batch, r, per_batch), F32) for r in col_out],
        compiler_params=_cparams(1),
        name="inproj",
    )(x, mod, mod, g_mix.reshape(g_mix.shape[0], 1, d), w_row, w_col, cos, sin, cos_t, sin_t)


def _stack_heads(q, first_head, n_heads, scale=None):
    q = q.astype(F32)
    if scale is not None:
        q = q * scale
    parts = [q[:, (first_head + g) * HEAD_DIM:(first_head + g + 1) * HEAD_DIM] for g in range(n_heads)]
    return jnp.concatenate(parts, axis=0).astype(BF16)


def _sort_key(x):
    bits = lax.bitcast_convert_type(x, jnp.int32)
    return bits ^ ((bits >> 31) & 0x7FFFFFFF)


def _count(mask):
    return jnp.sum(jnp.where(mask, 1.0, 0.0), axis=1, keepdims=True)


def _indexer_scores(qi, wi, ki_t, q_pos):
    t = qi.shape[0]
    logits = _dot(_stack_heads(qi, 0, N_IDX_HEADS), ki_t)
    score = None
    for h in range(N_IDX_HEADS):
        term = jnp.maximum(logits[h * t:(h + 1) * t], 0.0) * wi[:, h:h + 1]
        score = term if score is None else score + term
    k_pos = lax.broadcasted_iota(jnp.int32, score.shape, 1)
    score = jnp.where(score == 0.0, 0.0, score)
    return jnp.where(k_pos <= q_pos, score, NEG_INF)


def _topk_mask(score, n_keep, row_is_real, bias_ref):
    t, s_len = score.shape
    key = _sort_key(score)
    keep = float(n_keep)
    thr = jnp.where(_count(key >= 0) >= keep, 0, INT_MIN).astype(jnp.int32)

    def two_bits(i, thr):
        hi = jnp.left_shift(jnp.int32(1), 30 - 2 * i)
        lo = jnp.left_shift(jnp.int32(1), 29 - 2 * i)
        c_hi, c_lo, c_both = thr + hi, thr + lo, thr + hi + lo
        n_hi, n_lo, n_both = _count(key >= c_hi), _count(key >= c_lo), _count(key >= c_both)
        return jnp.where(n_both >= keep, c_both, jnp.where(n_hi >= keep, c_hi, jnp.where(n_lo >= keep, c_lo, thr)))

    thr = lax.fori_loop(0, 15, two_bits, thr)
    thr = jnp.where(_count(key >= thr + 1) >= keep, thr + 1, thr)
    at_least = key >= thr
    n_above = _count(key > thr)
    bias_ref[...] = jnp.where(at_least & (key > NEG_INF_KEY), 0.0, NEG_INF)

    tied = (_count(at_least) > keep) & (thr > NEG_INF_KEY) & row_is_real

    @pl.when(jnp.max(jnp.where(tied, 1.0, 0.0)) > 0.0)
    def _():
        free = keep - n_above
        r_i = lax.broadcasted_iota(jnp.int32, (MOBA_BLOCK, MOBA_BLOCK), 0)
        c_i = lax.broadcasted_iota(jnp.int32, (MOBA_BLOCK, MOBA_BLOCK), 1)
        tri = jnp.where(r_i <= c_i, 1.0, 0.0).astype(BF16)
        seen = jnp.zeros((t, 1), F32)
        for c0 in range(0, s_len, MOBA_BLOCK):
            w = min(MOBA_BLOCK, s_len - c0)
            key_c = key[:, c0:c0 + w]
            eq_c = jnp.where(key_c == thr, 1.0, 0.0)
            rank = _dot(eq_c.astype(BF16), tri[:w, :w]) + seen
            keep_c = ((key_c > thr) | ((key_c == thr) & (rank <= free))) & (key_c > NEG_INF_KEY)
            bias_ref[:, c0:c0 + w] = jnp.where(keep_c, 0.0, NEG_INF)
            seen = seen + jnp.sum(eq_c, axis=1, keepdims=True)


def _indexer_scores_t(kik, kiw_q, qi, q_pos_t):
    t = qi.shape[0]
    q = qi.astype(F32)
    lane = lax.broadcasted_iota(jnp.int32, (t, LANES), 1)
    parts = []
    for h in range(N_IDX_HEADS):
        pair = q[:, (h // 2) * LANES:(h // 2 + 1) * LANES]
        if h % 2:
            parts.append(jnp.where(lane >= D_IDX, pair, 0.0))
        else:
            parts.append(pltpu.roll(jnp.where(lane < D_IDX, pair, 0.0), D_IDX, 1))
    logits = _dot_nt(kik, jnp.concatenate(parts, axis=0).astype(BF16))
    wi_t = kiw_q.T
    score = None
    for h in range(N_IDX_HEADS):
        term = jnp.maximum(logits[:, h * t:(h + 1) * t], 0.0) * wi_t[h:h + 1, :]
        score = term if score is None else score + term
    k_pos = lax.broadcasted_iota(jnp.int32, score.shape, 0)
    score = jnp.where(score == 0.0, 0.0, score)
    return jnp.where(k_pos <= q_pos_t, score, NEG_INF)


COUNT_SLAB = 64


def _count_t(mask):
    ones = jnp.where(mask, 1.0, 0.0)
    s_len, t = ones.shape
    if s_len % COUNT_SLAB == 0 and s_len > COUNT_SLAB:
        ones = jnp.sum(ones.reshape(s_len // COUNT_SLAB, COUNT_SLAB, t), axis=0)
    return jnp.sum(ones, axis=0, keepdims=True)


def _store_transposed(mask_t, c0, bias_ref):
    for j in range(0, mask_t.shape[0], LANES):
        bias_ref[:, c0 + j:c0 + j + LANES] = mask_t[j:j + LANES, :].T


def _topk_mask_t(score_t, n_keep, bias_ref):
    s_len, t = score_t.shape
    key = _sort_key(score_t)
    keep = float(n_keep)
    thr = jnp.where(_count_t(key >= 0) >= keep, 0, INT_MIN).astype(jnp.int32)

    def bit_step(i, thr):
        cand = thr + jnp.left_shift(jnp.int32(1), 30 - i)
        return jnp.where(_count_t(key >= cand) >= keep, cand, thr)

    thr = lax.fori_loop(0, 31, bit_step, thr)
    at_least = key >= thr
    n_above = _count_t(key > thr)
    _store_transposed(jnp.where(at_least & (key > NEG_INF_KEY), 0.0, NEG_INF), 0, bias_ref)

    tied = (_count_t(at_least) > keep) & (thr > NEG_INF_KEY)

    @pl.when(jnp.max(jnp.where(tied, 1.0, 0.0)) > 0.0)
    def _():
        free = keep - n_above
        r_i = lax.broadcasted_iota(jnp.int32, (MOBA_BLOCK, MOBA_BLOCK), 0)
        c_i = lax.broadcasted_iota(jnp.int32, (MOBA_BLOCK, MOBA_BLOCK), 1)
        tri = jnp.where(c_i <= r_i, 1.0, 0.0).astype(BF16)
        seen = jnp.zeros((1, t), F32)
        for c0 in range(0, s_len, MOBA_BLOCK):
            w = min(MOBA_BLOCK, s_len - c0)
            key_c = key[c0:c0 + w, :]
            eq_c = jnp.where(key_c == thr, 1.0, 0.0)
            rank = _dot(tri[:w, :w], eq_c.astype(BF16)) + seen
            keep_c = ((key_c > thr) | ((key_c == thr) & (rank <= free))) & (key_c > NEG_INF_KEY)
            _store_transposed(jnp.where(keep_c, 0.0, NEG_INF), c0, bias_ref)
            seen = seen + jnp.sum(eq_c, axis=0, keepdims=True)


ONES_ROWS = 16


def _with_ones(v_t):
    return jnp.concatenate([v_t, jnp.ones((ONES_ROWS, v_t.shape[1]), BF16)], axis=0)


def _normalised_pv(p, v_t):
    o = _dot_nt(p, _with_ones(v_t))
    return o[:, :HEAD_DIM] / o[:, HEAD_DIM:HEAD_DIM + 1]


def _softmax_pv(s, v_t):
    m = jnp.max(s, axis=1, keepdims=True)
    return _normalised_pv(jnp.exp2(s - m).astype(BF16), v_t)


def _attend_group(q_g, k_t, v_t, bias):
    s = _dot(q_g, k_t)
    g = s.shape[0] // bias.shape[0]
    s = (s.reshape(g, bias.shape[0], s.shape[1]) + bias[None]).reshape(s.shape)
    return _softmax_pv(s, v_t)


def _store_heads(o_g, t, first_head, n_heads, out_ref):
    for g in range(0, n_heads, 2):
        pair = jnp.concatenate([o_g[g * t:(g + 1) * t], o_g[(g + 1) * t:(g + 2) * t]], axis=1)
        c0 = (first_head + g) * HEAD_DIM
        out_ref[:, c0:c0 + 2 * HEAD_DIM] = pair.astype(out_ref.dtype)


def _dsa_attention(qa, ka_t, va_t, bias, out_ref):
    t = qa.shape[0]
    hpk = N_HEADS_A // N_KV_A
    for kv in range(N_KV_A):
        k_t = ka_t[kv * HEAD_DIM:(kv + 1) * HEAD_DIM].astype(BF16)
        v_t = va_t[kv * HEAD_DIM:(kv + 1) * HEAD_DIM].astype(BF16)
        o_g = _attend_group(_stack_heads(qa, kv * hpk, hpk), k_t, v_t, bias)
        _store_heads(o_g, t, kv * hpk, hpk, out_ref)


GATE_ROWS = 8


def _block_means(kb_t, n_full):
    lane = lax.broadcasted_iota(jnp.int32, (KV_ROWS, LANES), 1)
    km = jnp.zeros((KV_ROWS, LANES), F32)
    for n in range(n_full):
        col = jnp.sum(kb_t[:, n * MOBA_BLOCK:(n + 1) * MOBA_BLOCK], axis=1, keepdims=True) * (1.0 / MOBA_BLOCK)
        km = jnp.where(lane == n, col, km)
    return km.T


def _moba_queries(qb, kv):
    hpk = N_HEADS_B // N_KV_B
    return jnp.concatenate([qb[:, (kv * hpk + g) * HEAD_DIM:(kv * hpk + g + 1) * HEAD_DIM] for g in range(hpk)],
                           axis=0)


def _moba_pick(q_f, k_mean_kv, own, n_sel, n_full):
    assert n_full <= GATE_ROWS
    rows = q_f.shape[0]
    rows_pad = -(-rows // LANES) * LANES
    if rows_pad != rows:
        q_f = jnp.concatenate([q_f, jnp.zeros((rows_pad - rows, HEAD_DIM), F32)], axis=0)
    blk = lax.broadcasted_iota(jnp.int32, (GATE_ROWS, rows_pad), 0)
    blk_f = blk.astype(F32)
    gate = _dot_nt(k_mean_kv, q_f, precision=HIGHEST)
    gate = jnp.where(blk < jnp.minimum(own, n_full), gate, NEG_INF)
    picked_t = jnp.zeros((GATE_ROWS, rows_pad), F32)
    for _ in range(n_sel):
        best = jnp.max(gate, axis=0, keepdims=True)
        first = jnp.min(jnp.where(gate == best, blk_f, float(GATE_ROWS)), axis=0, keepdims=True)
        hit = (blk_f == first) & (best > NEG_INF)
        picked_t = jnp.where(hit, 1.0, picked_t)
        gate = jnp.where(blk_f == first, NEG_INF, gate)
    picked_t = jnp.concatenate([picked_t, jnp.zeros((LANES - GATE_ROWS, rows_pad), F32)], axis=0)
    return picked_t.T[:rows]


def _moba_scores(q_g, k_t, picked, q_pos_g, own, own_min, n_full, s_ref):
    rows, s_len = q_g.shape[0], k_t.shape[1]
    for c0 in range(0, s_len, MOBA_BLOCK):
        c = c0 // MOBA_BLOCK
        w = min(MOBA_BLOCK, s_len - c0)
        if c < own_min:
            allowed = picked[:, c:c + 1] > 0.0
        else:
            k_pos = c0 + lax.broadcasted_iota(jnp.int32, (rows, w), 1)
            allowed = (k_pos <= q_pos_g) & (own == c)
            if c < n_full:
                allowed = allowed | (picked[:, c:c + 1] > 0.0)
        s_ref[:, c0:c0 + w] = jnp.where(allowed, _dot(q_g, k_t[:, c0:c0 + w]), NEG_INF)


def _moba_attention(qb, kb_t, vb_t, k_mean, q_pos, own, own_min, n_sel, n_full, s_ref, out_ref):
    t = qb.shape[0]
    s_len = kb_t.shape[1]
    hpk = N_HEADS_B // N_KV_B
    q_pos_g = jnp.concatenate([q_pos] * hpk, axis=0)
    for kv in range(N_KV_B):
        q_f = _moba_queries(qb, kv)
        picked = _moba_pick(q_f, k_mean[:GATE_ROWS, kv * HEAD_DIM:(kv + 1) * HEAD_DIM], own, n_sel, n_full)
        k_t = kb_t[kv * HEAD_DIM:(kv + 1) * HEAD_DIM].astype(BF16)
        v_t = vb_t[kv * HEAD_DIM:(kv + 1) * HEAD_DIM].astype(BF16)
        _moba_scores((q_f * SCORE_SCALE).astype(BF16), k_t, picked, q_pos_g, own, own_min, n_full,
                     s_ref.at[:, :s_len])
        _store_heads(_softmax_pv(s_ref[:, :s_len], v_t), t, kv * hpk, hpk, out_ref)


PROMPT_TQ = 128
KEY_SEGMENT = MOBA_BLOCK


def _per_key_segment(seq, tq, body):
    n_seg = max(1, seq // KEY_SEGMENT)
    seg_len = seq // n_seg
    assert seg_len % tq == 0 and seg_len % MOBA_BLOCK == 0
    seg = (pl.program_id(1) * tq) // seg_len
    for k in range(n_seg):
        pl.when(seg == k)(functools.partial(body, (k + 1) * seg_len, seg_len))


def _dsa_prompt_kernel(qi_ref, wik_q_ref, qa_ref, wik_k_ref, ka_ref, va_ref, o_ref, bias_ref, kik_ref,
                       *, tq, seq, n_keep):
    q_pos_t = pl.program_id(1) * tq + lax.broadcasted_iota(jnp.int32, (1, tq), 1)

    @pl.when(pl.program_id(1) == 0)
    def _():
        kik_ref[...] = wik_k_ref[...].astype(BF16)

    def body(s_len, seg_len):
        if s_len <= n_keep:
            q_pos = pl.program_id(1) * tq + lax.broadcasted_iota(jnp.int32, (tq, 1), 0)
            k_pos = lax.broadcasted_iota(jnp.int32, (tq, s_len), 1)
            bias_ref[:, :s_len] = jnp.where(k_pos <= q_pos, 0.0, NEG_INF)
        else:
            score_t = _indexer_scores_t(kik_ref[:s_len, :], wik_q_ref[...], qi_ref[...], q_pos_t)
            _topk_mask_t(score_t, n_keep, bias_ref.at[:, :s_len])
        _dsa_attention(qa_ref[...], ka_ref[:, :s_len], va_ref[:, :s_len], bias_ref[:, :s_len], o_ref)

    _per_key_segment(seq, tq, body)


def _dsa_prompt(qi, wik, qa, ka_t, va_t, tq):
    batch, _, seq = ka_t.shape
    assert tq == LANES
    nq = seq // tq
    qrow = lambda w: pl.BlockSpec((tq, w), lambda b, i: (b * nq + i, 0))
    keys_t = pl.BlockSpec((None, KV_ROWS, seq), lambda b, i: (b, 0, 0))
    return pl.pallas_call(
        functools.partial(_dsa_prompt_kernel, tq=tq, seq=seq, n_keep=min(IDX_TOPK_MAX, seq // 4)),
        grid=(batch, nq),
        in_specs=[qrow(qi.shape[1]), qrow(LANES), qrow(qa.shape[1]),
                  pl.BlockSpec((seq, LANES), lambda b, i: (b, 0)), keys_t, keys_t],
        out_specs=qrow(qa.shape[1]),
        out_shape=jax.ShapeDtypeStruct(qa.shape, BF16),
        scratch_shapes=[pltpu.VMEM((tq, seq), F32), pltpu.VMEM((seq, LANES), BF16)],
        compiler_params=_cparams(2),
        name="dsa_prompt",
    )(qi, wik, qa, wik, ka_t, va_t)


def _moba_prompt_kernel(qb_ref, kb_ref, vb_ref, o_ref, bias_ref, km_ref, *, tq, seq, n_sel):
    base = pl.program_id(1) * tq
    q_pos = base + lax.broadcasted_iota(jnp.int32, (tq, 1), 0)
    n_full = seq // MOBA_BLOCK

    @pl.when(pl.program_id(1) == 0)
    def _():
        km_ref[...] = _block_means(kb_ref[...], n_full)

    def body(s_len, seg_len):
        _moba_attention(qb_ref[...], kb_ref[:, :s_len], vb_ref[:, :s_len], km_ref[...], q_pos, base // MOBA_BLOCK,
                        (s_len - seg_len) // MOBA_BLOCK, n_sel, n_full, bias_ref, o_ref)

    _per_key_segment(seq, tq, body)


def _moba_prompt(qb, kb_t, vb_t, tq):
    batch, _, seq = kb_t.shape
    assert MOBA_BLOCK % tq == 0 and seq % MOBA_BLOCK == 0
    nq = seq // tq
    qrow = pl.BlockSpec((tq, qb.shape[1]), lambda b, i: (b * nq + i, 0))
    keys = pl.BlockSpec((None, KV_ROWS, seq), lambda b, i: (b, 0, 0))
    return pl.pallas_call(
        functools.partial(_moba_prompt_kernel, tq=tq, seq=seq, n_sel=min(MOBA_TOPK_MAX, seq // MOBA_BLOCK)),
        grid=(batch, nq),
        in_specs=[qrow, keys, keys],
        out_specs=qrow,
        out_shape=jax.ShapeDtypeStruct(qb.shape, BF16),
        scratch_shapes=[pltpu.VMEM((N_HEADS_B // N_KV_B * tq, seq), F32), pltpu.VMEM((KV_ROWS, LANES), F32)],
        compiler_params=_cparams(2),
        name="moba_prompt",
    )(qb, kb_t, vb_t)


SAMPLE_ROWS = 8


def _assemble_keys(page_refs, new_ref, dst_ref, page_size):
    for j, page in enumerate(page_refs):
        dst_ref[:, j * page_size:(j + 1) * page_size] = page[...]
    dst_ref[:, len(page_refs) * page_size:] = new_ref[...]


def _page_specs(layer, n_pages, rows, page_size):
    return [pl.BlockSpec((None, None, rows, page_size), lambda b, pt, j=j: (layer, pt[b, j], 0, 0))
            for j in range(n_pages)]


def _score_sample_kernel(pt_ref, qi_ref, wi_ref, kin_ref, *rest, n_pages, page_size):
    ki_pages = rest[:n_pages]
    o_ref, ki_s = rest[n_pages:]
    _assemble_keys(ki_pages, kin_ref, ki_s, page_size)
    q_pos = n_pages * page_size + lax.broadcasted_iota(jnp.int32, (SAMPLE_ROWS, 1), 0)
    o_ref[...] = _indexer_scores(qi_ref[...], wi_ref[...], ki_s[...].astype(BF16), q_pos)


def _score_sample(qi, wi, ki_new, cache_idx_t, page_table, layer):
    b, n_pages = page_table.shape
    page_size = cache_idx_t.shape[3]
    s_len = n_pages * page_size + LANES
    per_b = lambda r, w: pl.BlockSpec((None, r, w), lambda i, pt: (i, 0, 0))
    grid_spec = pltpu.PrefetchScalarGridSpec(
        num_scalar_prefetch=1,
        grid=(b,),
        in_specs=[per_b(SAMPLE_ROWS, qi.shape[2]), per_b(SAMPLE_ROWS, LANES), per_b(D_IDX, LANES)]
        + _page_specs(layer, n_pages, D_IDX, page_size),
        out_specs=per_b(SAMPLE_ROWS, s_len),
        scratch_shapes=[pltpu.VMEM((D_IDX, s_len), F32)],
    )
    return pl.pallas_call(
        functools.partial(_score_sample_kernel, n_pages=n_pages, page_size=page_size),
        grid_spec=grid_spec,
        out_shape=jax.ShapeDtypeStruct((b, SAMPLE_ROWS, s_len), F32),
        compiler_params=_cparams(1),
        name="score_sample",
    )(page_table, qi, wi, ki_new, *([cache_idx_t] * n_pages))


def _select_kernel(score_ref, bias_ref, *, n_keep, dec_seq):
    rows = score_ref.shape[0]
    row = lax.broadcasted_iota(jnp.int32, (rows, 1), 0)
    _topk_mask(score_ref[...], n_keep, (row % SAMPLE_ROWS) < dec_seq, bias_ref)


def _select_sample(score, n_keep, dec_seq):
    n, s_len = score.shape
    tr = min(256, n)
    assert n % tr == 0 and tr % SAMPLE_ROWS == 0
    spec = pl.BlockSpec((tr, s_len), lambda i: (i, 0))
    return pl.pallas_call(
        functools.partial(_select_kernel, n_keep=n_keep, dec_seq=dec_seq),
        grid=(n // tr,),
        in_specs=[spec],
        out_specs=spec,
        out_shape=jax.ShapeDtypeStruct((n, s_len), F32),
        compiler_params=_cparams(1),
        name="select_sample",
    )(score)


SAMPLE_GROUP = 1


def _attend_sample_kernel(pt_ref, qa_ref, bias_ref, kan_ref, van_ref, qb_ref, kbn_ref, vbn_ref, *rest,
                          n_pages, page_size, n_sel):
    n_page_refs = 4 * SAMPLE_GROUP * n_pages
    ya_ref, yb_ref = rest[n_page_refs:n_page_refs + 2]
    scratch = rest[n_page_refs + 2:]
    past = n_pages * page_size
    n_full = past // MOBA_BLOCK
    hpk = N_HEADS_A // N_KV_A
    rows = hpk * SAMPLE_ROWS
    q_pos_g = past + lax.broadcasted_iota(jnp.int32, (rows, 1), 0) % SAMPLE_ROWS
    for g in range(SAMPLE_GROUP):
        ka_s, va_s, kb_s, vb_s, s_ref = scratch[5 * g:5 * g + 5]
        for k, (new_ref, dst) in enumerate(zip((kan_ref, van_ref, kbn_ref, vbn_ref), (ka_s, va_s, kb_s, vb_s))):
            first = (k * SAMPLE_GROUP + g) * n_pages
            _assemble_keys(rest[first:first + n_pages], new_ref.at[g], dst, page_size)
        qa, qb = qa_ref[g], qb_ref[g]
        bias_g = jnp.concatenate([bias_ref[g]] * hpk, axis=0)
        kb_t = kb_s[...]
        k_mean = _block_means(kb_t, n_full)
        for kv in range(N_KV_A):
            k_t = ka_s[kv * HEAD_DIM:(kv + 1) * HEAD_DIM, :].astype(BF16)
            s_ref[kv * rows:(kv + 1) * rows, :] = _dot(_stack_heads(qa, kv * hpk, hpk), k_t) + bias_g
            q_f = _moba_queries(qb, kv)
            picked = _moba_pick(q_f, k_mean[:GATE_ROWS, kv * HEAD_DIM:(kv + 1) * HEAD_DIM], n_full, n_sel, n_full)
            _moba_scores((q_f * SCORE_SCALE).astype(BF16), kb_t[kv * HEAD_DIM:(kv + 1) * HEAD_DIM].astype(BF16),
                         picked, q_pos_g, n_full, n_full, n_full,
                         s_ref.at[(N_KV_A + kv) * rows:(N_KV_A + kv + 1) * rows, :])
        s = s_ref[...]
        p = jnp.exp2(s - jnp.max(s, axis=1, keepdims=True))
        l = jnp.sum(p, axis=1, keepdims=True)
        p = p.astype(BF16)
        groups = ((va_s, ya_ref, 0), (va_s, ya_ref, 1), (vb_s, yb_ref, 0), (vb_s, yb_ref, 1))
        for idx, (v_s, out_ref, kv) in enumerate(groups):
            r0 = idx * rows
            v_t = v_s[kv * HEAD_DIM:(kv + 1) * HEAD_DIM, :].astype(BF16)
            o_g = _dot_nt(p[r0:r0 + rows], v_t) / l[r0:r0 + rows]
            _store_heads(o_g, SAMPLE_ROWS, kv * hpk, hpk, out_ref.at[g])


def _attend_sample(qa, bias, ka_new, va_new, qb, kb_new, vb_new, caches_t, page_table, layer, dec_seq):
    b, n_pages = page_table.shape
    page_size = caches_t[0].shape[3]
    past = n_pages * page_size
    assert past % MOBA_BLOCK == 0 and dec_seq <= SAMPLE_ROWS and b % SAMPLE_GROUP == 0
    s_len = past + LANES
    n_blocks = -(-(past + dec_seq) // MOBA_BLOCK)
    per_step = lambda r, w: pl.BlockSpec((SAMPLE_GROUP, r, w), lambda i, pt: (i, 0, 0))
    new = per_step(KV_ROWS, LANES)
    q = per_step(SAMPLE_ROWS, qa.shape[2])
    page_specs = [pl.BlockSpec((None, None, KV_ROWS, page_size),
                               lambda i, pt, g=g, j=j: (layer, pt[i * SAMPLE_GROUP + g, j], 0, 0))
                  for _ in range(4) for g in range(SAMPLE_GROUP) for j in range(n_pages)]
    grid_spec = pltpu.PrefetchScalarGridSpec(
        num_scalar_prefetch=1,
        grid=(b // SAMPLE_GROUP,),
        in_specs=[q, per_step(SAMPLE_ROWS, s_len), new, new, q, new, new] + page_specs,
        out_specs=[q, q],
        scratch_shapes=([pltpu.VMEM((KV_ROWS, s_len), F32)] * 4
                        + [pltpu.VMEM(((N_HEADS_A + N_HEADS_B) * SAMPLE_ROWS, s_len), F32)]) * SAMPLE_GROUP,
    )
    return pl.pallas_call(
        functools.partial(_attend_sample_kernel, n_pages=n_pages, page_size=page_size,
                          n_sel=min(MOBA_TOPK_MAX, n_blocks)),
        grid_spec=grid_spec,
        out_shape=[jax.ShapeDtypeStruct(qa.shape, F32)] * 2,
        compiler_params=_cparams(1),
        name="attend_sample",
    )(page_table, qa, bias, ka_new, va_new, qb, kb_new, vb_new,
      *[c for c in caches_t for _ in range(SAMPLE_GROUP * n_pages)])


ROUTE_E1, ROUTE_E2, ROUTE_C1, ROUTE_C2 = 0, 1, 2, 3
SUBLANES = 8


def _store_token_tiles(x, ref, lead=()):
    t = x.shape[0]
    for j in range(x.shape[1] // LANES):
        ref[lead + (pl.ds(j, t, stride=SUBLANES), slice(None))] = x[:, j * LANES:(j + 1) * LANES]


def _load_token_tiles(ref, t, lead=()):
    return jnp.concatenate([ref[lead + (pl.ds(j, t, stride=SUBLANES), slice(None))] for j in range(SUBLANES)], axis=1)


def _mixout_kernel(ya_ref, yb_ref, ga_ref, gb_ref, x_ref, gt1_ref, sc2_ref, sh2_ref, g_ref,
                   wpa_ref, wpb_ref, wout_ref, wr_ref, br_ref, x1_ref, h2_ref, route_ref):
    merged = (ga_ref[...].astype(F32) * _dot(ya_ref[...], wpa_ref[...])
              + gb_ref[...].astype(F32) * _dot(yb_ref[...], wpb_ref[...]))
    mix = _dot(merged.astype(BF16), wout_ref[...])
    x1 = x_ref[...] + gt1_ref[...] * mix
    x1_ref[...] = x1
    h2 = x1 * lax.rsqrt(jnp.mean(x1 * x1, axis=-1, keepdims=True) + RMS_EPS) * g_ref[...]
    h2 = h2 * (1.0 + sc2_ref[...]) + sh2_ref[...]
    _store_token_tiles(h2, h2_ref)

    logits = _dot(h2, wr_ref[...], precision=HIGHEST) + br_ref[...]
    lane = lax.broadcasted_iota(jnp.int32, logits.shape, 1)
    lane_f = lane.astype(F32)
    is_grp = lane < N_GROUPS
    grp = jnp.where(is_grp, logits, NEG_INF)
    g_max = jnp.max(grp, axis=1, keepdims=True)
    g_star = jnp.min(jnp.where(grp == g_max, lane_f, float(N_GROUPS)), axis=1, keepdims=True)
    p_g = 1.0 / jnp.sum(jnp.where(is_grp, jnp.exp(grp - g_max), 0.0), axis=1, keepdims=True)
    e_id = lane_f - float(N_GROUPS)
    in_grp = (e_id >= g_star * EXPERTS_PER_GROUP) & (e_id < (g_star + 1.0) * EXPERTS_PER_GROUP)
    cand = jnp.where(in_grp, logits, NEG_INF)
    l1 = jnp.max(cand, axis=1, keepdims=True)
    e1 = jnp.min(jnp.where(cand == l1, e_id, float(N_EXPERTS)), axis=1, keepdims=True)
    cand = jnp.where(e_id == e1, NEG_INF, cand)
    l2 = jnp.max(cand, axis=1, keepdims=True)
    e2 = jnp.min(jnp.where(cand == l2, e_id, float(N_EXPERTS)), axis=1, keepdims=True)
    t = jnp.exp(l2 - l1)
    c1 = p_g / (1.0 + t)
    c2 = p_g * t / (1.0 + t)
    route = jnp.where(lane == ROUTE_E1, e1, 0.0)
    route = jnp.where(lane == ROUTE_E2, e2, route)
    route = jnp.where(lane == ROUTE_C1, c1, route)
    route_ref[...] = jnp.where(lane == ROUTE_C2, c2, route)


def _mixout(ya, yb, ga, gb, x, layer, mod, mod_spec, g_ffn, w_pa, w_pb, w_out, w_r, b_r, tm):
    n, d = x.shape
    row = lambda width: pl.BlockSpec((tm, width), lambda i: (i, 0))
    layer_w = lambda a: pl.BlockSpec((None,) + a.shape[1:], lambda i: (layer, 0, 0))
    return pl.pallas_call(
        _mixout_kernel,
        grid=(n // tm,),
        in_specs=[row(ya.shape[1]), row(yb.shape[1]), row(d), row(d), row(d),
                  mod_spec(2), mod_spec(4), mod_spec(3), pl.BlockSpec((None, 1, d), lambda i: (layer, 0, 0)),
                  layer_w(w_pa), layer_w(w_pb), layer_w(w_out), layer_w(w_r), layer_w(b_r)],
        out_specs=[row(d), pl.BlockSpec((tm * SUBLANES, LANES), lambda i: (i, 0)), row(LANES)],
        out_shape=[jax.ShapeDtypeStruct((n, d), F32), jax.ShapeDtypeStruct((n * SUBLANES, LANES), F32),
                   jax.ShapeDtypeStruct((n, LANES), F32)],
        compiler_params=_cparams(1),
        name="mixout",
    )(ya, yb, ga, gb, x, mod, mod, mod, g_ffn.reshape(g_ffn.shape[0], 1, d), w_pa, w_pb, w_out, w_r, b_r)


MOE_ROWS = 256
ROW_CHUNK = 32
N_BUF = 2


def _experts_kernel(blk_e_ref, n_valid_ref, src_ref, order_ref, h_hbm, wg_ref, wu_ref, wd_ref, y_hbm,
                    x_buf, y_buf, gather_sem, scatter_sem, *, n_tokens, n_blocks):
    i = pl.program_id(0)
    buf = i % N_BUF
    n_slots = TOPK_IN_GROUP * n_tokens

    def n_chunks(b):
        return jnp.right_shift(n_valid_ref[b] + (ROW_CHUNK - 1), ROW_CHUNK.bit_length() - 1)

    def slot_of(b, r):
        return order_ref[jnp.minimum(src_ref[b] + r, n_slots - 1)]

    chunk_rows = ROW_CHUNK * SUBLANES

    def row_view(buf_ref, bf, c, u):
        return buf_ref.at[bf, pl.ds(pl.multiple_of(c * chunk_rows, chunk_rows) + u * SUBLANES, SUBLANES)]

    def chunk_view(buf_ref, bf, c):
        return buf_ref.at[bf, pl.ds(pl.multiple_of(c * chunk_rows, chunk_rows), chunk_rows)]

    def token_tile(hbm_ref, token):
        return hbm_ref.at[pl.ds(pl.multiple_of(token * SUBLANES, SUBLANES), SUBLANES)]

    def for_rows(b, fn):
        def chunk(c, carry):
            for u in range(ROW_CHUNK):
                fn(c, u)
            return carry
        lax.fori_loop(0, n_chunks(b), chunk, 0)

    def wait_chunks(b, buf_ref, bf, sem):
        def chunk(c, carry):
            view = chunk_view(buf_ref, bf, c)
            pltpu.make_async_copy(view, view, sem.at[bf]).wait()
            return carry
        lax.fori_loop(0, n_chunks(b), chunk, 0)

    def start_gather(b, bf):
        def row(c, u):
            slot = slot_of(b, c * ROW_CHUNK + u)
            token = jnp.where(slot >= n_tokens, slot - n_tokens, slot)
            pltpu.make_async_copy(token_tile(h_hbm, token), row_view(x_buf, bf, c, u), gather_sem.at[bf]).start()
        for_rows(b, row)

    def start_scatter(b, bf):
        n_valid = n_valid_ref[b]

        def row(c, u):
            r = c * ROW_CHUNK + u
            dst = jnp.where(r < n_valid, slot_of(b, r), n_slots + bf * MOE_ROWS + r)
            pltpu.make_async_copy(row_view(y_buf, bf, c, u), token_tile(y_hbm, dst), scatter_sem.at[bf]).start()
        for_rows(b, row)

    @pl.when(i == 0)
    def _():
        x_buf[...] = jnp.zeros_like(x_buf)
        y_buf[...] = jnp.zeros_like(y_buf)
        for bf in range(N_BUF):
            spare = y_hbm.at[pl.ds((n_slots + bf * MOE_ROWS) * SUBLANES, MOE_ROWS * SUBLANES)]
            init = pltpu.make_async_copy(y_buf.at[bf], spare, scatter_sem.at[bf])
            init.start()
            init.wait()
        start_gather(0, 0)

    @pl.when(i + 1 < n_blocks)
    def _():
        start_gather(i + 1, 1 - buf)

    wait_chunks(i, x_buf, buf, gather_sem)

    @pl.when(n_valid_ref[i] > 0)
    def _():
        xb = _load_token_tiles(x_buf, MOE_ROWS, (buf,)).astype(BF16)
        gate = _dot(xb, wg_ref[...].astype(BF16))
        hidden = gate * _sigmoid(gate) * _dot(xb, wu_ref[...].astype(BF16))
        _store_token_tiles(_dot(hidden.astype(BF16), wd_ref[...].astype(BF16)), y_buf, (buf,))

    start_scatter(i, buf)

    @pl.when(i > 0)
    def _():
        wait_chunks(i - 1, y_buf, 1 - buf, scatter_sem)

    @pl.when(i == n_blocks - 1)
    def _():
        wait_chunks(i, y_buf, buf, scatter_sem)


def _experts(h2, expert_ids, layer, w_gate, w_up, w_down):
    n = expert_ids.shape[0]
    assert h2.shape == (n * SUBLANES, LANES) and w_gate.shape[2] == SUBLANES * LANES
    m = TOPK_IN_GROUP * n
    n_blocks = -(-m // MOE_ROWS) + N_EXPERTS
    flat_e = expert_ids.T.reshape(m)
    order = jnp.argsort(flat_e).astype(jnp.int32)
    experts = jnp.arange(N_EXPERTS, dtype=jnp.int32)
    counts = jnp.sum((flat_e[:, None] == experts[None, :]).astype(jnp.int32), axis=0)
    padded = (counts + MOE_ROWS - 1) // MOE_ROWS * MOE_ROWS
    pad_end = jnp.cumsum(padded)
    pad_start = pad_end - padded
    start = jnp.cumsum(counts) - counts
    block_start = jnp.arange(n_blocks, dtype=jnp.int32) * MOE_ROWS
    blk_e = jnp.minimum(jnp.sum((pad_end[None, :] <= block_start[:, None]).astype(jnp.int32), axis=1), N_EXPERTS - 1)
    offset = block_start - pad_start[blk_e]
    n_valid = jnp.clip(counts[blk_e] - offset, 0, MOE_ROWS).astype(jnp.int32)
    src = jnp.clip(start[blk_e] + offset, 0, m - 1).astype(jnp.int32)
    blk_e = jnp.where(n_valid > 0, blk_e, jnp.max(jnp.where(n_valid > 0, blk_e, 0))).astype(jnp.int32)

    w_spec = lambda a: pl.BlockSpec((None, None) + a.shape[2:], lambda i, be, nv, sr, od: (layer, be[i], 0, 0))
    grid_spec = pltpu.PrefetchScalarGridSpec(
        num_scalar_prefetch=4,
        grid=(n_blocks,),
        in_specs=[pl.BlockSpec(memory_space=pl.ANY), w_spec(w_gate), w_spec(w_up), w_spec(w_down)],
        out_specs=pl.BlockSpec(memory_space=pl.ANY),
        scratch_shapes=[pltpu.VMEM((N_BUF, MOE_ROWS * SUBLANES, LANES), F32)] * 2
        + [pltpu.SemaphoreType.DMA((N_BUF,))] * 2,
    )
    return pl.pallas_call(
        functools.partial(_experts_kernel, n_tokens=n, n_blocks=n_blocks),
        grid_spec=grid_spec,
        out_shape=jax.ShapeDtypeStruct(((m + N_BUF * MOE_ROWS) * SUBLANES, LANES), F32),
        compiler_params=_cparams(1),
        name="experts",
    )(blk_e, n_valid, src, order, h2, w_gate, w_up, w_down)


def _combine_kernel(x1_ref, y0_ref, y1_ref, route_ref, gt2_ref, g_ref, o_ref, *, final_norm):
    route = route_ref[...]
    t = route.shape[0]
    moe = (route[:, ROUTE_C1:ROUTE_C1 + 1] * _load_token_tiles(y0_ref, t)
           + route[:, ROUTE_C2:ROUTE_C2 + 1] * _load_token_tiles(y1_ref, t))
    x2 = x1_ref[...] + gt2_ref[...] * moe
    if final_norm:
        x2 = x2 * lax.rsqrt(jnp.mean(x2 * x2, axis=-1, keepdims=True) + RMS_EPS) * g_ref[...]
    o_ref[...] = x2


def _combine(x1, y, route, mod, mod_spec, g_final, tm, final_norm):
    n, d = x1.shape
    nt = n // tm
    row = lambda width: pl.BlockSpec((tm, width), lambda i: (i, 0))
    tiles = lambda first: pl.BlockSpec((tm * SUBLANES, LANES), lambda i: (first + i, 0))
    return pl.pallas_call(
        functools.partial(_combine_kernel, final_norm=final_norm),
        grid=(nt,),
        in_specs=[row(d), tiles(0), tiles(nt), row(LANES),
                  mod_spec(5), pl.BlockSpec((1, d), lambda i: (0, 0))],
        out_specs=row(d),
        out_shape=jax.ShapeDtypeStruct((n, d), F32),
        compiler_params=_cparams(1),
        name="combine",
    )(x1, y, y, route, mod, g_final.reshape(1, d))


def _rope_tables(pos):
    half = HEAD_DIM // 2
    inv = jnp.power(ROPE_THETA, -jnp.arange(half, dtype=F32) / half)
    ang = pos.astype(F32)[:, None] * inv[None, :]
    cos = jnp.tile(jnp.cos(ang), (1, LANES // half))
    sin = jnp.tile(jnp.sin(ang), (1, LANES // half))
    return cos, sin, cos.T, sin.T


def _sample_rows(a, batch, t):
    return jnp.pad(a.astype(F32).reshape(batch, t, a.shape[-1]), ((0, 0), (0, SAMPLE_ROWS - t), (0, 0)))


def _sample_new_keys(a_t, batch, t):
    r = a_t.shape[1]
    per_b = jnp.transpose(a_t.reshape(r, batch, t), (1, 0, 2))
    return jnp.pad(per_b, ((0, 0), (0, 0), (0, LANES - t)))


def _cache_t(cache):
    depth, n_pool, page = cache.shape[:3]
    return jnp.swapaxes(cache.reshape(depth, n_pool, page, -1), 2, 3)


def kernel(x_prompt, x_sample, c_prompt, c_sample, cache_k_a, cache_v_a, cache_idx_k, cache_k_b, cache_v_b, page_table, w_ada, b_ada, g_mix, w_in, w_proj_a, w_proj_b, w_out, g_ffn, w_router_group, b_router_group, w_router_expert, b_router_expert, w_exp_gate, w_exp_up, w_exp_down, g_final):
    batch, seq, d = x_prompt.shape
    dec_batch, dec_seq, _ = x_sample.shape
    depth = w_in.shape[0]
    past_len = page_table.shape[1] * cache_k_a.shape[2]
    n_p, n_s = batch * seq, dec_batch * dec_seq
    tm_p = min(512, seq)
    tm_s = min(256, n_s)
    assert seq % tm_p == 0 and n_s % tm_s == 0

    mod_all = _ada_mod(jnp.concatenate([c_prompt, c_sample], axis=0), w_ada, b_ada)
    mod_p = mod_all[:, :batch].reshape(depth, batch, 1, 6 * d)
    mod_s = jnp.repeat(mod_all[:, batch:], dec_seq, axis=1)
    w_row, w_col = _split_w_in(w_in)
    w_pa, w_pb, w_o = w_proj_a.astype(BF16), w_proj_b.astype(BF16), w_out.astype(BF16)
    pad_r = LANES - N_GROUPS - N_EXPERTS
    w_r = jnp.pad(jnp.concatenate([w_router_group, w_router_expert], axis=2), ((0, 0), (0, 0), (0, pad_r)))
    b_r = jnp.pad(jnp.concatenate([b_router_group, b_router_expert], axis=1), ((0, 0), (0, pad_r)))
    b_r = b_r.reshape(depth, 1, LANES)
    cache_idx_t = _cache_t(cache_idx_k)
    caches_t = tuple(_cache_t(c) for c in (cache_k_a, cache_v_a, cache_k_b, cache_v_b))

    trig_p = _rope_tables(jnp.arange(seq, dtype=jnp.int32))
    trig_s = _rope_tables(jnp.tile(past_len + jnp.arange(dec_seq, dtype=jnp.int32), dec_batch))
    nq_p = seq // tm_p
    trig_spec_p = pl.BlockSpec((tm_p, LANES), lambda i: (i % nq_p, 0))
    trig_t_spec_p = pl.BlockSpec((LANES, tm_p), lambda i: (0, i % nq_p))
    trig_spec_s = pl.BlockSpec((tm_s, LANES), lambda i: (i, 0))
    trig_t_spec_s = pl.BlockSpec((LANES, tm_s), lambda i: (0, i))

    xp = x_prompt.reshape(n_p, d)
    xs = x_sample.reshape(n_s, d)
    rows_p, rows_s = [], []
    for l in range(depth):
        mod_spec_p = lambda chunk, l=l: pl.BlockSpec((None, None, 1, d), lambda i: (l, i // nq_p, 0, chunk))
        mod_spec_s = lambda chunk, l=l: pl.BlockSpec((None, tm_s, d), lambda i: (l, i, chunk))
        last = l == depth - 1

        qa, qi, wi, qb, ga, gb, ka_t, va_t, ki_t, kb_t, vb_t = _inproj(
            xp, l, mod_p, mod_spec_p, g_mix, w_row, w_col, trig_p, trig_spec_p, trig_t_spec_p, batch, tm_p, BF16)
        rows_p.append((ka_t, va_t, ki_t, kb_t, vb_t))
        ya = _dsa_prompt(qi, wi, qa, ka_t, va_t, min(PROMPT_TQ, seq))
        yb = _moba_prompt(qb, kb_t, vb_t, min(PROMPT_TQ, seq))
        x1, h2, route = _mixout(ya, yb, ga, gb, xp, l, mod_p, mod_spec_p, g_ffn, w_pa, w_pb, w_o, w_r, b_r, tm_p)
        y = _experts(h2, route[:, ROUTE_E1:ROUTE_E2 + 1].astype(jnp.int32), l, w_exp_gate, w_exp_up, w_exp_down)
        xp = _combine(x1, y, route, mod_p, mod_spec_p, g_final, tm_p, last)

        qa, qi, wi, qb, ga, gb, ka_t, va_t, ki_t, kb_t, vb_t = _inproj(
            xs, l, mod_s, mod_spec_s, g_mix, w_row, w_col, trig_s, trig_spec_s, trig_t_spec_s, 1, tm_s, F32)
        rows_s.append((ka_t, va_t, ki_t, kb_t, vb_t))
        rows8 = lambda a: _sample_rows(a, dec_batch, dec_seq)
        new = lambda a_t: _sample_new_keys(a_t, dec_batch, dec_seq)
        score = _score_sample(rows8(qi), rows8(wi), new(ki_t), cache_idx_t, page_table, l)
        s_len = score.shape[-1]
        bias = _select_sample(score.reshape(dec_batch * SAMPLE_ROWS, s_len),
                              min(IDX_TOPK_MAX, (past_len + dec_seq) // 4), dec_seq)
        ya, yb = _attend_sample(rows8(qa), bias.reshape(dec_batch, SAMPLE_ROWS, s_len), new(ka_t), new(va_t),
                                rows8(qb), new(kb_t), new(vb_t), caches_t, page_table, l, dec_seq)
        unpad = lambda a: a[:, :dec_seq].reshape(n_s, a.shape[-1]).astype(BF16)
        x1, h2, route = _mixout(unpad(ya), unpad(yb), ga, gb, xs, l, mod_s, mod_spec_s, g_ffn, w_pa, w_pb, w_o,
                                w_r, b_r, tm_s)
        y = _experts(h2, route[:, ROUTE_E1:ROUTE_E2 + 1].astype(jnp.int32), l, w_exp_gate, w_exp_up, w_exp_down)
        xs = _combine(x1, y, route, mod_s, mod_spec_s, g_final, tm_s, last)

    def stack_p(i, heads):
        a = jnp.stack([r[i] for r in rows_p], axis=0)
        a = jnp.swapaxes(a, 2, 3)
        return a.reshape((depth, batch, seq, heads, HEAD_DIM) if heads else (depth, batch, seq, D_IDX))

    def stack_s(i, heads):
        a = jnp.stack([r[i][0] for r in rows_s], axis=0)
        a = jnp.swapaxes(a, 1, 2)
        return a.reshape((depth, dec_batch, dec_seq, heads, HEAD_DIM) if heads else (depth, dec_batch, dec_seq, D_IDX))

    return (xp.reshape(batch, seq, d), xs.reshape(dec_batch, dec_seq, d),
            stack_p(0, N_KV_A), stack_p(1, N_KV_A), stack_p(3, N_KV_B), stack_p(4, N_KV_B), stack_p(2, 0),
            stack_s(0, N_KV_A), stack_s(1, N_KV_A), stack_s(3, N_KV_B), stack_s(4, N_KV_B), stack_s(2, 0))
```

```python
import functools

import jax
import jax.numpy as jnp
from jax import lax
from jax.experimental import pallas as pl
from jax.experimental.pallas import tpu as pltpu

HEAD_DIM = 64
N_HEADS_A = 8
N_KV_A = 2
N_IDX_HEADS = 4
D_IDX = 64
IDX_TOPK_MAX = 256
N_HEADS_B = 8
N_KV_B = 2
MOBA_BLOCK = 256
MOBA_TOPK_MAX = 3
N_GROUPS = 4
EXPERTS_PER_GROUP = 8
N_EXPERTS = N_GROUPS * EXPERTS_PER_GROUP
TOPK_IN_GROUP = 2
ROPE_THETA = 10000.0
RMS_EPS = 1e-6

LANES = 128
KV_ROWS = N_KV_A * HEAD_DIM
assert KV_ROWS == LANES == N_KV_B * HEAD_DIM and D_IDX == HEAD_DIM and TOPK_IN_GROUP == 2

VMEM_LIMIT = 56 * 1024 * 1024

F32 = jnp.float32
BF16 = jnp.bfloat16
NEG_INF = float("-inf")
INT_MIN = -(2 ** 31)
NEG_INF_KEY = -2139095041
HIGHEST = lax.Precision.HIGHEST
SCORE_SCALE = HEAD_DIM ** -0.5 * 1.4426950408889634


def _cparams(n_grid):
    return pltpu.CompilerParams(dimension_semantics=("arbitrary",) * n_grid, vmem_limit_bytes=VMEM_LIMIT)


def _dot(a, b, precision=None):
    return jnp.dot(a, b, preferred_element_type=F32, precision=precision)


def _dot_nt(a, b, precision=None):
    return lax.dot_general(a, b, (((1,), (1,)), ((), ())), preferred_element_type=F32, precision=precision)


def _sigmoid(x):
    return 1.0 / (1.0 + jnp.exp(-x))


def _ada_kernel(c_ref, w_ref, b_ref, o_ref):
    c = c_ref[...]
    o_ref[...] = _dot((c * _sigmoid(c)).astype(BF16), w_ref[...].astype(BF16)) + b_ref[...]


def _ada_mod(c_all, w_ada, b_ada):
    depth, d, n6 = w_ada.shape
    m = c_all.shape[0]
    tn = 1024
    return pl.pallas_call(
        _ada_kernel,
        grid=(depth, n6 // tn),
        in_specs=[
            pl.BlockSpec((m, d), lambda l, j: (0, 0)),
            pl.BlockSpec((None, d, tn), lambda l, j: (l, 0, j)),
            pl.BlockSpec((None, 1, tn), lambda l, j: (l, 0, j)),
        ],
        out_specs=pl.BlockSpec((None, m, tn), lambda l, j: (l, 0, j)),
        out_shape=jax.ShapeDtypeStruct((depth, m, n6), F32),
        compiler_params=_cparams(2),
        name="ada_mod",
    )(c_all, w_ada, b_ada.reshape(depth, 1, n6))


_QA_W = N_HEADS_A * HEAD_DIM
_QI_W = N_IDX_HEADS * D_IDX
_QB_W = N_HEADS_B * HEAD_DIM
_ROW_QA, _ROW_QI, _ROW_WI, _ROW_QB = 0, _QA_W, _QA_W + _QI_W, _QA_W + _QI_W + LANES
_ROW_GATE = _ROW_QB + _QB_W
_COL_KA, _COL_VA, _COL_KI, _COL_KB, _COL_VB = 0, KV_ROWS, 2 * KV_ROWS, 2 * KV_ROWS + D_IDX, 3 * KV_ROWS + D_IDX
_COL_END = 4 * KV_ROWS + D_IDX


def _rope_lanes(y, cos, sin, lane):
    first_half = (lane % HEAD_DIM) < (HEAD_DIM // 2)
    rot = jnp.where(first_half, -pltpu.roll(y, LANES - HEAD_DIM // 2, 1), pltpu.roll(y, HEAD_DIM // 2, 1))
    return y * cos + rot * sin


def _rope_rows(y, cos, sin):
    half = HEAD_DIM // 2
    parts = []
    for r0 in range(0, y.shape[0], HEAD_DIM):
        parts += [-y[r0 + half:r0 + HEAD_DIM], y[r0:r0 + half]]
    rot = jnp.concatenate(parts, axis=0)
    return y * cos[:y.shape[0]] + rot * sin[:y.shape[0]]


def _inproj_kernel(x_ref, sc_ref, sh_ref, g_ref, wr_ref, wc_ref, cos_ref, sin_ref, cos_t_ref, sin_t_ref,
                   qa_ref, qi_ref, wi_ref, qb_ref, ga_ref, gb_ref, ka_ref, va_ref, ki_ref, kb_ref, vb_ref,
                   *, d_model):
    x = x_ref[...]
    h = x * lax.rsqrt(jnp.mean(x * x, axis=-1, keepdims=True) + RMS_EPS) * g_ref[...]
    h = h * (1.0 + sc_ref[...]) + sh_ref[...]
    hb = h.astype(BF16)
    cos = cos_ref[...]
    sin = sin_ref[...]
    lane = lax.broadcasted_iota(jnp.int32, cos.shape, 1)

    def proj(c0, width):
        return _dot(hb, wr_ref[:, c0:c0 + width])

    def roped(c0, out_ref, scale=None):
        for c in range(out_ref.shape[1] // LANES):
            y = _rope_lanes(proj(c0 + c * LANES, LANES), cos, sin, lane)
            if scale is not None:
                y = y * scale
            out_ref[:, c * LANES:(c + 1) * LANES] = y.astype(out_ref.dtype)

    roped(_ROW_QA, qa_ref, SCORE_SCALE)
    roped(_ROW_QI, qi_ref)
    wik = proj(_ROW_WI, LANES)
    wi_ref[...] = jnp.where(lane >= D_IDX, _rope_lanes(wik, cos, sin, lane), wik)
    roped(_ROW_QB, qb_ref)
    for c in range(d_model // 512):
        ga_ref[:, c * 512:(c + 1) * 512] = _sigmoid(proj(_ROW_GATE + c * 512, 512)).astype(ga_ref.dtype)
        gb_ref[:, c * 512:(c + 1) * 512] = _sigmoid(proj(_ROW_GATE + d_model + c * 512, 512)).astype(gb_ref.dtype)

    cos_t = cos_t_ref[...]
    sin_t = sin_t_ref[...]

    def proj_t(r0, r1):
        return _dot_nt(wc_ref[r0:r1, :], hb)

    ka_ref[...] = _rope_rows(proj_t(_COL_KA, _COL_VA), cos_t, sin_t)
    va_ref[...] = proj_t(_COL_VA, _COL_KI)
    ki_ref[...] = _rope_rows(proj_t(_COL_KI, _COL_KB), cos_t, sin_t)
    kb_ref[...] = _rope_rows(proj_t(_COL_KB, _COL_VB), cos_t, sin_t)
    vb_ref[...] = proj_t(_COL_VB, _COL_END)


def _split_w_in(w_in):
    depth, d, d_in = w_in.shape
    widths = (_QA_W, KV_ROWS, KV_ROWS, _QI_W, D_IDX, N_IDX_HEADS, _QB_W, KV_ROWS, KV_ROWS, d, d)
    offs = [0]
    for w in widths:
        offs.append(offs[-1] + w)
    assert offs[-1] == d_in
    w_t = jnp.transpose(w_in, (2, 0, 1))
    qa, ka, va, qi, ki, wi, qb, kb, vb, ga, gb = [w_t[offs[i]:offs[i + 1]] for i in range(11)]
    pad = jnp.zeros((LANES - N_IDX_HEADS - D_IDX, depth, d), w_in.dtype)
    w_row = jnp.transpose(jnp.concatenate([qa, qi, wi, pad, ki, qb, ga, gb], axis=0).astype(BF16), (1, 2, 0))
    w_col = jnp.transpose(jnp.concatenate([ka, va, ki, kb, vb], axis=0).astype(BF16), (1, 0, 2))
    return w_row, w_col


def _inproj(x, layer, mod, mod_spec, g_mix, w_row, w_col, trig, trig_spec, trig_t_spec, n_batch, tm, q_dtype):
    n, d = x.shape
    per_batch = n // n_batch
    nq = per_batch // tm
    cos, sin, cos_t, sin_t = trig
    row = lambda width: pl.BlockSpec((tm, width), lambda i: (i, 0))
    col = lambda rows: pl.BlockSpec((None, rows, tm), lambda i: (i // nq, 0, i % nq))
    row_out = [(_QA_W, q_dtype), (_QI_W, q_dtype), (LANES, F32), (_QB_W, F32), (d, q_dtype), (d, q_dtype)]
    col_out = [KV_ROWS, KV_ROWS, D_IDX, KV_ROWS, KV_ROWS]
    layer_w = lambda a: pl.BlockSpec((None,) + a.shape[1:], lambda i: (layer, 0, 0))
    return pl.pallas_call(
        functools.partial(_inproj_kernel, d_model=d),
        grid=(n // tm,),
        in_specs=[row(d), mod_spec(1), mod_spec(0), pl.BlockSpec((None, 1, d), lambda i: (layer, 0, 0)),
                  layer_w(w_row), layer_w(w_col), trig_spec, trig_spec, trig_t_spec, trig_t_spec],
        out_specs=[row(w) for w, _ in row_out] + [col(r) for r in col_out],
        out_shape=[jax.ShapeDtypeStruct((n, w), dt) for w, dt in row_out]
        + [jax.ShapeDtypeStruct((n_batch, r, per_batch), F32) for r in col_out],
        compiler_params=_cparams(1),
        name="inproj",
    )(x, mod, mod, g_mix.reshape(g_mix.shape[0], 1, d), w_row, w_col, cos, sin, cos_t, sin_t)


def _stack_heads(q, first_head, n_heads, scale=None):
    q = q.astype(F32)
    if scale is not None:
        q = q * scale
    parts = [q[:, (first_head + g) * HEAD_DIM:(first_head + g + 1) * HEAD_DIM] for g in range(n_heads)]
    return jnp.concatenate(parts, axis=0).astype(BF16)


def _sort_key(x):
    bits = lax.bitcast_convert_type(x, jnp.int32)
    return bits ^ ((bits >> 31) & 0x7FFFFFFF)


def _count(mask):
    return jnp.sum(jnp.where(mask, 1.0, 0.0), axis=1, keepdims=True)


def _indexer_scores(qi, wi, ki_t, q_pos):
    t = qi.shape[0]
    logits = _dot(_stack_heads(qi, 0, N_IDX_HEADS), ki_t)
    score = None
    for h in range(N_IDX_HEADS):
        term = jnp.maximum(logits[h * t:(h + 1) * t], 0.0) * wi[:, h:h + 1]
        score = term if score is None else score + term
    k_pos = lax.broadcasted_iota(jnp.int32, score.shape, 1)
    score = jnp.where(score == 0.0, 0.0, score)
    return jnp.where(k_pos <= q_pos, score, NEG_INF)


def _topk_mask(score, n_keep, row_is_real, bias_ref):
    t, s_len = score.shape
    key = _sort_key(score)
    keep = float(n_keep)
    thr = jnp.where(_count(key >= 0) >= keep, 0, INT_MIN).astype(jnp.int32)

    def two_bits(i, thr):
        hi = jnp.left_shift(jnp.int32(1), 30 - 2 * i)
        lo = jnp.left_shift(jnp.int32(1), 29 - 2 * i)
        c_hi, c_lo, c_both = thr + hi, thr + lo, thr + hi + lo
        n_hi, n_lo, n_both = _count(key >= c_hi), _count(key >= c_lo), _count(key >= c_both)
        return jnp.where(n_both >= keep, c_both, jnp.where(n_hi >= keep, c_hi, jnp.where(n_lo >= keep, c_lo, thr)))

    thr = lax.fori_loop(0, 15, two_bits, thr)
    thr = jnp.where(_count(key >= thr + 1) >= keep, thr + 1, thr)
    at_least = key >= thr
    n_above = _count(key > thr)
    bias_ref[...] = jnp.where(at_least & (key > NEG_INF_KEY), 0.0, NEG_INF)

    tied = (_count(at_least) > keep) & (thr > NEG_INF_KEY) & row_is_real

    @pl.when(jnp.max(jnp.where(tied, 1.0, 0.0)) > 0.0)
    def _():
        free = keep - n_above
        r_i = lax.broadcasted_iota(jnp.int32, (MOBA_BLOCK, MOBA_BLOCK), 0)
        c_i = lax.broadcasted_iota(jnp.int32, (MOBA_BLOCK, MOBA_BLOCK), 1)
        tri = jnp.where(r_i <= c_i, 1.0, 0.0).astype(BF16)
        seen = jnp.zeros((t, 1), F32)
        for c0 in range(0, s_len, MOBA_BLOCK):
            w = min(MOBA_BLOCK, s_len - c0)
            key_c = key[:, c0:c0 + w]
            eq_c = jnp.where(key_c == thr, 1.0, 0.0)
            rank = _dot(eq_c.astype(BF16), tri[:w, :w]) + seen
            keep_c = ((key_c > thr) | ((key_c == thr) & (rank <= free))) & (key_c > NEG_INF_KEY)
            bias_ref[:, c0:c0 + w] = jnp.where(keep_c, 0.0, NEG_INF)
            seen = seen + jnp.sum(eq_c, axis=1, keepdims=True)


def _indexer_scores_t(kik, kiw_q, qi, q_pos_t):
    t = qi.shape[0]
    q = qi.astype(F32)
    lane = lax.broadcasted_iota(jnp.int32, (t, LANES), 1)
    parts = []
    for h in range(N_IDX_HEADS):
        pair = q[:, (h // 2) * LANES:(h // 2 + 1) * LANES]
        if h % 2:
            parts.append(jnp.where(lane >= D_IDX, pair, 0.0))
        else:
            parts.append(pltpu.roll(jnp.where(lane < D_IDX, pair, 0.0), D_IDX, 1))
    logits = _dot_nt(kik, jnp.concatenate(parts, axis=0).astype(BF16))
    wi_t = kiw_q.T
    score = None
    for h in range(N_IDX_HEADS):
        term = jnp.maximum(logits[:, h * t:(h + 1) * t], 0.0) * wi_t[h:h + 1, :]
        score = term if score is None else score + term
    k_pos = lax.broadcasted_iota(jnp.int32, score.shape, 0)
    score = jnp.where(score == 0.0, 0.0, score)
    return jnp.where(k_pos <= q_pos_t, score, NEG_INF)


COUNT_SLAB = 64


def _count_t(mask):
    ones = jnp.where(mask, 1.0, 0.0)
    s_len, t = ones.shape
    if s_len % COUNT_SLAB == 0 and s_len > COUNT_SLAB:
        ones = jnp.sum(ones.reshape(s_len // COUNT_SLAB, COUNT_SLAB, t), axis=0)
    return jnp.sum(ones, axis=0, keepdims=True)


def _store_transposed(mask_t, c0, bias_ref):
    for j in range(0, mask_t.shape[0], LANES):
        bias_ref[:, c0 + j:c0 + j + LANES] = mask_t[j:j + LANES, :].T


def _topk_mask_t(score_t, n_keep, bias_ref):
    s_len, t = score_t.shape
    key = _sort_key(score_t)
    keep = float(n_keep)
    thr = jnp.where(_count_t(key >= 0) >= keep, 0, INT_MIN).astype(jnp.int32)

    def bit_step(i, thr):
        cand = thr + jnp.left_shift(jnp.int32(1), 30 - i)
        return jnp.where(_count_t(key >= cand) >= keep, cand, thr)

    thr = lax.fori_loop(0, 31, bit_step, thr)
    at_least = key >= thr
    n_above = _count_t(key > thr)
    _store_transposed(jnp.where(at_least & (key > NEG_INF_KEY), 0.0, NEG_INF), 0, bias_ref)

    tied = (_count_t(at_least) > keep) & (thr > NEG_INF_KEY)

    @pl.when(jnp.max(jnp.where(tied, 1.0, 0.0)) > 0.0)
    def _():
        free = keep - n_above
        r_i = lax.broadcasted_iota(jnp.int32, (MOBA_BLOCK, MOBA_BLOCK), 0)
        c_i = lax.broadcasted_iota(jnp.int32, (MOBA_BLOCK, MOBA_BLOCK), 1)
        tri = jnp.where(c_i <= r_i, 1.0, 0.0).astype(BF16)
        seen = jnp.zeros((1, t), F32)
        for c0 in range(0, s_len, MOBA_BLOCK):
            w = min(MOBA_BLOCK, s_len - c0)
            key_c = key[c0:c0 + w, :]
            eq_c = jnp.where(key_c == thr, 1.0, 0.0)
            rank = _dot(tri[:w, :w], eq_c.astype(BF16)) + seen
            keep_c = ((key_c > thr) | ((key_c == thr) & (rank <= free))) & (key_c > NEG_INF_KEY)
            _store_transposed(jnp.where(keep_c, 0.0, NEG_INF), c0, bias_ref)
            seen = seen + jnp.sum(eq_c, axis=0, keepdims=True)


ONES_ROWS = 16


def _with_ones(v_t):
    return jnp.concatenate([v_t, jnp.ones((ONES_ROWS, v_t.shape[1]), BF16)], axis=0)


def _normalised_pv(p, v_t):
    o = _dot_nt(p, _with_ones(v_t))
    return o[:, :HEAD_DIM] / o[:, HEAD_DIM:HEAD_DIM + 1]


def _softmax_pv(s, v_t):
    m = jnp.max(s, axis=1, keepdims=True)
    return _normalised_pv(jnp.exp2(s - m).astype(BF16), v_t)


def _attend_group(q_g, k_t, v_t, bias):
    s = _dot(q_g, k_t)
    g = s.shape[0] // bias.shape[0]
    s = (s.reshape(g, bias.shape[0], s.shape[1]) + bias[None]).reshape(s.shape)
    return _softmax_pv(s, v_t)


def _store_heads(o_g, t, first_head, n_heads, out_ref):
    for g in range(0, n_heads, 2):
        pair = jnp.concatenate([o_g[g * t:(g + 1) * t], o_g[(g + 1) * t:(g + 2) * t]], axis=1)
        c0 = (first_head + g) * HEAD_DIM
        out_ref[:, c0:c0 + 2 * HEAD_DIM] = pair.astype(out_ref.dtype)


def _dsa_attention(qa, ka_t, va_t, bias, out_ref):
    t = qa.shape[0]
    hpk = N_HEADS_A // N_KV_A
    for kv in range(N_KV_A):
        k_t = ka_t[kv * HEAD_DIM:(kv + 1) * HEAD_DIM].astype(BF16)
        v_t = va_t[kv * HEAD_DIM:(kv + 1) * HEAD_DIM].astype(BF16)
        o_g = _attend_group(_stack_heads(qa, kv * hpk, hpk), k_t, v_t, bias)
        _store_heads(o_g, t, kv * hpk, hpk, out_ref)


GATE_ROWS = 8


def _block_means(kb_t, n_full):
    lane = lax.broadcasted_iota(jnp.int32, (KV_ROWS, LANES), 1)
    km = jnp.zeros((KV_ROWS, LANES), F32)
    for n in range(n_full):
        col = jnp.sum(kb_t[:, n * MOBA_BLOCK:(n + 1) * MOBA_BLOCK], axis=1, keepdims=True) * (1.0 / MOBA_BLOCK)
        km = jnp.where(lane == n, col, km)
    return km.T


def _moba_queries(qb, kv):
    hpk = N_HEADS_B // N_KV_B
    return jnp.concatenate([qb[:, (kv * hpk + g) * HEAD_DIM:(kv * hpk + g + 1) * HEAD_DIM] for g in range(hpk)],
                           axis=0)


def _moba_pick(q_f, k_mean_kv, own, n_sel, n_full):
    assert n_full <= GATE_ROWS
    rows = q_f.shape[0]
    rows_pad = -(-rows // LANES) * LANES
    if rows_pad != rows:
        q_f = jnp.concatenate([q_f, jnp.zeros((rows_pad - rows, HEAD_DIM), F32)], axis=0)
    blk = lax.broadcasted_iota(jnp.int32, (GATE_ROWS, rows_pad), 0)
    blk_f = blk.astype(F32)
    gate = _dot_nt(k_mean_kv, q_f, precision=HIGHEST)
    gate = jnp.where(blk < jnp.minimum(own, n_full), gate, NEG_INF)
    picked_t = jnp.zeros((GATE_ROWS, rows_pad), F32)
    for _ in range(n_sel):
        best = jnp.max(gate, axis=0, keepdims=True)
        first = jnp.min(jnp.where(gate == best, blk_f, float(GATE_ROWS)), axis=0, keepdims=True)
        hit = (blk_f == first) & (best > NEG_INF)
        picked_t = jnp.where(hit, 1.0, picked_t)
        gate = jnp.where(blk_f == first, NEG_INF, gate)
    picked_t = jnp.concatenate([picked_t, jnp.zeros((LANES - GATE_ROWS, rows_pad), F32)], axis=0)
    return picked_t.T[:rows]


def _moba_scores(q_g, k_t, picked, q_pos_g, own, own_min, n_full, s_ref):
    rows, s_len = q_g.shape[0], k_t.shape[1]
    for c0 in range(0, s_len, MOBA_BLOCK):
        c = c0 // MOBA_BLOCK
        w = min(MOBA_BLOCK, s_len - c0)
        if c < own_min:
            allowed = picked[:, c:c + 1] > 0.0
        else:
            k_pos = c0 + lax.broadcasted_iota(jnp.int32, (rows, w), 1)
            allowed = (k_pos <= q_pos_g) & (own == c)
            if c < n_full:
                allowed = allowed | (picked[:, c:c + 1] > 0.0)
        s_ref[:, c0:c0 + w] = jnp.where(allowed, _dot(q_g, k_t[:, c0:c0 + w]), NEG_INF)


def _moba_attention(qb, kb_t, vb_t, k_mean, q_pos, own, own_min, n_sel, n_full, s_ref, out_ref):
    t = qb.shape[0]
    s_len = kb_t.shape[1]
    hpk = N_HEADS_B // N_KV_B
    q_pos_g = jnp.concatenate([q_pos] * hpk, axis=0)
    for kv in range(N_KV_B):
        q_f = _moba_queries(qb, kv)
        picked = _moba_pick(q_f, k_mean[:GATE_ROWS, kv * HEAD_DIM:(kv + 1) * HEAD_DIM], own, n_sel, n_full)
        k_t = kb_t[kv * HEAD_DIM:(kv + 1) * HEAD_DIM].astype(BF16)
        v_t = vb_t[kv * HEAD_DIM:(kv + 1) * HEAD_DIM].astype(BF16)
        _moba_scores((q_f * SCORE_SCALE).astype(BF16), k_t, picked, q_pos_g, own, own_min, n_full,
                     s_ref.at[:, :s_len])
        _store_heads(_softmax_pv(s_ref[:, :s_len], v_t), t, kv * hpk, hpk, out_ref)


PROMPT_TQ = 128
KEY_SEGMENT = MOBA_BLOCK


def _per_key_segment(seq, tq, body):
    n_seg = max(1, seq // KEY_SEGMENT)
    seg_len = seq // n_seg
    assert seg_len % tq == 0 and seg_len % MOBA_BLOCK == 0
    seg = (pl.program_id(1) * tq) // seg_len
    for k in range(n_seg):
        pl.when(seg == k)(functools.partial(body, (k + 1) * seg_len, seg_len))


def _dsa_prompt_kernel(qi_ref, wik_q_ref, qa_ref, wik_k_ref, ka_ref, va_ref, o_ref, bias_ref, kik_ref,
                       *, tq, seq, n_keep):
    q_pos_t = pl.program_id(1) * tq + lax.broadcasted_iota(jnp.int32, (1, tq), 1)

    @pl.when(pl.program_id(1) == 0)
    def _():
        kik_ref[...] = wik_k_ref[...].astype(BF16)

    def body(s_len, seg_len):
        if s_len <= n_keep:
            q_pos = pl.program_id(1) * tq + lax.broadcasted_iota(jnp.int32, (tq, 1), 0)
            k_pos = lax.broadcasted_iota(jnp.int32, (tq, s_len), 1)
            bias_ref[:, :s_len] = jnp.where(k_pos <= q_pos, 0.0, NEG_INF)
        else:
            score_t = _indexer_scores_t(kik_ref[:s_len, :], wik_q_ref[...], qi_ref[...], q_pos_t)
            _topk_mask_t(score_t, n_keep, bias_ref.at[:, :s_len])
        _dsa_attention(qa_ref[...], ka_ref[:, :s_len], va_ref[:, :s_len], bias_ref[:, :s_len], o_ref)

    _per_key_segment(seq, tq, body)


def _dsa_prompt(qi, wik, qa, ka_t, va_t, tq):
    batch, _, seq = ka_t.shape
    assert tq == LANES
    nq = seq // tq
    qrow = lambda w: pl.BlockSpec((tq, w), lambda b, i: (b * nq + i, 0))
    keys_t = pl.BlockSpec((None, KV_ROWS, seq), lambda b, i: (b, 0, 0))
    return pl.pallas_call(
        functools.partial(_dsa_prompt_kernel, tq=tq, seq=seq, n_keep=min(IDX_TOPK_MAX, seq // 4)),
        grid=(batch, nq),
        in_specs=[qrow(qi.shape[1]), qrow(LANES), qrow(qa.shape[1]),
                  pl.BlockSpec((seq, LANES), lambda b, i: (b, 0)), keys_t, keys_t],
        out_specs=qrow(qa.shape[1]),
        out_shape=jax.ShapeDtypeStruct(qa.shape, BF16),
        scratch_shapes=[pltpu.VMEM((tq, seq), F32), pltpu.VMEM((seq, LANES), BF16)],
        compiler_params=_cparams(2),
        name="dsa_prompt",
    )(qi, wik, qa, wik, ka_t, va_t)


def _moba_prompt_kernel(qb_ref, kb_ref, vb_ref, o_ref, bias_ref, km_ref, *, tq, seq, n_sel):
    base = pl.program_id(1) * tq
    q_pos = base + lax.broadcasted_iota(jnp.int32, (tq, 1), 0)
    n_full = seq // MOBA_BLOCK

    @pl.when(pl.program_id(1) == 0)
    def _():
        km_ref[...] = _block_means(kb_ref[...], n_full)

    def body(s_len, seg_len):
        _moba_attention(qb_ref[...], kb_ref[:, :s_len], vb_ref[:, :s_len], km_ref[...], q_pos, base // MOBA_BLOCK,
                        (s_len - seg_len) // MOBA_BLOCK, n_sel, n_full, bias_ref, o_ref)

    _per_key_segment(seq, tq, body)


def _moba_prompt(qb, kb_t, vb_t, tq):
    batch, _, seq = kb_t.shape
    assert MOBA_BLOCK % tq == 0 and seq % MOBA_BLOCK == 0
    nq = seq // tq
    qrow = pl.BlockSpec((tq, qb.shape[1]), lambda b, i: (b * nq + i, 0))
    keys = pl.BlockSpec((None, KV_ROWS, seq), lambda b, i: (b, 0, 0))
    return pl.pallas_call(
        functools.partial(_moba_prompt_kernel, tq=tq, seq=seq, n_sel=min(MOBA_TOPK_MAX, seq // MOBA_BLOCK)),
        grid=(batch, nq),
        in_specs=[qrow, keys, keys],
        out_specs=qrow,
        out_shape=jax.ShapeDtypeStruct(qb.shape, BF16),
        scratch_shapes=[pltpu.VMEM((N_HEADS_B // N_KV_B * tq, seq), F32), pltpu.VMEM((KV_ROWS, LANES), F32)],
        compiler_params=_cparams(2),
        name="moba_prompt",
    )(qb, kb_t, vb_t)


SAMPLE_ROWS = 8


def _assemble_keys(page_refs, new_ref, dst_ref, page_size):
    for j, page in enumerate(page_refs):
        dst_ref[:, j * page_size:(j + 1) * page_size] = page[...]
    dst_ref[:, len(page_refs) * page_size:] = new_ref[...]


def _page_specs(layer, n_pages, rows, page_size):
    return [pl.BlockSpec((None, None, rows, page_size), lambda b, pt, j=j: (layer, pt[b, j], 0, 0))
            for j in range(n_pages)]


def _score_sample_kernel(pt_ref, qi_ref, wi_ref, kin_ref, *rest, n_pages, page_size):
    ki_pages = rest[:n_pages]
    o_ref, ki_s = rest[n_pages:]
    _assemble_keys(ki_pages, kin_ref, ki_s, page_size)
    q_pos = n_pages * page_size + lax.broadcasted_iota(jnp.int32, (SAMPLE_ROWS, 1), 0)
    o_ref[...] = _indexer_scores(qi_ref[...], wi_ref[...], ki_s[...].astype(BF16), q_pos)


def _score_sample(qi, wi, ki_new, cache_idx_t, page_table, layer):
    b, n_pages = page_table.shape
    page_size = cache_idx_t.shape[3]
    s_len = n_pages * page_size + LANES
    per_b = lambda r, w: pl.BlockSpec((None, r, w), lambda i, pt: (i, 0, 0))
    grid_spec = pltpu.PrefetchScalarGridSpec(
        num_scalar_prefetch=1,
        grid=(b,),
        in_specs=[per_b(SAMPLE_ROWS, qi.shape[2]), per_b(SAMPLE_ROWS, LANES), per_b(D_IDX, LANES)]
        + _page_specs(layer, n_pages, D_IDX, page_size),
        out_specs=per_b(SAMPLE_ROWS, s_len),
        scratch_shapes=[pltpu.VMEM((D_IDX, s_len), F32)],
    )
    return pl.pallas_call(
        functools.partial(_score_sample_kernel, n_pages=n_pages, page_size=page_size),
        grid_spec=grid_spec,
        out_shape=jax.ShapeDtypeStruct((b, SAMPLE_ROWS, s_len), F32),
        compiler_params=_cparams(1),
        name="score_sample",
    )(page_table, qi, wi, ki_new, *([cache_idx_t] * n_pages))


def _select_kernel(score_ref, bias_ref, *, n_keep, dec_seq):
    rows = score_ref.shape[0]
    row = lax.broadcasted_iota(jnp.int32, (rows, 1), 0)
    _topk_mask(score_ref[...], n_keep, (row % SAMPLE_ROWS) < dec_seq, bias_ref)


def _select_sample(score, n_keep, dec_seq):
    n, s_len = score.shape
    tr = min(256, n)
    assert n % tr == 0 and tr % SAMPLE_ROWS == 0
    spec = pl.BlockSpec((tr, s_len), lambda i: (i, 0))
    return pl.pallas_call(
        functools.partial(_select_kernel, n_keep=n_keep, dec_seq=dec_seq),
        grid=(n // tr,),
        in_specs=[spec],
        out_specs=spec,
        out_shape=jax.ShapeDtypeStruct((n, s_len), F32),
        compiler_params=_cparams(1),
        name="select_sample",
    )(score)


SAMPLE_GROUP = 1


def _attend_sample_kernel(pt_ref, qa_ref, bias_ref, kan_ref, van_ref, qb_ref, kbn_ref, vbn_ref, *rest,
                          n_pages, page_size, n_sel):
    n_page_refs = 4 * SAMPLE_GROUP * n_pages
    ya_ref, yb_ref = rest[n_page_refs:n_page_refs + 2]
    scratch = rest[n_page_refs + 2:]
    past = n_pages * page_size
    n_full = past // MOBA_BLOCK
    hpk = N_HEADS_A // N_KV_A
    rows = hpk * SAMPLE_ROWS
    q_pos_g = past + lax.broadcasted_iota(jnp.int32, (rows, 1), 0) % SAMPLE_ROWS
    for g in range(SAMPLE_GROUP):
        ka_s, va_s, kb_s, vb_s, s_ref = scratch[5 * g:5 * g + 5]
        for k, (new_ref, dst) in enumerate(zip((kan_ref, van_ref, kbn_ref, vbn_ref), (ka_s, va_s, kb_s, vb_s))):
            first = (k * SAMPLE_GROUP + g) * n_pages
            _assemble_keys(rest[first:first + n_pages], new_ref.at[g], dst, page_size)
        qa, qb = qa_ref[g], qb_ref[g]
        bias_g = jnp.concatenate([bias_ref[g]] * hpk, axis=0)
        kb_t = kb_s[...]
        k_mean = _block_means(kb_t, n_full)
        for kv in range(N_KV_A):
            k_t = ka_s[kv * HEAD_DIM:(kv + 1) * HEAD_DIM, :].astype(BF16)
            s_ref[kv * rows:(kv + 1) * rows, :] = _dot(_stack_heads(qa, kv * hpk, hpk), k_t) + bias_g
            q_f = _moba_queries(qb, kv)
            picked = _moba_pick(q_f, k_mean[:GATE_ROWS, kv * HEAD_DIM:(kv + 1) * HEAD_DIM], n_full, n_sel, n_full)
            _moba_scores((q_f * SCORE_SCALE).astype(BF16), kb_t[kv * HEAD_DIM:(kv + 1) * HEAD_DIM].astype(BF16),
                         picked, q_pos_g, n_full, n_full, n_full,
                         s_ref.at[(N_KV_A + kv) * rows:(N_KV_A + kv + 1) * rows, :])
        s = s_ref[...]
        p = jnp.exp2(s - jnp.max(s, axis=1, keepdims=True))
        l = jnp.sum(p, axis=1, keepdims=True)
        p = p.astype(BF16)
        groups = ((va_s, ya_ref, 0), (va_s, ya_ref, 1), (vb_s, yb_ref, 0), (vb_s, yb_ref, 1))
        for idx, (v_s, out_ref, kv) in enumerate(groups):
            r0 = idx * rows
            v_t = v_s[kv * HEAD_DIM:(kv + 1) * HEAD_DIM, :].astype(BF16)
            o_g = _dot_nt(p[r0:r0 + rows], v_t) / l[r0:r0 + rows]
            _store_heads(o_g, SAMPLE_ROWS, kv * hpk, hpk, out_ref.at[g])


def _attend_sample(qa, bias, ka_new, va_new, qb, kb_new, vb_new, caches_t, page_table, layer, dec_seq):
    b, n_pages = page_table.shape
    page_size = caches_t[0].shape[3]
    past = n_pages * page_size
    assert past % MOBA_BLOCK == 0 and dec_seq <= SAMPLE_ROWS and b % SAMPLE_GROUP == 0
    s_len = past + LANES
    n_blocks = -(-(past + dec_seq) // MOBA_BLOCK)
    per_step = lambda r, w: pl.BlockSpec((SAMPLE_GROUP, r, w), lambda i, pt: (i, 0, 0))
    new = per_step(KV_ROWS, LANES)
    q = per_step(SAMPLE_ROWS, qa.shape[2])
    page_specs = [pl.BlockSpec((None, None, KV_ROWS, page_size),
                               lambda i, pt, g=g, j=j: (layer, pt[i * SAMPLE_GROUP + g, j], 0, 0))
                  for _ in range(4) for g in range(SAMPLE_GROUP) for j in range(n_pages)]
    grid_spec = pltpu.PrefetchScalarGridSpec(
        num_scalar_prefetch=1,
        grid=(b // SAMPLE_GROUP,),
        in_specs=[q, per_step(SAMPLE_ROWS, s_len), new, new, q, new, new] + page_specs,
        out_specs=[q, q],
        scratch_shapes=([pltpu.VMEM((KV_ROWS, s_len), F32)] * 4
                        + [pltpu.VMEM(((N_HEADS_A + N_HEADS_B) * SAMPLE_ROWS, s_len), F32)]) * SAMPLE_GROUP,
    )
    return pl.pallas_call(
        functools.partial(_attend_sample_kernel, n_pages=n_pages, page_size=page_size,
                          n_sel=min(MOBA_TOPK_MAX, n_blocks)),
        grid_spec=grid_spec,
        out_shape=[jax.ShapeDtypeStruct(qa.shape, F32)] * 2,
        compiler_params=_cparams(1),
        name="attend_sample",
    )(page_table, qa, bias, ka_new, va_new, qb, kb_new, vb_new,
      *[c for c in caches_t for _ in range(SAMPLE_GROUP * n_pages)])


ROUTE_E1, ROUTE_E2, ROUTE_C1, ROUTE_C2 = 0, 1, 2, 3
SUBLANES = 8


def _store_token_tiles(x, ref, lead=()):
    t = x.shape[0]
    for j in range(x.shape[1] // LANES):
        ref[lead + (pl.ds(j, t, stride=SUBLANES), slice(None))] = x[:, j * LANES:(j + 1) * LANES]


def _load_token_tiles(ref, t, lead=()):
    return jnp.concatenate([ref[lead + (pl.ds(j, t, stride=SUBLANES), slice(None))] for j in range(SUBLANES)], axis=1)


def _mixout_kernel(ya_ref, yb_ref, ga_ref, gb_ref, x_ref, gt1_ref, sc2_ref, sh2_ref, g_ref,
                   wpa_ref, wpb_ref, wout_ref, wr_ref, br_ref, x1_ref, h2_ref, route_ref):
    merged = (ga_ref[...].astype(F32) * _dot(ya_ref[...], wpa_ref[...])
              + gb_ref[...].astype(F32) * _dot(yb_ref[...], wpb_ref[...]))
    mix = _dot(merged.astype(BF16), wout_ref[...])
    x1 = x_ref[...] + gt1_ref[...] * mix
    x1_ref[...] = x1
    h2 = x1 * lax.rsqrt(jnp.mean(x1 * x1, axis=-1, keepdims=True) + RMS_EPS) * g_ref[...]
    h2 = h2 * (1.0 + sc2_ref[...]) + sh2_ref[...]
    _store_token_tiles(h2, h2_ref)

    logits = _dot(h2, wr_ref[...], precision=HIGHEST) + br_ref[...]
    lane = lax.broadcasted_iota(jnp.int32, logits.shape, 1)
    lane_f = lane.astype(F32)
    is_grp = lane < N_GROUPS
    grp = jnp.where(is_grp, logits, NEG_INF)
    g_max = jnp.max(grp, axis=1, keepdims=True)
    g_star = jnp.min(jnp.where(grp == g_max, lane_f, float(N_GROUPS)), axis=1, keepdims=True)
    p_g = 1.0 / jnp.sum(jnp.where(is_grp, jnp.exp(grp - g_max), 0.0), axis=1, keepdims=True)
    e_id = lane_f - float(N_GROUPS)
    in_grp = (e_id >= g_star * EXPERTS_PER_GROUP) & (e_id < (g_star + 1.0) * EXPERTS_PER_GROUP)
    cand = jnp.where(in_grp, logits, NEG_INF)
    l1 = jnp.max(cand, axis=1, keepdims=True)
    e1 = jnp.min(jnp.where(cand == l1, e_id, float(N_EXPERTS)), axis=1, keepdims=True)
    cand = jnp.where(e_id == e1, NEG_INF, cand)
    l2 = jnp.max(cand, axis=1, keepdims=True)
    e2 = jnp.min(jnp.where(cand == l2, e_id, float(N_EXPERTS)), axis=1, keepdims=True)
    t = jnp.exp(l2 - l1)
    c1 = p_g / (1.0 + t)
    c2 = p_g * t / (1.0 + t)
    route = jnp.where(lane == ROUTE_E1, e1, 0.0)
    route = jnp.where(lane == ROUTE_E2, e2, route)
    route = jnp.where(lane == ROUTE_C1, c1, route)
    route_ref[...] = jnp.where(lane == ROUTE_C2, c2, route)


def _mixout(ya, yb, ga, gb, x, layer, mod, mod_spec, g_ffn, w_pa, w_pb, w_out, w_r, b_r, tm):
    n, d = x.shape
    row = lambda width: pl.BlockSpec((tm, width), lambda i: (i, 0))
    layer_w = lambda a: pl.BlockSpec((None,) + a.shape[1:], lambda i: (layer, 0, 0))
    return pl.pallas_call(
        _mixout_kernel,
        grid=(n // tm,),
        in_specs=[row(ya.shape[1]), row(yb.shape[1]), row(d), row(d), row(d),
                  mod_spec(2), mod_spec(4), mod_spec(3), pl.BlockSpec((None, 1, d), lambda i: (layer, 0, 0)),
                  layer_w(w_pa), layer_w(w_pb), layer_w(w_out), layer_w(w_r), layer_w(b_r)],
        out_specs=[row(d), pl.BlockSpec((tm * SUBLANES, LANES), lambda i: (i, 0)), row(LANES)],
        out_shape=[jax.ShapeDtypeStruct((n, d), F32), jax.ShapeDtypeStruct((n * SUBLANES, LANES), F32),
                   jax.ShapeDtypeStruct((n, LANES), F32)],
        compiler_params=_cparams(1),
        name="mixout",
    )(ya, yb, ga, gb, x, mod, mod, mod, g_ffn.reshape(g_ffn.shape[0], 1, d), w_pa, w_pb, w_out, w_r, b_r)


MOE_ROWS = 256
ROW_CHUNK = 32
N_BUF = 2


def _experts_kernel(blk_e_ref, n_valid_ref, src_ref, src_row_ref, dst_row_ref, h_hbm, wg_ref, wu_ref, wd_ref, y_hbm,
                    x_buf, y_buf, gather_sem, scatter_sem, *, n_tokens, n_blocks):
    i = pl.program_id(0)
    buf = i % N_BUF
    n_slots = TOPK_IN_GROUP * n_tokens

    def n_chunks(b):
        return jnp.right_shift(n_valid_ref[b] + (ROW_CHUNK - 1), ROW_CHUNK.bit_length() - 1)

    def sorted_index(b, r):
        return jnp.minimum(src_ref[b] + r, n_slots - 1)

    chunk_rows = ROW_CHUNK * SUBLANES

    def row_view(buf_ref, bf, c, u):
        return buf_ref.at[bf, pl.ds(pl.multiple_of(c * chunk_rows, chunk_rows) + u * SUBLANES, SUBLANES)]

    def chunk_view(buf_ref, bf, c):
        return buf_ref.at[bf, pl.ds(pl.multiple_of(c * chunk_rows, chunk_rows), chunk_rows)]

    def token_tile(hbm_ref, first_row):
        return hbm_ref.at[pl.ds(pl.multiple_of(first_row, SUBLANES), SUBLANES)]

    def for_rows(b, fn):
        def chunk(c, carry):
            for u in range(ROW_CHUNK):
                fn(c, u)
            return carry
        lax.fori_loop(0, n_chunks(b), chunk, 0)

    def wait_chunks(b, buf_ref, bf, sem):
        def chunk(c, carry):
            view = chunk_view(buf_ref, bf, c)
            pltpu.make_async_copy(view, view, sem.at[bf]).wait()
            return carry
        lax.fori_loop(0, n_chunks(b), chunk, 0)

    def start_gather(b, bf):
        def row(c, u):
            first_row = src_row_ref[sorted_index(b, c * ROW_CHUNK + u)]
            pltpu.make_async_copy(token_tile(h_hbm, first_row), row_view(x_buf, bf, c, u), gather_sem.at[bf]).start()
        for_rows(b, row)

    def start_scatter(b, bf):
        n_valid = n_valid_ref[b]

        def row(c, u):
            r = c * ROW_CHUNK + u
            first_row = jnp.where(r < n_valid, dst_row_ref[sorted_index(b, r)],
                                  (n_slots + bf * MOE_ROWS + r) * SUBLANES)
            pltpu.make_async_copy(row_view(y_buf, bf, c, u), token_tile(y_hbm, first_row), scatter_sem.at[bf]).start()
        for_rows(b, row)

    @pl.when(i == 0)
    def _():
        x_buf[...] = jnp.zeros_like(x_buf)
        y_buf[...] = jnp.zeros_like(y_buf)
        for bf in range(N_BUF):
            spare = y_hbm.at[pl.ds((n_slots + bf * MOE_ROWS) * SUBLANES, MOE_ROWS * SUBLANES)]
            init = pltpu.make_async_copy(y_buf.at[bf], spare, scatter_sem.at[bf])
            init.start()
            init.wait()
        start_gather(0, 0)

    @pl.when(i + 1 < n_blocks)
    def _():
        start_gather(i + 1, 1 - buf)

    wait_chunks(i, x_buf, buf, gather_sem)

    @pl.when(n_valid_ref[i] > 0)
    def _():
        xb = _load_token_tiles(x_buf, MOE_ROWS, (buf,)).astype(BF16)
        gate = _dot(xb, wg_ref[...].astype(BF16))
        hidden = gate * _sigmoid(gate) * _dot(xb, wu_ref[...].astype(BF16))
        _store_token_tiles(_dot(hidden.astype(BF16), wd_ref[...].astype(BF16)), y_buf, (buf,))

    start_scatter(i, buf)

    @pl.when(i > 0)
    def _():
        wait_chunks(i - 1, y_buf, 1 - buf, scatter_sem)

    @pl.when(i == n_blocks - 1)
    def _():
        wait_chunks(i, y_buf, buf, scatter_sem)


def _experts(h2, expert_ids, layer, w_gate, w_up, w_down):
    n = expert_ids.shape[0]
    assert h2.shape == (n * SUBLANES, LANES) and w_gate.shape[2] == SUBLANES * LANES
    m = TOPK_IN_GROUP * n
    n_blocks = -(-m // MOE_ROWS) + N_EXPERTS
    flat_e = expert_ids.T.reshape(m)
    order = jnp.argsort(flat_e).astype(jnp.int32)
    experts = jnp.arange(N_EXPERTS, dtype=jnp.int32)
    counts = jnp.sum((flat_e[:, None] == experts[None, :]).astype(jnp.int32), axis=0)
    padded = (counts + MOE_ROWS - 1) // MOE_ROWS * MOE_ROWS
    pad_end = jnp.cumsum(padded)
    pad_start = pad_end - padded
    start = jnp.cumsum(counts) - counts
    block_start = jnp.arange(n_blocks, dtype=jnp.int32) * MOE_ROWS
    blk_e = jnp.minimum(jnp.sum((pad_end[None, :] <= block_start[:, None]).astype(jnp.int32), axis=1), N_EXPERTS - 1)
    offset = block_start - pad_start[blk_e]
    n_valid = jnp.clip(counts[blk_e] - offset, 0, MOE_ROWS).astype(jnp.int32)
    src = jnp.clip(start[blk_e] + offset, 0, m - 1).astype(jnp.int32)
    blk_e = jnp.where(n_valid > 0, blk_e, jnp.max(jnp.where(n_valid > 0, blk_e, 0))).astype(jnp.int32)

    src_row = jnp.where(order >= n, order - n, order) * SUBLANES
    dst_row = order * SUBLANES

    w_spec = lambda a: pl.BlockSpec((None, None) + a.shape[2:], lambda i, be, nv, sr, s8, d8: (layer, be[i], 0, 0))
    grid_spec = pltpu.PrefetchScalarGridSpec(
        num_scalar_prefetch=5,
        grid=(n_blocks,),
        in_specs=[pl.BlockSpec(memory_space=pl.ANY), w_spec(w_gate), w_spec(w_up), w_spec(w_down)],
        out_specs=pl.BlockSpec(memory_space=pl.ANY),
        scratch_shapes=[pltpu.VMEM((N_BUF, MOE_ROWS * SUBLANES, LANES), F32)] * 2
        + [pltpu.SemaphoreType.DMA((N_BUF,))] * 2,
    )
    return pl.pallas_call(
        functools.partial(_experts_kernel, n_tokens=n, n_blocks=n_blocks),
        grid_spec=grid_spec,
        out_shape=jax.ShapeDtypeStruct(((m + N_BUF * MOE_ROWS) * SUBLANES, LANES), F32),
        compiler_params=_cparams(1),
        name="experts",
    )(blk_e, n_valid, src, src_row, dst_row, h2, w_gate, w_up, w_down)


def _combine_kernel(x1_ref, y0_ref, y1_ref, route_ref, gt2_ref, g_ref, o_ref, *, final_norm):
    route = route_ref[...]
    t = route.shape[0]
    moe = (route[:, ROUTE_C1:ROUTE_C1 + 1] * _load_token_tiles(y0_ref, t)
           + route[:, ROUTE_C2:ROUTE_C2 + 1] * _load_token_tiles(y1_ref, t))
    x2 = x1_ref[...] + gt2_ref[...] * moe
    if final_norm:
        x2 = x2 * lax.rsqrt(jnp.mean(x2 * x2, axis=-1, keepdims=True) + RMS_EPS) * g_ref[...]
    o_ref[...] = x2


def _combine(x1, y, route, mod, mod_spec, g_final, tm, final_norm):
    n, d = x1.shape
    nt = n // tm
    row = lambda width: pl.BlockSpec((tm, width), lambda i: (i, 0))
    tiles = lambda first: pl.BlockSpec((tm * SUBLANES, LANES), lambda i: (first + i, 0))
    return pl.pallas_call(
        functools.partial(_combine_kernel, final_norm=final_norm),
        grid=(nt,),
        in_specs=[row(d), tiles(0), tiles(nt), row(LANES),
                  mod_spec(5), pl.BlockSpec((1, d), lambda i: (0, 0))],
        out_specs=row(d),
        out_shape=jax.ShapeDtypeStruct((n, d), F32),
        compiler_params=_cparams(1),
        name="combine",
    )(x1, y, y, route, mod, g_final.reshape(1, d))


def _rope_tables(pos):
    half = HEAD_DIM // 2
    inv = jnp.power(ROPE_THETA, -jnp.arange(half, dtype=F32) / half)
    ang = pos.astype(F32)[:, None] * inv[None, :]
    cos = jnp.tile(jnp.cos(ang), (1, LANES // half))
    sin = jnp.tile(jnp.sin(ang), (1, LANES // half))
    return cos, sin, cos.T, sin.T


def _sample_rows(a, batch, t):
    return jnp.pad(a.astype(F32).reshape(batch, t, a.shape[-1]), ((0, 0), (0, SAMPLE_ROWS - t), (0, 0)))


def _sample_new_keys(a_t, batch, t):
    r = a_t.shape[1]
    per_b = jnp.transpose(a_t.reshape(r, batch, t), (1, 0, 2))
    return jnp.pad(per_b, ((0, 0), (0, 0), (0, LANES - t)))


def _cache_t(cache):
    depth, n_pool, page = cache.shape[:3]
    return jnp.swapaxes(cache.reshape(depth, n_pool, page, -1), 2, 3)


def kernel(x_prompt, x_sample, c_prompt, c_sample, cache_k_a, cache_v_a, cache_idx_k, cache_k_b, cache_v_b, page_table, w_ada, b_ada, g_mix, w_in, w_proj_a, w_proj_b, w_out, g_ffn, w_router_group, b_router_group, w_router_expert, b_router_expert, w_exp_gate, w_exp_up, w_exp_down, g_final):
    batch, seq, d = x_prompt.shape
    dec_batch, dec_seq, _ = x_sample.shape
    depth = w_in.shape[0]
    past_len = page_table.shape[1] * cache_k_a.shape[2]
    n_p, n_s = batch * seq, dec_batch * dec_seq
    tm_p = min(512, seq)
    tm_s = min(256, n_s)
    assert seq % tm_p == 0 and n_s % tm_s == 0

    mod_all = _ada_mod(jnp.concatenate([c_prompt, c_sample], axis=0), w_ada, b_ada)
    mod_p = mod_all[:, :batch].reshape(depth, batch, 1, 6 * d)
    mod_s = jnp.repeat(mod_all[:, batch:], dec_seq, axis=1)
    w_row, w_col = _split_w_in(w_in)
    w_pa, w_pb, w_o = w_proj_a.astype(BF16), w_proj_b.astype(BF16), w_out.astype(BF16)
    pad_r = LANES - N_GROUPS - N_EXPERTS
    w_r = jnp.pad(jnp.concatenate([w_router_group, w_router_expert], axis=2), ((0, 0), (0, 0), (0, pad_r)))
    b_r = jnp.pad(jnp.concatenate([b_router_group, b_router_expert], axis=1), ((0, 0), (0, pad_r)))
    b_r = b_r.reshape(depth, 1, LANES)
    cache_idx_t = _cache_t(cache_idx_k)
    caches_t = tuple(_cache_t(c) for c in (cache_k_a, cache_v_a, cache_k_b, cache_v_b))

    trig_p = _rope_tables(jnp.arange(seq, dtype=jnp.int32))
    trig_s = _rope_tables(jnp.tile(past_len + jnp.arange(dec_seq, dtype=jnp.int32), dec_batch))
    nq_p = seq // tm_p
    trig_spec_p = pl.BlockSpec((tm_p, LANES), lambda i: (i % nq_p, 0))
    trig_t_spec_p = pl.BlockSpec((LANES, tm_p), lambda i: (0, i % nq_p))
    trig_spec_s = pl.BlockSpec((tm_s, LANES), lambda i: (i, 0))
    trig_t_spec_s = pl.BlockSpec((LANES, tm_s), lambda i: (0, i))

    xp = x_prompt.reshape(n_p, d)
    xs = x_sample.reshape(n_s, d)
    rows_p, rows_s = [], []
    for l in range(depth):
        mod_spec_p = lambda chunk, l=l: pl.BlockSpec((None, None, 1, d), lambda i: (l, i // nq_p, 0, chunk))
        mod_spec_s = lambda chunk, l=l: pl.BlockSpec((None, tm_s, d), lambda i: (l, i, chunk))
        last = l == depth - 1

        qa, qi, wi, qb, ga, gb, ka_t, va_t, ki_t, kb_t, vb_t = _inproj(
            xp, l, mod_p, mod_spec_p, g_mix, w_row, w_col, trig_p, trig_spec_p, trig_t_spec_p, batch, tm_p, BF16)
        rows_p.append((ka_t, va_t, ki_t, kb_t, vb_t))
        ya = _dsa_prompt(qi, wi, qa, ka_t, va_t, min(PROMPT_TQ, seq))
        yb = _moba_prompt(qb, kb_t, vb_t, min(PROMPT_TQ, seq))
        x1, h2, route = _mixout(ya, yb, ga, gb, xp, l, mod_p, mod_spec_p, g_ffn, w_pa, w_pb, w_o, w_r, b_r, tm_p)
        y = _experts(h2, route[:, ROUTE_E1:ROUTE_E2 + 1].astype(jnp.int32), l, w_exp_gate, w_exp_up, w_exp_down)
        xp = _combine(x1, y, route, mod_p, mod_spec_p, g_final, tm_p, last)

        qa, qi, wi, qb, ga, gb, ka_t, va_t, ki_t, kb_t, vb_t = _inproj(
            xs, l, mod_s, mod_spec_s, g_mix, w_row, w_col, trig_s, trig_spec_s, trig_t_spec_s, 1, tm_s, F32)
        rows_s.append((ka_t, va_t, ki_t, kb_t, vb_t))
        rows8 = lambda a: _sample_rows(a, dec_batch, dec_seq)
        new = lambda a_t: _sample_new_keys(a_t, dec_batch, dec_seq)
        score = _score_sample(rows8(qi), rows8(wi), new(ki_t), cache_idx_t, page_table, l)
        s_len = score.shape[-1]
        bias = _select_sample(score.reshape(dec_batch * SAMPLE_ROWS, s_len),
                              min(IDX_TOPK_MAX, (past_len + dec_seq) // 4), dec_seq)
        ya, yb = _attend_sample(rows8(qa), bias.reshape(dec_batch, SAMPLE_ROWS, s_len), new(ka_t), new(va_t),
                                rows8(qb), new(kb_t), new(vb_t), caches_t, page_table, l, dec_seq)
        unpad = lambda a: a[:, :dec_seq].reshape(n_s, a.shape[-1]).astype(BF16)
        x1, h2, route = _mixout(unpad(ya), unpad(yb), ga, gb, xs, l, mod_s, mod_spec_s, g_ffn, w_pa, w_pb, w_o,
                                w_r, b_r, tm_s)
        y = _experts(h2, route[:, ROUTE_E1:ROUTE_E2 + 1].astype(jnp.int32), l, w_exp_gate, w_exp_up, w_exp_down)
        xs = _combine(x1, y, route, mod_s, mod_spec_s, g_final, tm_s, last)

    def stack_p(i, heads):
        a = jnp.stack([r[i] for r in rows_p], axis=0)
        a = jnp.swapaxes(a, 2, 3)
        return a.reshape((depth, batch, seq, heads, HEAD_DIM) if heads else (depth, batch, seq, D_IDX))

    def stack_s(i, heads):
        a = jnp.stack([r[i][0] for r in rows_s], axis=0)
        a = jnp.swapaxes(a, 1, 2)
        return a.reshape((depth, dec_batch, dec_seq, heads, HEAD_DIM) if heads else (depth, dec_batch, dec_seq, D_IDX))

    return (xp.reshape(batch, seq, d), xs.reshape(dec_batch, dec_seq, d),
            stack_p(0, N_KV_A), stack_p(1, N_KV_A), stack_p(3, N_KV_B), stack_p(4, N_KV_B), stack_p(2, 0),
            stack_s(0, N_KV_A), stack_s(1, N_KV_A), stack_s(3, N_KV_B), stack_s(4, N_KV_B), stack_s(2, 0))
```

```python
import functools

import jax
import jax.numpy as jnp
from jax import lax
from jax.experimental import pallas as pl
from jax.experimental.pallas import tpu as pltpu

HEAD_DIM = 64
N_HEADS_A = 8
N_KV_A = 2
N_IDX_HEADS = 4
D_IDX = 64
IDX_TOPK_MAX = 256
N_HEADS_B = 8
N_KV_B = 2
MOBA_BLOCK = 256
MOBA_TOPK_MAX = 3
N_GROUPS = 4
EXPERTS_PER_GROUP = 8
N_EXPERTS = N_GROUPS * EXPERTS_PER_GROUP
TOPK_IN_GROUP = 2
ROPE_THETA = 10000.0
RMS_EPS = 1e-6

LANES = 128
KV_ROWS = N_KV_A * HEAD_DIM
assert KV_ROWS == LANES == N_KV_B * HEAD_DIM and D_IDX == HEAD_DIM and TOPK_IN_GROUP == 2

VMEM_LIMIT = 56 * 1024 * 1024

F32 = jnp.float32
BF16 = jnp.bfloat16
NEG_INF = float("-inf")
INT_MIN = -(2 ** 31)
NEG_INF_KEY = -2139095041
HIGHEST = lax.Precision.HIGHEST
SCORE_SCALE = HEAD_DIM ** -0.5 * 1.4426950408889634


def _cparams(n_grid):
    return pltpu.CompilerParams(dimension_semantics=("arbitrary",) * n_grid, vmem_limit_bytes=VMEM_LIMIT)


def _dot(a, b, precision=None):
    return jnp.dot(a, b, preferred_element_type=F32, precision=precision)


def _dot_nt(a, b, precision=None):
    return lax.dot_general(a, b, (((1,), (1,)), ((), ())), preferred_element_type=F32, precision=precision)


def _sigmoid(x):
    return 1.0 / (1.0 + jnp.exp(-x))


def _ada_kernel(c_ref, w_ref, b_ref, o_ref):
    c = c_ref[...]
    o_ref[...] = _dot((c * _sigmoid(c)).astype(BF16), w_ref[...].astype(BF16)) + b_ref[...]


def _ada_mod(c_all, w_ada, b_ada):
    depth, d, n6 = w_ada.shape
    m = c_all.shape[0]
    tn = 1024
    return pl.pallas_call(
        _ada_kernel,
        grid=(depth, n6 // tn),
        in_specs=[
            pl.BlockSpec((m, d), lambda l, j: (0, 0)),
            pl.BlockSpec((None, d, tn), lambda l, j: (l, 0, j)),
            pl.BlockSpec((None, 1, tn), lambda l, j: (l, 0, j)),
        ],
        out_specs=pl.BlockSpec((None, m, tn), lambda l, j: (l, 0, j)),
        out_shape=jax.ShapeDtypeStruct((depth, m, n6), F32),
        compiler_params=_cparams(2),
        name="ada_mod",
    )(c_all, w_ada, b_ada.reshape(depth, 1, n6))


_QA_W = N_HEADS_A * HEAD_DIM
_QI_W = N_IDX_HEADS * D_IDX
_QB_W = N_HEADS_B * HEAD_DIM
_ROW_QA, _ROW_QI, _ROW_WI, _ROW_QB = 0, _QA_W, _QA_W + _QI_W, _QA_W + _QI_W + LANES
_ROW_GATE = _ROW_QB + _QB_W
_COL_KA, _COL_VA, _COL_KI, _COL_KB, _COL_VB = 0, KV_ROWS, 2 * KV_ROWS, 2 * KV_ROWS + D_IDX, 3 * KV_ROWS + D_IDX
_COL_END = 4 * KV_ROWS + D_IDX


def _rope_lanes(y, cos, sin, lane):
    first_half = (lane % HEAD_DIM) < (HEAD_DIM // 2)
    rot = jnp.where(first_half, -pltpu.roll(y, LANES - HEAD_DIM // 2, 1), pltpu.roll(y, HEAD_DIM // 2, 1))
    return y * cos + rot * sin


def _rope_rows(y, cos, sin):
    half = HEAD_DIM // 2
    parts = []
    for r0 in range(0, y.shape[0], HEAD_DIM):
        parts += [-y[r0 + half:r0 + HEAD_DIM], y[r0:r0 + half]]
    rot = jnp.concatenate(parts, axis=0)
    return y * cos[:y.shape[0]] + rot * sin[:y.shape[0]]


def _inproj_kernel(x_ref, sc_ref, sh_ref, g_ref, wr_ref, wc_ref, cos_ref, sin_ref, cos_t_ref, sin_t_ref,
                   qa_ref, qi_ref, wi_ref, qb_ref, ga_ref, gb_ref, ka_ref, va_ref, ki_ref, kb_ref, vb_ref,
                   *, d_model):
    x = x_ref[...]
    h = x * lax.rsqrt(jnp.mean(x * x, axis=-1, keepdims=True) + RMS_EPS) * g_ref[...]
    h = h * (1.0 + sc_ref[...]) + sh_ref[...]
    hb = h.astype(BF16)
    cos = cos_ref[...]
    sin = sin_ref[...]
    lane = lax.broadcasted_iota(jnp.int32, cos.shape, 1)

    def proj(c0, width):
        return _dot(hb, wr_ref[:, c0:c0 + width])

    def roped(c0, out_ref, scale=None):
        for c in range(out_ref.shape[1] // LANES):
            y = _rope_lanes(proj(c0 + c * LANES, LANES), cos, sin, lane)
            if scale is not None:
                y = y * scale
            out_ref[:, c * LANES:(c + 1) * LANES] = y.astype(out_ref.dtype)

    roped(_ROW_QA, qa_ref, SCORE_SCALE)
    roped(_ROW_QI, qi_ref)
    wik = proj(_ROW_WI, LANES)
    wi_ref[...] = jnp.where(lane >= D_IDX, _rope_lanes(wik, cos, sin, lane), wik)
    roped(_ROW_QB, qb_ref)
    for c in range(d_model // 512):
        ga_ref[:, c * 512:(c + 1) * 512] = _sigmoid(proj(_ROW_GATE + c * 512, 512)).astype(ga_ref.dtype)
        gb_ref[:, c * 512:(c + 1) * 512] = _sigmoid(proj(_ROW_GATE + d_model + c * 512, 512)).astype(gb_ref.dtype)

    cos_t = cos_t_ref[...]
    sin_t = sin_t_ref[...]

    def proj_t(r0, r1):
        return _dot_nt(wc_ref[r0:r1, :], hb)

    ka_ref[...] = _rope_rows(proj_t(_COL_KA, _COL_VA), cos_t, sin_t)
    va_ref[...] = proj_t(_COL_VA, _COL_KI)
    ki_ref[...] = _rope_rows(proj_t(_COL_KI, _COL_KB), cos_t, sin_t)
    kb_ref[...] = _rope_rows(proj_t(_COL_KB, _COL_VB), cos_t, sin_t)
    vb_ref[...] = proj_t(_COL_VB, _COL_END)


def _split_w_in(w_in):
    depth, d, d_in = w_in.shape
    widths = (_QA_W, KV_ROWS, KV_ROWS, _QI_W, D_IDX, N_IDX_HEADS, _QB_W, KV_ROWS, KV_ROWS, d, d)
    offs = [0]
    for w in widths:
        offs.append(offs[-1] + w)
    assert offs[-1] == d_in
    w_t = jnp.transpose(w_in, (2, 0, 1))
    qa, ka, va, qi, ki, wi, qb, kb, vb, ga, gb = [w_t[offs[i]:offs[i + 1]] for i in range(11)]
    pad = jnp.zeros((LANES - N_IDX_HEADS - D_IDX, depth, d), w_in.dtype)
    w_row = jnp.transpose(jnp.concatenate([qa, qi, wi, pad, ki, qb, ga, gb], axis=0).astype(BF16), (1, 2, 0))
    w_col = jnp.transpose(jnp.concatenate([ka, va, ki, kb, vb], axis=0).astype(BF16), (1, 0, 2))
    return w_row, w_col


def _inproj(x, layer, mod, mod_spec, g_mix, w_row, w_col, trig, trig_spec, trig_t_spec, n_batch, tm, q_dtype):
    n, d = x.shape
    per_batch = n // n_batch
    nq = per_batch // tm
    cos, sin, cos_t, sin_t = trig
    row = lambda width: pl.BlockSpec((tm, width), lambda i: (i, 0))
    col = lambda rows: pl.BlockSpec((None, rows, tm), lambda i: (i // nq, 0, i % nq))
    row_out = [(_QA_W, q_dtype), (_QI_W, q_dtype), (LANES, F32), (_QB_W, F32), (d, q_dtype), (d, q_dtype)]
    col_out = [KV_ROWS, KV_ROWS, D_IDX, KV_ROWS, KV_ROWS]
    layer_w = lambda a: pl.BlockSpec((None,) + a.shape[1:], lambda i: (layer, 0, 0))
    return pl.pallas_call(
        functools.partial(_inproj_kernel, d_model=d),
        grid=(n // tm,),
        in_specs=[row(d), mod_spec(1), mod_spec(0), pl.BlockSpec((None, 1, d), lambda i: (layer, 0, 0)),
                  layer_w(w_row), layer_w(w_col), trig_spec, trig_spec, trig_t_spec, trig_t_spec],
        out_specs=[row(w) for w, _ in row_out] + [col(r) for r in col_out],
        out_shape=[jax.ShapeDtypeStruct((n, w), dt) for w, dt in row_out]
        + [jax.ShapeDtypeStruct((n_batch, r, per_batch), F32) for r in col_out],
        compiler_params=_cparams(1),
        name="inproj",
    )(x, mod, mod, g_mix.reshape(g_mix.shape[0], 1, d), w_row, w_col, cos, sin, cos_t, sin_t)


def _stack_heads(q, first_head, n_heads, scale=None):
    q = q.astype(F32)
    if scale is not None:
        q = q * scale
    parts = [q[:, (first_head + g) * HEAD_DIM:(first_head + g + 1) * HEAD_DIM] for g in range(n_heads)]
    return jnp.concatenate(parts, axis=0).astype(BF16)


def _sort_key(x):
    bits = lax.bitcast_convert_type(x, jnp.int32)
    return bits ^ ((bits >> 31) & 0x7FFFFFFF)


def _count(mask):
    return jnp.sum(jnp.where(mask, 1.0, 0.0), axis=1, keepdims=True)


def _indexer_scores(qi, wi, ki_t, q_pos):
    t = qi.shape[0]
    logits = _dot(_stack_heads(qi, 0, N_IDX_HEADS), ki_t)
    score = None
    for h in range(N_IDX_HEADS):
        term = jnp.maximum(logits[h * t:(h + 1) * t], 0.0) * wi[:, h:h + 1]
        score = term if score is None else score + term
    k_pos = lax.broadcasted_iota(jnp.int32, score.shape, 1)
    score = jnp.where(score == 0.0, 0.0, score)
    return jnp.where(k_pos <= q_pos, score, NEG_INF)


def _topk_mask(score, n_keep, row_is_real, bias_ref):
    t, s_len = score.shape
    key = _sort_key(score)
    keep = float(n_keep)
    thr = jnp.where(_count(key >= 0) >= keep, 0, INT_MIN).astype(jnp.int32)

    def two_bits(i, thr):
        hi = jnp.left_shift(jnp.int32(1), 30 - 2 * i)
        lo = jnp.left_shift(jnp.int32(1), 29 - 2 * i)
        c_hi, c_lo, c_both = thr + hi, thr + lo, thr + hi + lo
        n_hi, n_lo, n_both = _count(key >= c_hi), _count(key >= c_lo), _count(key >= c_both)
        return jnp.where(n_both >= keep, c_both, jnp.where(n_hi >= keep, c_hi, jnp.where(n_lo >= keep, c_lo, thr)))

    thr = lax.fori_loop(0, 15, two_bits, thr)
    thr = jnp.where(_count(key >= thr + 1) >= keep, thr + 1, thr)
    at_least = key >= thr
    n_above = _count(key > thr)
    bias_ref[...] = jnp.where(at_least & (key > NEG_INF_KEY), 0.0, NEG_INF)

    tied = (_count(at_least) > keep) & (thr > NEG_INF_KEY) & row_is_real

    @pl.when(jnp.max(jnp.where(tied, 1.0, 0.0)) > 0.0)
    def _():
        free = keep - n_above
        r_i = lax.broadcasted_iota(jnp.int32, (MOBA_BLOCK, MOBA_BLOCK), 0)
        c_i = lax.broadcasted_iota(jnp.int32, (MOBA_BLOCK, MOBA_BLOCK), 1)
        tri = jnp.where(r_i <= c_i, 1.0, 0.0).astype(BF16)
        seen = jnp.zeros((t, 1), F32)
        for c0 in range(0, s_len, MOBA_BLOCK):
            w = min(MOBA_BLOCK, s_len - c0)
            key_c = key[:, c0:c0 + w]
            eq_c = jnp.where(key_c == thr, 1.0, 0.0)
            rank = _dot(eq_c.astype(BF16), tri[:w, :w]) + seen
            keep_c = ((key_c > thr) | ((key_c == thr) & (rank <= free))) & (key_c > NEG_INF_KEY)
            bias_ref[:, c0:c0 + w] = jnp.where(keep_c, 0.0, NEG_INF)
            seen = seen + jnp.sum(eq_c, axis=1, keepdims=True)


def _indexer_scores_t(kik, kiw_q, qi, q_pos_t):
    t = qi.shape[0]
    q = qi.astype(F32)
    lane = lax.broadcasted_iota(jnp.int32, (t, LANES), 1)
    parts = []
    for h in range(N_IDX_HEADS):
        pair = q[:, (h // 2) * LANES:(h // 2 + 1) * LANES]
        if h % 2:
            parts.append(jnp.where(lane >= D_IDX, pair, 0.0))
        else:
            parts.append(pltpu.roll(jnp.where(lane < D_IDX, pair, 0.0), D_IDX, 1))
    logits = _dot_nt(kik, jnp.concatenate(parts, axis=0).astype(BF16))
    wi_t = kiw_q.T
    score = None
    for h in range(N_IDX_HEADS):
        term = jnp.maximum(logits[:, h * t:(h + 1) * t], 0.0) * wi_t[h:h + 1, :]
        score = term if score is None else score + term
    k_pos = lax.broadcasted_iota(jnp.int32, score.shape, 0)
    score = jnp.where(score == 0.0, 0.0, score)
    return jnp.where(k_pos <= q_pos_t, score, NEG_INF)


COUNT_SLAB = 64


def _count_t(mask):
    ones = jnp.where(mask, 1.0, 0.0)
    s_len, t = ones.shape
    if s_len % COUNT_SLAB == 0 and s_len > COUNT_SLAB:
        ones = jnp.sum(ones.reshape(s_len // COUNT_SLAB, COUNT_SLAB, t), axis=0)
    return jnp.sum(ones, axis=0, keepdims=True)


def _store_transposed(mask_t, c0, bias_ref):
    for j in range(0, mask_t.shape[0], LANES):
        bias_ref[:, c0 + j:c0 + j + LANES] = mask_t[j:j + LANES, :].T


def _topk_mask_t(score_t, n_keep, bias_ref):
    s_len, t = score_t.shape
    key = _sort_key(score_t)
    keep = float(n_keep)
    thr = jnp.where(_count_t(key >= 0) >= keep, 0, INT_MIN).astype(jnp.int32)

    def bit_step(i, thr):
        cand = thr + jnp.left_shift(jnp.int32(1), 30 - i)
        return jnp.where(_count_t(key >= cand) >= keep, cand, thr)

    thr = lax.fori_loop(0, 31, bit_step, thr)
    at_least = key >= thr
    n_above = _count_t(key > thr)
    _store_transposed(jnp.where(at_least & (key > NEG_INF_KEY), 0.0, NEG_INF), 0, bias_ref)

    tied = (_count_t(at_least) > keep) & (thr > NEG_INF_KEY)

    @pl.when(jnp.max(jnp.where(tied, 1.0, 0.0)) > 0.0)
    def _():
        free = keep - n_above
        r_i = lax.broadcasted_iota(jnp.int32, (MOBA_BLOCK, MOBA_BLOCK), 0)
        c_i = lax.broadcasted_iota(jnp.int32, (MOBA_BLOCK, MOBA_BLOCK), 1)
        tri = jnp.where(c_i <= r_i, 1.0, 0.0).astype(BF16)
        seen = jnp.zeros((1, t), F32)
        for c0 in range(0, s_len, MOBA_BLOCK):
            w = min(MOBA_BLOCK, s_len - c0)
            key_c = key[c0:c0 + w, :]
            eq_c = jnp.where(key_c == thr, 1.0, 0.0)
            rank = _dot(tri[:w, :w], eq_c.astype(BF16)) + seen
            keep_c = ((key_c > thr) | ((key_c == thr) & (rank <= free))) & (key_c > NEG_INF_KEY)
            _store_transposed(jnp.where(keep_c, 0.0, NEG_INF), c0, bias_ref)
            seen = seen + jnp.sum(eq_c, axis=0, keepdims=True)


ONES_ROWS = 16


def _with_ones(v_t):
    return jnp.concatenate([v_t, jnp.ones((ONES_ROWS, v_t.shape[1]), BF16)], axis=0)


def _normalised_pv(p, v_t):
    o = _dot_nt(p, _with_ones(v_t))
    return o[:, :HEAD_DIM] / o[:, HEAD_DIM:HEAD_DIM + 1]


def _softmax_pv(s, v_t):
    m = jnp.max(s, axis=1, keepdims=True)
    return _normalised_pv(jnp.exp2(s - m).astype(BF16), v_t)


def _attend_group(q_g, k_t, v_t, bias):
    s = _dot(q_g, k_t)
    g = s.shape[0] // bias.shape[0]
    s = (s.reshape(g, bias.shape[0], s.shape[1]) + bias[None]).reshape(s.shape)
    return _softmax_pv(s, v_t)


def _store_heads(o_g, t, first_head, n_heads, out_ref):
    for g in range(0, n_heads, 2):
        pair = jnp.concatenate([o_g[g * t:(g + 1) * t], o_g[(g + 1) * t:(g + 2) * t]], axis=1)
        c0 = (first_head + g) * HEAD_DIM
        out_ref[:, c0:c0 + 2 * HEAD_DIM] = pair.astype(out_ref.dtype)


def _dsa_attention(qa, ka_t, va_t, bias, out_ref):
    t = qa.shape[0]
    hpk = N_HEADS_A // N_KV_A
    for kv in range(N_KV_A):
        k_t = ka_t[kv * HEAD_DIM:(kv + 1) * HEAD_DIM].astype(BF16)
        v_t = va_t[kv * HEAD_DIM:(kv + 1) * HEAD_DIM].astype(BF16)
        o_g = _attend_group(_stack_heads(qa, kv * hpk, hpk), k_t, v_t, bias)
        _store_heads(o_g, t, kv * hpk, hpk, out_ref)


GATE_ROWS = 8


def _block_means(kb_t, n_full):
    lane = lax.broadcasted_iota(jnp.int32, (KV_ROWS, LANES), 1)
    km = jnp.zeros((KV_ROWS, LANES), F32)
    for n in range(n_full):
        col = jnp.sum(kb_t[:, n * MOBA_BLOCK:(n + 1) * MOBA_BLOCK], axis=1, keepdims=True) * (1.0 / MOBA_BLOCK)
        km = jnp.where(lane == n, col, km)
    return km.T


def _moba_queries(qb, kv):
    hpk = N_HEADS_B // N_KV_B
    return jnp.concatenate([qb[:, (kv * hpk + g) * HEAD_DIM:(kv * hpk + g + 1) * HEAD_DIM] for g in range(hpk)],
                           axis=0)


def _moba_pick(q_f, k_mean_kv, own, n_sel, n_full):
    assert n_full <= GATE_ROWS
    rows = q_f.shape[0]
    rows_pad = -(-rows // LANES) * LANES
    if rows_pad != rows:
        q_f = jnp.concatenate([q_f, jnp.zeros((rows_pad - rows, HEAD_DIM), F32)], axis=0)
    blk = lax.broadcasted_iota(jnp.int32, (GATE_ROWS, rows_pad), 0)
    blk_f = blk.astype(F32)
    gate = _dot_nt(k_mean_kv, q_f, precision=HIGHEST)
    gate = jnp.where(blk < jnp.minimum(own, n_full), gate, NEG_INF)
    picked_t = jnp.zeros((GATE_ROWS, rows_pad), F32)
    for _ in range(n_sel):
        best = jnp.max(gate, axis=0, keepdims=True)
        first = jnp.min(jnp.where(gate == best, blk_f, float(GATE_ROWS)), axis=0, keepdims=True)
        hit = (blk_f == first) & (best > NEG_INF)
        picked_t = jnp.where(hit, 1.0, picked_t)
        gate = jnp.where(blk_f == first, NEG_INF, gate)
    picked_t = jnp.concatenate([picked_t, jnp.zeros((LANES - GATE_ROWS, rows_pad), F32)], axis=0)
    return picked_t.T[:rows]


def _moba_scores(q_g, k_t, picked, q_pos_g, own, own_min, n_full, s_ref):
    rows, s_len = q_g.shape[0], k_t.shape[1]
    for c0 in range(0, s_len, MOBA_BLOCK):
        c = c0 // MOBA_BLOCK
        w = min(MOBA_BLOCK, s_len - c0)
        if c < own_min:
            allowed = picked[:, c:c + 1] > 0.0
        else:
            k_pos = c0 + lax.broadcasted_iota(jnp.int32, (rows, w), 1)
            allowed = (k_pos <= q_pos_g) & (own == c)
            if c < n_full:
                allowed = allowed | (picked[:, c:c + 1] > 0.0)
        s_ref[:, c0:c0 + w] = jnp.where(allowed, _dot(q_g, k_t[:, c0:c0 + w]), NEG_INF)


def _moba_attention(qb, kb_t, vb_t, k_mean, q_pos, own, own_min, n_sel, n_full, s_ref, out_ref):
    t = qb.shape[0]
    s_len = kb_t.shape[1]
    hpk = N_HEADS_B // N_KV_B
    q_pos_g = jnp.concatenate([q_pos] * hpk, axis=0)
    for kv in range(N_KV_B):
        q_f = _moba_queries(qb, kv)
        picked = _moba_pick(q_f, k_mean[:GATE_ROWS, kv * HEAD_DIM:(kv + 1) * HEAD_DIM], own, n_sel, n_full)
        k_t = kb_t[kv * HEAD_DIM:(kv + 1) * HEAD_DIM].astype(BF16)
        v_t = vb_t[kv * HEAD_DIM:(kv + 1) * HEAD_DIM].astype(BF16)
        _moba_scores((q_f * SCORE_SCALE).astype(BF16), k_t, picked, q_pos_g, own, own_min, n_full,
                     s_ref.at[:, :s_len])
        _store_heads(_softmax_pv(s_ref[:, :s_len], v_t), t, kv * hpk, hpk, out_ref)


PROMPT_TQ = 128
KEY_SEGMENT = MOBA_BLOCK


def _per_key_segment(seq, tq, body):
    n_seg = max(1, seq // KEY_SEGMENT)
    seg_len = seq // n_seg
    assert seg_len % tq == 0 and seg_len % MOBA_BLOCK == 0
    seg = (pl.program_id(1) * tq) // seg_len
    for k in range(n_seg):
        pl.when(seg == k)(functools.partial(body, (k + 1) * seg_len, seg_len))


def _dsa_prompt_kernel(qi_ref, wik_q_ref, qa_ref, wik_k_ref, ka_ref, va_ref, o_ref, bias_ref, kik_ref,
                       *, tq, seq, n_keep):
    q_pos_t = pl.program_id(1) * tq + lax.broadcasted_iota(jnp.int32, (1, tq), 1)

    @pl.when(pl.program_id(1) == 0)
    def _():
        kik_ref[...] = wik_k_ref[...].astype(BF16)

    def body(s_len, seg_len):
        if s_len <= n_keep:
            q_pos = pl.program_id(1) * tq + lax.broadcasted_iota(jnp.int32, (tq, 1), 0)
            k_pos = lax.broadcasted_iota(jnp.int32, (tq, s_len), 1)
            bias_ref[:, :s_len] = jnp.where(k_pos <= q_pos, 0.0, NEG_INF)
        else:
            score_t = _indexer_scores_t(kik_ref[:s_len, :], wik_q_ref[...], qi_ref[...], q_pos_t)
            _topk_mask_t(score_t, n_keep, bias_ref.at[:, :s_len])
        _dsa_attention(qa_ref[...], ka_ref[:, :s_len], va_ref[:, :s_len], bias_ref[:, :s_len], o_ref)

    _per_key_segment(seq, tq, body)


def _dsa_prompt(qi, wik, qa, ka_t, va_t, tq):
    batch, _, seq = ka_t.shape
    assert tq == LANES
    nq = seq // tq
    qrow = lambda w: pl.BlockSpec((tq, w), lambda b, i: (b * nq + i, 0))
    keys_t = pl.BlockSpec((None, KV_ROWS, seq), lambda b, i: (b, 0, 0))
    return pl.pallas_call(
        functools.partial(_dsa_prompt_kernel, tq=tq, seq=seq, n_keep=min(IDX_TOPK_MAX, seq // 4)),
        grid=(batch, nq),
        in_specs=[qrow(qi.shape[1]), qrow(LANES), qrow(qa.shape[1]),
                  pl.BlockSpec((seq, LANES), lambda b, i: (b, 0)), keys_t, keys_t],
        out_specs=qrow(qa.shape[1]),
        out_shape=jax.ShapeDtypeStruct(qa.shape, BF16),
        scratch_shapes=[pltpu.VMEM((tq, seq), F32), pltpu.VMEM((seq, LANES), BF16)],
        compiler_params=_cparams(2),
        name="dsa_prompt",
    )(qi, wik, qa, wik, ka_t, va_t)


def _moba_prompt_kernel(qb_ref, kb_ref, vb_ref, o_ref, bias_ref, km_ref, *, tq, seq, n_sel):
    base = pl.program_id(1) * tq
    q_pos = base + lax.broadcasted_iota(jnp.int32, (tq, 1), 0)
    n_full = seq // MOBA_BLOCK

    @pl.when(pl.program_id(1) == 0)
    def _():
        km_ref[...] = _block_means(kb_ref[...], n_full)

    def body(s_len, seg_len):
        _moba_attention(qb_ref[...], kb_ref[:, :s_len], vb_ref[:, :s_len], km_ref[...], q_pos, base // MOBA_BLOCK,
                        (s_len - seg_len) // MOBA_BLOCK, n_sel, n_full, bias_ref, o_ref)

    _per_key_segment(seq, tq, body)


def _moba_prompt(qb, kb_t, vb_t, tq):
    batch, _, seq = kb_t.shape
    assert MOBA_BLOCK % tq == 0 and seq % MOBA_BLOCK == 0
    nq = seq // tq
    qrow = pl.BlockSpec((tq, qb.shape[1]), lambda b, i: (b * nq + i, 0))
    keys = pl.BlockSpec((None, KV_ROWS, seq), lambda b, i: (b, 0, 0))
    return pl.pallas_call(
        functools.partial(_moba_prompt_kernel, tq=tq, seq=seq, n_sel=min(MOBA_TOPK_MAX, seq // MOBA_BLOCK)),
        grid=(batch, nq),
        in_specs=[qrow, keys, keys],
        out_specs=qrow,
        out_shape=jax.ShapeDtypeStruct(qb.shape, BF16),
        scratch_shapes=[pltpu.VMEM((N_HEADS_B // N_KV_B * tq, seq), F32), pltpu.VMEM((KV_ROWS, LANES), F32)],
        compiler_params=_cparams(2),
        name="moba_prompt",
    )(qb, kb_t, vb_t)


SAMPLE_ROWS = 8


def _assemble_keys(page_refs, new_ref, dst_ref, page_size):
    for j, page in enumerate(page_refs):
        dst_ref[:, j * page_size:(j + 1) * page_size] = page[...]
    dst_ref[:, len(page_refs) * page_size:] = new_ref[...]


def _page_specs(layer, n_pages, rows, page_size):
    return [pl.BlockSpec((None, None, rows, page_size), lambda b, pt, j=j: (layer, pt[b, j], 0, 0))
            for j in range(n_pages)]


def _score_sample_kernel(pt_ref, qi_ref, wi_ref, kin_ref, *rest, n_pages, page_size):
    ki_pages = rest[:n_pages]
    o_ref, ki_s = rest[n_pages:]
    _assemble_keys(ki_pages, kin_ref, ki_s, page_size)
    q_pos = n_pages * page_size + lax.broadcasted_iota(jnp.int32, (SAMPLE_ROWS, 1), 0)
    o_ref[...] = _indexer_scores(qi_ref[...], wi_ref[...], ki_s[...].astype(BF16), q_pos)


def _score_sample(qi, wi, ki_new, cache_idx_t, page_table, layer):
    b, n_pages = page_table.shape
    page_size = cache_idx_t.shape[3]
    s_len = n_pages * page_size + LANES
    per_b = lambda r, w: pl.BlockSpec((None, r, w), lambda i, pt: (i, 0, 0))
    grid_spec = pltpu.PrefetchScalarGridSpec(
        num_scalar_prefetch=1,
        grid=(b,),
        in_specs=[per_b(SAMPLE_ROWS, qi.shape[2]), per_b(SAMPLE_ROWS, LANES), per_b(D_IDX, LANES)]
        + _page_specs(layer, n_pages, D_IDX, page_size),
        out_specs=per_b(SAMPLE_ROWS, s_len),
        scratch_shapes=[pltpu.VMEM((D_IDX, s_len), F32)],
    )
    return pl.pallas_call(
        functools.partial(_score_sample_kernel, n_pages=n_pages, page_size=page_size),
        grid_spec=grid_spec,
        out_shape=jax.ShapeDtypeStruct((b, SAMPLE_ROWS, s_len), F32),
        compiler_params=_cparams(1),
        name="score_sample",
    )(page_table, qi, wi, ki_new, *([cache_idx_t] * n_pages))


def _select_kernel(score_ref, bias_ref, *, n_keep, dec_seq):
    rows = score_ref.shape[0]
    row = lax.broadcasted_iota(jnp.int32, (rows, 1), 0)
    _topk_mask(score_ref[...], n_keep, (row % SAMPLE_ROWS) < dec_seq, bias_ref)


def _select_sample(score, n_keep, dec_seq):
    n, s_len = score.shape
    tr = min(256, n)
    assert n % tr == 0 and tr % SAMPLE_ROWS == 0
    spec = pl.BlockSpec((tr, s_len), lambda i: (i, 0))
    return pl.pallas_call(
        functools.partial(_select_kernel, n_keep=n_keep, dec_seq=dec_seq),
        grid=(n // tr,),
        in_specs=[spec],
        out_specs=spec,
        out_shape=jax.ShapeDtypeStruct((n, s_len), F32),
        compiler_params=_cparams(1),
        name="select_sample",
    )(score)


SAMPLE_GROUP = 1


def _attend_sample_kernel(pt_ref, qa_ref, bias_ref, kan_ref, van_ref, qb_ref, kbn_ref, vbn_ref, *rest,
                          n_pages, page_size, n_sel):
    n_page_refs = 4 * SAMPLE_GROUP * n_pages
    ya_ref, yb_ref = rest[n_page_refs:n_page_refs + 2]
    scratch = rest[n_page_refs + 2:]
    past = n_pages * page_size
    n_full = past // MOBA_BLOCK
    hpk = N_HEADS_A // N_KV_A
    rows = hpk * SAMPLE_ROWS
    q_pos_g = past + lax.broadcasted_iota(jnp.int32, (rows, 1), 0) % SAMPLE_ROWS
    for g in range(SAMPLE_GROUP):
        ka_s, va_s, kb_s, vb_s, s_ref = scratch[5 * g:5 * g + 5]
        for k, (new_ref, dst) in enumerate(zip((kan_ref, van_ref, kbn_ref, vbn_ref), (ka_s, va_s, kb_s, vb_s))):
            first = (k * SAMPLE_GROUP + g) * n_pages
            _assemble_keys(rest[first:first + n_pages], new_ref.at[g], dst, page_size)
        qa, qb = qa_ref[g], qb_ref[g]
        bias_g = jnp.concatenate([bias_ref[g]] * hpk, axis=0)
        kb_t = kb_s[...]
        k_mean = _block_means(kb_t, n_full)
        for kv in range(N_KV_A):
            k_t = ka_s[kv * HEAD_DIM:(kv + 1) * HEAD_DIM, :].astype(BF16)
            s_ref[kv * rows:(kv + 1) * rows, :] = _dot(_stack_heads(qa, kv * hpk, hpk), k_t) + bias_g
            q_f = _moba_queries(qb, kv)
            picked = _moba_pick(q_f, k_mean[:GATE_ROWS, kv * HEAD_DIM:(kv + 1) * HEAD_DIM], n_full, n_sel, n_full)
            _moba_scores((q_f * SCORE_SCALE).astype(BF16), kb_t[kv * HEAD_DIM:(kv + 1) * HEAD_DIM].astype(BF16),
                         picked, q_pos_g, n_full, n_full, n_full,
                         s_ref.at[(N_KV_A + kv) * rows:(N_KV_A + kv + 1) * rows, :])
        s = s_ref[...]
        p = jnp.exp2(s - jnp.max(s, axis=1, keepdims=True))
        l = jnp.sum(p, axis=1, keepdims=True)
        p = p.astype(BF16)
        groups = ((va_s, ya_ref, 0), (va_s, ya_ref, 1), (vb_s, yb_ref, 0), (vb_s, yb_ref, 1))
        for idx, (v_s, out_ref, kv) in enumerate(groups):
            r0 = idx * rows
            v_t = v_s[kv * HEAD_DIM:(kv + 1) * HEAD_DIM, :].astype(BF16)
            o_g = _dot_nt(p[r0:r0 + rows], v_t) / l[r0:r0 + rows]
            _store_heads(o_g, SAMPLE_ROWS, kv * hpk, hpk, out_ref.at[g])


def _attend_sample(qa, bias, ka_new, va_new, qb, kb_new, vb_new, caches_t, page_table, layer, dec_seq):
    b, n_pages = page_table.shape
    page_size = caches_t[0].shape[3]
    past = n_pages * page_size
    assert past % MOBA_BLOCK == 0 and dec_seq <= SAMPLE_ROWS and b % SAMPLE_GROUP == 0
    s_len = past + LANES
    n_blocks = -(-(past + dec_seq) // MOBA_BLOCK)
    per_step = lambda r, w: pl.BlockSpec((SAMPLE_GROUP, r, w), lambda i, pt: (i, 0, 0))
    new = per_step(KV_ROWS, LANES)
    q = per_step(SAMPLE_ROWS, qa.shape[2])
    page_specs = [pl.BlockSpec((None, None, KV_ROWS, page_size),
                               lambda i, pt, g=g, j=j: (layer, pt[i * SAMPLE_GROUP + g, j], 0, 0))
                  for _ in range(4) for g in range(SAMPLE_GROUP) for j in range(n_pages)]
    grid_spec = pltpu.PrefetchScalarGridSpec(
        num_scalar_prefetch=1,
        grid=(b // SAMPLE_GROUP,),
        in_specs=[q, per_step(SAMPLE_ROWS, s_len), new, new, q, new, new] + page_specs,
        out_specs=[q, q],
        scratch_shapes=([pltpu.VMEM((KV_ROWS, s_len), F32)] * 4
                        + [pltpu.VMEM(((N_HEADS_A + N_HEADS_B) * SAMPLE_ROWS, s_len), F32)]) * SAMPLE_GROUP,
    )
    return pl.pallas_call(
        functools.partial(_attend_sample_kernel, n_pages=n_pages, page_size=page_size,
                          n_sel=min(MOBA_TOPK_MAX, n_blocks)),
        grid_spec=grid_spec,
        out_shape=[jax.ShapeDtypeStruct(qa.shape, F32)] * 2,
        compiler_params=_cparams(1),
        name="attend_sample",
    )(page_table, qa, bias, ka_new, va_new, qb, kb_new, vb_new,
      *[c for c in caches_t for _ in range(SAMPLE_GROUP * n_pages)])


ROUTE_E1, ROUTE_E2, ROUTE_C1, ROUTE_C2 = 0, 1, 2, 3
SUBLANES = 8


def _store_token_tiles(x, ref, lead=()):
    t = x.shape[0]
    for j in range(x.shape[1] // LANES):
        ref[lead + (pl.ds(j, t, stride=SUBLANES), slice(None))] = x[:, j * LANES:(j + 1) * LANES]


def _load_token_tiles(ref, t, lead=()):
    return jnp.concatenate([ref[lead + (pl.ds(j, t, stride=SUBLANES), slice(None))] for j in range(SUBLANES)], axis=1)


def _mixout_kernel(ya_ref, yb_ref, ga_ref, gb_ref, x_ref, gt1_ref, sc2_ref, sh2_ref, g_ref,
                   wpa_ref, wpb_ref, wout_ref, wr_ref, br_ref, x1_ref, h2_ref, route_ref):
    merged = (ga_ref[...].astype(F32) * _dot(ya_ref[...], wpa_ref[...])
              + gb_ref[...].astype(F32) * _dot(yb_ref[...], wpb_ref[...]))
    mix = _dot(merged.astype(BF16), wout_ref[...])
    x1 = x_ref[...] + gt1_ref[...] * mix
    x1_ref[...] = x1
    h2 = x1 * lax.rsqrt(jnp.mean(x1 * x1, axis=-1, keepdims=True) + RMS_EPS) * g_ref[...]
    h2 = h2 * (1.0 + sc2_ref[...]) + sh2_ref[...]
    _store_token_tiles(h2, h2_ref)

    logits = _dot(h2, wr_ref[...], precision=HIGHEST) + br_ref[...]
    lane = lax.broadcasted_iota(jnp.int32, logits.shape, 1)
    lane_f = lane.astype(F32)
    is_grp = lane < N_GROUPS
    grp = jnp.where(is_grp, logits, NEG_INF)
    g_max = jnp.max(grp, axis=1, keepdims=True)
    g_star = jnp.min(jnp.where(grp == g_max, lane_f, float(N_GROUPS)), axis=1, keepdims=True)
    p_g = 1.0 / jnp.sum(jnp.where(is_grp, jnp.exp(grp - g_max), 0.0), axis=1, keepdims=True)
    e_id = lane_f - float(N_GROUPS)
    in_grp = (e_id >= g_star * EXPERTS_PER_GROUP) & (e_id < (g_star + 1.0) * EXPERTS_PER_GROUP)
    cand = jnp.where(in_grp, logits, NEG_INF)
    l1 = jnp.max(cand, axis=1, keepdims=True)
    e1 = jnp.min(jnp.where(cand == l1, e_id, float(N_EXPERTS)), axis=1, keepdims=True)
    cand = jnp.where(e_id == e1, NEG_INF, cand)
    l2 = jnp.max(cand, axis=1, keepdims=True)
    e2 = jnp.min(jnp.where(cand == l2, e_id, float(N_EXPERTS)), axis=1, keepdims=True)
    t = jnp.exp(l2 - l1)
    c1 = p_g / (1.0 + t)
    c2 = p_g * t / (1.0 + t)
    route = jnp.where(lane == ROUTE_E1, e1, 0.0)
    route = jnp.where(lane == ROUTE_E2, e2, route)
    route = jnp.where(lane == ROUTE_C1, c1, route)
    route_ref[...] = jnp.where(lane == ROUTE_C2, c2, route)


def _mixout(ya, yb, ga, gb, x, layer, mod, mod_spec, g_ffn, w_pa, w_pb, w_out, w_r, b_r, tm):
    n, d = x.shape
    row = lambda width: pl.BlockSpec((tm, width), lambda i: (i, 0))
    layer_w = lambda a: pl.BlockSpec((None,) + a.shape[1:], lambda i: (layer, 0, 0))
    return pl.pallas_call(
        _mixout_kernel,
        grid=(n // tm,),
        in_specs=[row(ya.shape[1]), row(yb.shape[1]), row(d), row(d), row(d),
                  mod_spec(2), mod_spec(4), mod_spec(3), pl.BlockSpec((None, 1, d), lambda i: (layer, 0, 0)),
                  layer_w(w_pa), layer_w(w_pb), layer_w(w_out), layer_w(w_r), layer_w(b_r)],
        out_specs=[row(d), pl.BlockSpec((tm * SUBLANES, LANES), lambda i: (i, 0)), row(LANES)],
        out_shape=[jax.ShapeDtypeStruct((n, d), F32), jax.ShapeDtypeStruct((n * SUBLANES, LANES), F32),
                   jax.ShapeDtypeStruct((n, LANES), F32)],
        compiler_params=_cparams(1),
        name="mixout",
    )(ya, yb, ga, gb, x, mod, mod, mod, g_ffn.reshape(g_ffn.shape[0], 1, d), w_pa, w_pb, w_out, w_r, b_r)


MOE_ROWS = 256
ROW_CHUNK = 32
N_BUF = 2


def _experts_kernel(blk_e_ref, n_valid_ref, src_ref, src_row_ref, dst_row_ref, h_hbm, wg_ref, wu_ref, wd_ref, y_hbm,
                    x_buf, y_buf, gather_sem, scatter_sem, wg_b, wu_b, wd_b, *, n_tokens, n_blocks):
    i = pl.program_id(0)
    buf = i % N_BUF
    n_slots = TOPK_IN_GROUP * n_tokens

    def n_chunks(b):
        return jnp.right_shift(n_valid_ref[b] + (ROW_CHUNK - 1), ROW_CHUNK.bit_length() - 1)

    def sorted_index(b, r):
        return jnp.minimum(src_ref[b] + r, n_slots - 1)

    chunk_rows = ROW_CHUNK * SUBLANES

    def row_view(buf_ref, bf, c, u):
        return buf_ref.at[bf, pl.ds(pl.multiple_of(c * chunk_rows, chunk_rows) + u * SUBLANES, SUBLANES)]

    def chunk_view(buf_ref, bf, c):
        return buf_ref.at[bf, pl.ds(pl.multiple_of(c * chunk_rows, chunk_rows), chunk_rows)]

    def token_tile(hbm_ref, first_row):
        return hbm_ref.at[pl.ds(pl.multiple_of(first_row, SUBLANES), SUBLANES)]

    def for_rows(b, fn):
        def chunk(c, carry):
            for u in range(ROW_CHUNK):
                fn(c, u)
            return carry
        lax.fori_loop(0, n_chunks(b), chunk, 0)

    def wait_chunks(b, buf_ref, bf, sem):
        def chunk(c, carry):
            view = chunk_view(buf_ref, bf, c)
            pltpu.make_async_copy(view, view, sem.at[bf]).wait()
            return carry
        lax.fori_loop(0, n_chunks(b), chunk, 0)

    def start_gather(b, bf):
        def row(c, u):
            first_row = src_row_ref[sorted_index(b, c * ROW_CHUNK + u)]
            pltpu.make_async_copy(token_tile(h_hbm, first_row), row_view(x_buf, bf, c, u), gather_sem.at[bf]).start()
        for_rows(b, row)

    def start_scatter(b, bf):
        n_valid = n_valid_ref[b]

        def row(c, u):
            r = c * ROW_CHUNK + u
            first_row = jnp.where(r < n_valid, dst_row_ref[sorted_index(b, r)],
                                  (n_slots + bf * MOE_ROWS + r) * SUBLANES)
            pltpu.make_async_copy(row_view(y_buf, bf, c, u), token_tile(y_hbm, first_row), scatter_sem.at[bf]).start()
        for_rows(b, row)

    @pl.when(i == 0)
    def _():
        x_buf[...] = jnp.zeros_like(x_buf)
        y_buf[...] = jnp.zeros_like(y_buf)
        for bf in range(N_BUF):
            spare = y_hbm.at[pl.ds((n_slots + bf * MOE_ROWS) * SUBLANES, MOE_ROWS * SUBLANES)]
            init = pltpu.make_async_copy(y_buf.at[bf], spare, scatter_sem.at[bf])
            init.start()
            init.wait()
        start_gather(0, 0)

    @pl.when(i + 1 < n_blocks)
    def _():
        start_gather(i + 1, 1 - buf)

    wait_chunks(i, x_buf, buf, gather_sem)

    @pl.when((i == 0) | (blk_e_ref[i] != blk_e_ref[jnp.maximum(i - 1, 0)]))
    def _():
        wg_b[...] = wg_ref[...].astype(BF16)
        wu_b[...] = wu_ref[...].astype(BF16)
        wd_b[...] = wd_ref[...].astype(BF16)

    @pl.when(n_valid_ref[i] > 0)
    def _():
        xb = _load_token_tiles(x_buf, MOE_ROWS, (buf,)).astype(BF16)
        gate = _dot(xb, wg_b[...])
        hidden = gate * _sigmoid(gate) * _dot(xb, wu_b[...])
        _store_token_tiles(_dot(hidden.astype(BF16), wd_b[...]), y_buf, (buf,))

    start_scatter(i, buf)

    @pl.when(i > 0)
    def _():
        wait_chunks(i - 1, y_buf, 1 - buf, scatter_sem)

    @pl.when(i == n_blocks - 1)
    def _():
        wait_chunks(i, y_buf, buf, scatter_sem)


def _experts(h2, expert_ids, layer, w_gate, w_up, w_down):
    n = expert_ids.shape[0]
    assert h2.shape == (n * SUBLANES, LANES) and w_gate.shape[2] == SUBLANES * LANES
    m = TOPK_IN_GROUP * n
    n_blocks = -(-m // MOE_ROWS) + N_EXPERTS
    flat_e = expert_ids.T.reshape(m)
    order = jnp.argsort(flat_e).astype(jnp.int32)
    experts = jnp.arange(N_EXPERTS, dtype=jnp.int32)
    counts = jnp.sum((flat_e[:, None] == experts[None, :]).astype(jnp.int32), axis=0)
    padded = (counts + MOE_ROWS - 1) // MOE_ROWS * MOE_ROWS
    pad_end = jnp.cumsum(padded)
    pad_start = pad_end - padded
    start = jnp.cumsum(counts) - counts
    block_start = jnp.arange(n_blocks, dtype=jnp.int32) * MOE_ROWS
    blk_e = jnp.minimum(jnp.sum((pad_end[None, :] <= block_start[:, None]).astype(jnp.int32), axis=1), N_EXPERTS - 1)
    offset = block_start - pad_start[blk_e]
    n_valid = jnp.clip(counts[blk_e] - offset, 0, MOE_ROWS).astype(jnp.int32)
    src = jnp.clip(start[blk_e] + offset, 0, m - 1).astype(jnp.int32)
    blk_e = jnp.where(n_valid > 0, blk_e, jnp.max(jnp.where(n_valid > 0, blk_e, 0))).astype(jnp.int32)

    src_row = jnp.where(order >= n, order - n, order) * SUBLANES
    dst_row = order * SUBLANES

    w_spec = lambda a: pl.BlockSpec((None, None) + a.shape[2:], lambda i, be, nv, sr, s8, d8: (layer, be[i], 0, 0))
    grid_spec = pltpu.PrefetchScalarGridSpec(
        num_scalar_prefetch=5,
        grid=(n_blocks,),
        in_specs=[pl.BlockSpec(memory_space=pl.ANY), w_spec(w_gate), w_spec(w_up), w_spec(w_down)],
        out_specs=pl.BlockSpec(memory_space=pl.ANY),
        scratch_shapes=[pltpu.VMEM((N_BUF, MOE_ROWS * SUBLANES, LANES), F32)] * 2
        + [pltpu.SemaphoreType.DMA((N_BUF,))] * 2
        + [pltpu.VMEM(w.shape[2:], BF16) for w in (w_gate, w_up, w_down)],
    )
    return pl.pallas_call(
        functools.partial(_experts_kernel, n_tokens=n, n_blocks=n_blocks),
        grid_spec=grid_spec,
        out_shape=jax.ShapeDtypeStruct(((m + N_BUF * MOE_ROWS) * SUBLANES, LANES), F32),
        compiler_params=_cparams(1),
        name="experts",
    )(blk_e, n_valid, src, src_row, dst_row, h2, w_gate, w_up, w_down)


def _combine_kernel(x1_ref, y0_ref, y1_ref, route_ref, gt2_ref, g_ref, o_ref, *, final_norm):
    route = route_ref[...]
    t = route.shape[0]
    moe = (route[:, ROUTE_C1:ROUTE_C1 + 1] * _load_token_tiles(y0_ref, t)
           + route[:, ROUTE_C2:ROUTE_C2 + 1] * _load_token_tiles(y1_ref, t))
    x2 = x1_ref[...] + gt2_ref[...] * moe
    if final_norm:
        x2 = x2 * lax.rsqrt(jnp.mean(x2 * x2, axis=-1, keepdims=True) + RMS_EPS) * g_ref[...]
    o_ref[...] = x2


def _combine(x1, y, route, mod, mod_spec, g_final, tm, final_norm):
    n, d = x1.shape
    nt = n // tm
    row = lambda width: pl.BlockSpec((tm, width), lambda i: (i, 0))
    tiles = lambda first: pl.BlockSpec((tm * SUBLANES, LANES), lambda i: (first + i, 0))
    return pl.pallas_call(
        functools.partial(_combine_kernel, final_norm=final_norm),
        grid=(nt,),
        in_specs=[row(d), tiles(0), tiles(nt), row(LANES),
                  mod_spec(5), pl.BlockSpec((1, d), lambda i: (0, 0))],
        out_specs=row(d),
        out_shape=jax.ShapeDtypeStruct((n, d), F32),
        compiler_params=_cparams(1),
        name="combine",
    )(x1, y, y, route, mod, g_final.reshape(1, d))


def _rope_tables(pos):
    half = HEAD_DIM // 2
    inv = jnp.power(ROPE_THETA, -jnp.arange(half, dtype=F32) / half)
    ang = pos.astype(F32)[:, None] * inv[None, :]
    cos = jnp.tile(jnp.cos(ang), (1, LANES // half))
    sin = jnp.tile(jnp.sin(ang), (1, LANES // half))
    return cos, sin, cos.T, sin.T


def _sample_rows(a, batch, t):
    return jnp.pad(a.astype(F32).reshape(batch, t, a.shape[-1]), ((0, 0), (0, SAMPLE_ROWS - t), (0, 0)))


def _sample_new_keys(a_t, batch, t):
    r = a_t.shape[1]
    per_b = jnp.transpose(a_t.reshape(r, batch, t), (1, 0, 2))
    return jnp.pad(per_b, ((0, 0), (0, 0), (0, LANES - t)))


def _cache_t(cache):
    depth, n_pool, page = cache.shape[:3]
    return jnp.swapaxes(cache.reshape(depth, n_pool, page, -1), 2, 3)


def kernel(x_prompt, x_sample, c_prompt, c_sample, cache_k_a, cache_v_a, cache_idx_k, cache_k_b, cache_v_b, page_table, w_ada, b_ada, g_mix, w_in, w_proj_a, w_proj_b, w_out, g_ffn, w_router_group, b_router_group, w_router_expert, b_router_expert, w_exp_gate, w_exp_up, w_exp_down, g_final):
    batch, seq, d = x_prompt.shape
    dec_batch, dec_seq, _ = x_sample.shape
    depth = w_in.shape[0]
    past_len = page_table.shape[1] * cache_k_a.shape[2]
    n_p, n_s = batch * seq, dec_batch * dec_seq
    tm_p = min(512, seq)
    tm_s = min(256, n_s)
    assert seq % tm_p == 0 and n_s % tm_s == 0

    mod_all = _ada_mod(jnp.concatenate([c_prompt, c_sample], axis=0), w_ada, b_ada)
    mod_p = mod_all[:, :batch].reshape(depth, batch, 1, 6 * d)
    mod_s = jnp.repeat(mod_all[:, batch:], dec_seq, axis=1)
    w_row, w_col = _split_w_in(w_in)
    w_pa, w_pb, w_o = w_proj_a.astype(BF16), w_proj_b.astype(BF16), w_out.astype(BF16)
    pad_r = LANES - N_GROUPS - N_EXPERTS
    w_r = jnp.pad(jnp.concatenate([w_router_group, w_router_expert], axis=2), ((0, 0), (0, 0), (0, pad_r)))
    b_r = jnp.pad(jnp.concatenate([b_router_group, b_router_expert], axis=1), ((0, 0), (0, pad_r)))
    b_r = b_r.reshape(depth, 1, LANES)
    cache_idx_t = _cache_t(cache_idx_k)
    caches_t = tuple(_cache_t(c) for c in (cache_k_a, cache_v_a, cache_k_b, cache_v_b))

    trig_p = _rope_tables(jnp.arange(seq, dtype=jnp.int32))
    trig_s = _rope_tables(jnp.tile(past_len + jnp.arange(dec_seq, dtype=jnp.int32), dec_batch))
    nq_p = seq // tm_p
    trig_spec_p = pl.BlockSpec((tm_p, LANES), lambda i: (i % nq_p, 0))
    trig_t_spec_p = pl.BlockSpec((LANES, tm_p), lambda i: (0, i % nq_p))
    trig_spec_s = pl.BlockSpec((tm_s, LANES), lambda i: (i, 0))
    trig_t_spec_s = pl.BlockSpec((LANES, tm_s), lambda i: (0, i))

    xp = x_prompt.reshape(n_p, d)
    xs = x_sample.reshape(n_s, d)
    rows_p, rows_s = [], []
    for l in range(depth):
        mod_spec_p = lambda chunk, l=l: pl.BlockSpec((None, None, 1, d), lambda i: (l, i // nq_p, 0, chunk))
        mod_spec_s = lambda chunk, l=l: pl.BlockSpec((None, tm_s, d), lambda i: (l, i, chunk))
        last = l == depth - 1

        qa, qi, wi, qb, ga, gb, ka_t, va_t, ki_t, kb_t, vb_t = _inproj(
            xp, l, mod_p, mod_spec_p, g_mix, w_row, w_col, trig_p, trig_spec_p, trig_t_spec_p, batch, tm_p, BF16)
        rows_p.append((ka_t, va_t, ki_t, kb_t, vb_t))
        ya = _dsa_prompt(qi, wi, qa, ka_t, va_t, min(PROMPT_TQ, seq))
        yb = _moba_prompt(qb, kb_t, vb_t, min(PROMPT_TQ, seq))
        x1, h2, route = _mixout(ya, yb, ga, gb, xp, l, mod_p, mod_spec_p, g_ffn, w_pa, w_pb, w_o, w_r, b_r, tm_p)
        y = _experts(h2, route[:, ROUTE_E1:ROUTE_E2 + 1].astype(jnp.int32), l, w_exp_gate, w_exp_up, w_exp_down)
        xp = _combine(x1, y, route, mod_p, mod_spec_p, g_final, tm_p, last)

        qa, qi, wi, qb, ga, gb, ka_t, va_t, ki_t, kb_t, vb_t = _inproj(
            xs, l, mod_s, mod_spec_s, g_mix, w_row, w_col, trig_s, trig_spec_s, trig_t_spec_s, 1, tm_s, F32)
        rows_s.append((ka_t, va_t, ki_t, kb_t, vb_t))
        rows8 = lambda a: _sample_rows(a, dec_batch, dec_seq)
        new = lambda a_t: _sample_new_keys(a_t, dec_batch, dec_seq)
        score = _score_sample(rows8(qi), rows8(wi), new(ki_t), cache_idx_t, page_table, l)
        s_len = score.shape[-1]
        bias = _select_sample(score.reshape(dec_batch * SAMPLE_ROWS, s_len),
                              min(IDX_TOPK_MAX, (past_len + dec_seq) // 4), dec_seq)
        ya, yb = _attend_sample(rows8(qa), bias.reshape(dec_batch, SAMPLE_ROWS, s_len), new(ka_t), new(va_t),
                                rows8(qb), new(kb_t), new(vb_t), caches_t, page_table, l, dec_seq)
        unpad = lambda a: a[:, :dec_seq].reshape(n_s, a.shape[-1]).astype(BF16)
        x1, h2, route = _mixout(unpad(ya), unpad(yb), ga, gb, xs, l, mod_s, mod_spec_s, g_ffn, w_pa, w_pb, w_o,
                                w_r, b_r, tm_s)
        y = _experts(h2, route[:, ROUTE_E1:ROUTE_E2 + 1].astype(jnp.int32), l, w_exp_gate, w_exp_up, w_exp_down)
        xs = _combine(x1, y, route, mod_s, mod_spec_s, g_final, tm_s, last)

    def stack_p(i, heads):
        a = jnp.stack([r[i] for r in rows_p], axis=0)
        a = jnp.swapaxes(a, 2, 3)
        return a.reshape((depth, batch, seq, heads, HEAD_DIM) if heads else (depth, batch, seq, D_IDX))

    def stack_s(i, heads):
        a = jnp.stack([r[i][0] for r in rows_s], axis=0)
        a = jnp.swapaxes(a, 1, 2)
        return a.reshape((depth, dec_batch, dec_seq, heads, HEAD_DIM) if heads else (depth, dec_batch, dec_seq, D_IDX))

    return (xp.reshape(batch, seq, d), xs.reshape(dec_batch, dec_seq, d),
            stack_p(0, N_KV_A), stack_p(1, N_KV_A), stack_p(3, N_KV_B), stack_p(4, N_KV_B), stack_p(2, 0),
            stack_s(0, N_KV_A), stack_s(1, N_KV_A), stack_s(3, N_KV_B), stack_s(4, N_KV_B), stack_s(2, 0))
```
